```python
import jax
import jax.numpy as jnp
from jax import lax
import numpy as np

D_MODEL = 1024
BATCH = 4
SEQ = 4096
DEPTH = 4
DEC_BATCH = 128
DEC_SEQ = 1
PAST_LEN = 2048
PAGE_SIZE = 128

HEAD_DIM = 64
D_MIX = D_MODEL
POOL_CH = D_MIX // 4
POOL_WINDOWS = (2, 4, 8, 16)
POOL_GROUP = POOL_CH // 4
POOL_HIST = 15
NSA_HEADS = (3 * D_MIX // 8) // HEAD_DIM
NSA_KV_HEADS = 2
NSA_GROUP = NSA_HEADS // NSA_KV_HEADS
MOBA_HEADS = (D_MIX - POOL_CH - NSA_HEADS * HEAD_DIM) // HEAD_DIM
CMP_LEN = 32
CMP_STRIDE = 16
SEL_BLOCK = 64
SEL_TOPN = 16
WIN = 512
MOBA_BLOCK = 256
MOBA_TOPK = 3
ROPE_THETA = 500000.0
ROPE_DIM = HEAD_DIM // 4
D_FF = 11 * D_MODEL // 4
Q_CHUNK = 64
BAND_BLOCK = 128
D_IN = POOL_CH + NSA_HEADS * HEAD_DIM + 6 * NSA_KV_HEADS * HEAD_DIM + 3 * NSA_HEADS + 3 * MOBA_HEADS * HEAD_DIM
EPS = 1e-6
NEG_INF = -1e30
SEL_FORCE = 1e4

kernel_name = 'hybrid_pool_nsa_moba_macaron_step'


def rms_norm(x, g):
    xf = x.astype(jnp.float32)
    y = xf * lax.rsqrt(jnp.mean(xf * xf, axis=-1, keepdims=True) + EPS)
    return (y * g.astype(jnp.float32)).astype(x.dtype)


def swiglu(x, w_gate, w_up, w_down):
    return (jax.nn.silu(x @ w_gate) * (x @ w_up)) @ w_down


def rope_partial(x, pos):
    half = ROPE_DIM // 2
    inv_freq = ROPE_THETA ** (-jnp.arange(half, dtype=jnp.float32) / half)
    ang = pos.astype(jnp.float32)[:, None] * inv_freq[None, :]
    cos = jnp.cos(ang)[:, None, :]
    sin = jnp.sin(ang)[:, None, :]
    xr = x[..., :ROPE_DIM].astype(jnp.float32)
    x1, x2 = xr[..., :half], xr[..., half:]
    rot = jnp.concatenate([x1 * cos - x2 * sin, x2 * cos + x1 * sin], axis=-1)
    return jnp.concatenate([rot.astype(x.dtype), x[..., ROPE_DIM:]], axis=-1)


def masked_softmax(s, valid):
    p = jax.nn.softmax(jnp.where(valid, s, NEG_INF), axis=-1)
    return jnp.where(valid, p, 0.0)


def to_blocks(t, bs):
    b, l, h, d = t.shape
    nb = -(-l // bs)
    t = jnp.pad(t, ((0, 0), (0, nb * bs - l), (0, 0), (0, 0)))
    return t.reshape(b, nb, bs, h, d).transpose(0, 3, 1, 2, 4)


def pool_mix(hist, xc, pos0, w_pool, scale):
    t = xc.shape[1]
    xe = jnp.concatenate([hist.astype(jnp.float32), xc.astype(jnp.float32)], axis=1)
    cs = jnp.concatenate([jnp.zeros_like(xe[:, :1]), jnp.cumsum(xe, axis=1)], axis=1)
    pos = pos0 + jnp.arange(t, dtype=jnp.int32)
    outs = []
    for g, w in enumerate(POOL_WINDOWS):
        ch = slice(g * POOL_GROUP, (g + 1) * POOL_GROUP)
        win_sum = cs[:, POOL_HIST + 1:POOL_HIST + 1 + t, ch] - cs[:, POOL_HIST + 1 - w:POOL_HIST + 1 - w + t, ch]
        count = jnp.minimum(pos + 1, w).astype(jnp.float32)[None, :, None]
        d = win_sum / count - xe[:, POOL_HIST:, ch]
        outs.append(d @ w_pool[g].astype(jnp.float32))
    return (jnp.concatenate(outs, axis=-1) * scale.astype(jnp.float32)).astype(xc.dtype)


def nsa_compress(rows, pos_emb, w1, w2):
    nc = (rows.shape[1] - CMP_LEN) // CMP_STRIDE + 1
    idx = jnp.arange(nc)[:, None] * CMP_STRIDE + jnp.arange(CMP_LEN)[None, :]
    blk = rows[:, idx] + pos_emb[None, None, :, None, :]
    h = jax.nn.silu(jnp.einsum('bnlhd,lde->bnhe', blk, w1))
    return jnp.einsum('bnhe,ef->bnhf', h, w2)


def nsa_cmp_sel(q_cmp, q_rot, qpos, k_cmp, v_cmp, ks_blk, vs_blk):
    b, tq = q_cmp.shape[:2]
    nc = k_cmp.shape[1]
    nsb = ks_blk.shape[2]
    scale = HEAD_DIM ** -0.5
    c_start = jnp.arange(nc) * CMP_STRIDE
    c_end = c_start + CMP_LEN
    s_c = jnp.einsum('bqhgd,bnhd->bhgqn', q_cmp, k_cmp).astype(jnp.float32) * scale
    p_c = masked_softmax(s_c, c_end[None, :] <= qpos[:, None] + 1)
    o_cmp = jnp.einsum('bhgqn,bnhd->bqhgd', p_c.astype(v_cmp.dtype), v_cmp)
    j = jnp.arange(nsb)
    overlap = ((c_start[:, None] < (j[None, :] + 1) * SEL_BLOCK) & (c_end[:, None] > j[None, :] * SEL_BLOCK)).astype(jnp.float32)
    imp = jnp.einsum('bhgqn,nj->bhqj', p_c, overlap)
    own = qpos // SEL_BLOCK
    forced = (j[None, :] == 0) | (j[None, :] == own[:, None]) | (j[None, :] == own[:, None] - 1)
    imp = jnp.where(forced, imp + SEL_FORCE, imp)
    imp = jnp.where(j[None, :] <= own[:, None], imp, -jnp.inf)
    vals, sel = lax.top_k(imp, min(SEL_TOPN, nsb))
    bi = jnp.arange(b)[:, None, None, None]
    hi = jnp.arange(NSA_KV_HEADS)[None, :, None, None]
    kg = ks_blk[bi, hi, sel]
    vg = vs_blk[bi, hi, sel]
    kpos = sel[..., None] * SEL_BLOCK + jnp.arange(SEL_BLOCK)
    valid = (vals > -jnp.inf)[..., None] & (kpos <= qpos[None, None, :, None, None])
    s_s = jnp.einsum('bqhgd,bhqnkd->bhgqnk', q_rot, kg).astype(jnp.float32) * scale
    s_s = s_s.reshape(s_s.shape[:4] + (-1,))
    p_s = masked_softmax(s_s, valid.reshape(b, NSA_KV_HEADS, 1, tq, -1))
    o_sel = jnp.einsum('bhgqm,bhqmd->bqhgd', p_s.astype(vg.dtype), vg.reshape(b, NSA_KV_HEADS, tq, -1, HEAD_DIM))
    return o_cmp, o_sel


def window_attn(q, qpos, k, v, kpos):
    s = jnp.einsum('bnqhgd,bnkhd->bnhgqk', q, k).astype(jnp.float32) * HEAD_DIM ** -0.5
    rel = qpos[:, :, None] - kpos[:, None, :]
    valid = (rel >= 0) & (rel < WIN) & (kpos[:, None, :] >= 0)
    p = masked_softmax(s, valid[None, :, None, None])
    return jnp.einsum('bnhgqk,bnkhd->bnqhgd', p.astype(v.dtype), v)


def moba_attn(q, qpos, k_blk, v_blk, k_mean):
    b, tq = q.shape[:2]
    nb = k_blk.shape[2]
    own = qpos // MOBA_BLOCK
    gate = jnp.einsum('bqhd,bhnd->bhqn', q, k_mean).astype(jnp.float32)
    gate = jnp.where(jnp.arange(nb)[None, :] < own[:, None], gate, -jnp.inf)
    vals, sel = lax.top_k(gate, min(MOBA_TOPK, nb))
    own_b = jnp.broadcast_to(own[None, None, :, None], (b, MOBA_HEADS, tq, 1)).astype(sel.dtype)
    blocks = jnp.concatenate([own_b, sel], axis=-1)
    ok = jnp.concatenate([jnp.ones((b, MOBA_HEADS, tq, 1), bool), vals > -jnp.inf], axis=-1)
    bi = jnp.arange(b)[:, None, None, None]
    hi = jnp.arange(MOBA_HEADS)[None, :, None, None]
    kg = k_blk[bi, hi, blocks]
    vg = v_blk[bi, hi, blocks]
    kpos = blocks[..., None] * MOBA_BLOCK + jnp.arange(MOBA_BLOCK)
    valid = ok[..., None] & (kpos <= qpos[None, None, :, None, None])
    s = jnp.einsum('bqhd,bhqnkd->bhqnk', q, kg).astype(jnp.float32) * HEAD_DIM ** -0.5
    p = masked_softmax(s.reshape(b, MOBA_HEADS, tq, -1), valid.reshape(b, MOBA_HEADS, tq, -1))
    return jnp.einsum('bhqm,bhqmd->bqhd', p.astype(vg.dtype), vg.reshape(b, MOBA_HEADS, tq, -1, HEAD_DIM))


def prep(u, pos, lp):
    b, t, _ = u.shape
    z = u @ lp['w_in']
    sizes = (POOL_CH, NSA_HEADS * HEAD_DIM) + (NSA_KV_HEADS * HEAD_DIM,) * 6 + (3 * NSA_HEADS,) + (MOBA_HEADS * HEAD_DIM,) * 3
    cuts = np.cumsum(sizes)[:-1].tolist()
    xpool, nq, kc, vc, ks, vs, kw, vw, gl, mq, mk, mv = jnp.split(z, cuts, axis=-1)
    heads = lambda a: a.reshape(b, t, -1, HEAD_DIM)
    grp = lambda a: a.reshape(b, t, NSA_KV_HEADS, NSA_GROUP, HEAD_DIM)
    nq = rms_norm(heads(nq), lp['g_nsa_q'])
    return {
        'xpool': xpool,
        'q_cmp': grp(nq),
        'q_rot': grp(rope_partial(nq, pos)),
        'kc': heads(kc), 'vc': heads(vc),
        'ks': rope_partial(rms_norm(heads(ks), lp['g_nsa_ks']), pos), 'vs': heads(vs),
        'kw': rope_partial(rms_norm(heads(kw), lp['g_nsa_kw']), pos), 'vw': heads(vw),
        'gate': jax.nn.sigmoid(gl.astype(jnp.float32)).astype(u.dtype).reshape(b, t, NSA_KV_HEADS, NSA_GROUP, 3),
        'mq': rope_partial(rms_norm(heads(mq), lp['g_moba_q']), pos),
        'mk': rope_partial(rms_norm(heads(mk), lp['g_moba_k']), pos),
        'mv': heads(mv),
    }


def kv_side(kc, vc, ks, vs, mk, mv, lp):
    k_cmp = rms_norm(nsa_compress(kc, lp['cmp_pos'][0], lp['cmp_w1'][0], lp['cmp_w2'][0]), lp['g_nsa_kc'])
    v_cmp = nsa_compress(vc, lp['cmp_pos'][1], lp['cmp_w1'][1], lp['cmp_w2'][1])
    mk_blk = to_blocks(mk, MOBA_BLOCK)
    mk_mean = jnp.mean(mk_blk.astype(jnp.float32), axis=3).astype(mk.dtype)
    return (k_cmp, v_cmp, to_blocks(ks, SEL_BLOCK), to_blocks(vs, SEL_BLOCK), mk_blk, to_blocks(mv, MOBA_BLOCK), mk_mean)


def combine(pr, y_pool, o_cmp, o_sel, o_win, o_moba, w_out):
    b, t = y_pool.shape[:2]
    g = pr['gate']
    o_nsa = g[..., 0:1] * o_cmp + g[..., 1:2] * o_sel + g[..., 2:3] * o_win
    y = jnp.concatenate([y_pool, o_nsa.reshape(b, t, -1), o_moba.reshape(b, t, -1)], axis=-1)
    return y @ w_out


def mix_prompt(u, lp):
    b, t, _ = u.shape
    pos = jnp.arange(t, dtype=jnp.int32)
    pr = prep(u, pos, lp)
    y_pool = pool_mix(jnp.zeros((b, POOL_HIST, POOL_CH), pr['xpool'].dtype), pr['xpool'], 0, lp['pool_w'], lp['pool_scale'])
    k_cmp, v_cmp, ks_blk, vs_blk, mk_blk, mv_blk, mk_mean = kv_side(pr['kc'], pr['vc'], pr['ks'], pr['vs'], pr['mk'], pr['mv'], lp)

    def chunk(c):
        s = c * Q_CHUNK
        qp = s + jnp.arange(Q_CHUNK, dtype=jnp.int32)
        sl = lambda a: lax.dynamic_slice_in_dim(a, s, Q_CHUNK, axis=1)
        o_c, o_s = nsa_cmp_sel(sl(pr['q_cmp']), sl(pr['q_rot']), qp, k_cmp, v_cmp, ks_blk, vs_blk)
        o_m = moba_attn(sl(pr['mq']), qp, mk_blk, mv_blk, mk_mean)
        return o_c, o_s, o_m

    o_c, o_s, o_m = lax.map(chunk, jnp.arange(t // Q_CHUNK, dtype=jnp.int32))
    unchunk = lambda a: jnp.moveaxis(a, 0, 1).reshape((b, t) + a.shape[3:])
    nb = t // BAND_BLOCK
    idx = jnp.arange(nb)[:, None] * BAND_BLOCK + jnp.arange(BAND_BLOCK + WIN)[None, :]
    band = lambda a: jnp.pad(a, ((0, 0), (WIN, 0), (0, 0), (0, 0)))[:, idx]
    q_w = pr['q_rot'].reshape(b, nb, BAND_BLOCK, NSA_KV_HEADS, NSA_GROUP, HEAD_DIM)
    o_w = window_attn(q_w, pos.reshape(nb, BAND_BLOCK), band(pr['kw']), band(pr['vw']), idx - WIN)
    o_w = o_w.reshape(b, t, NSA_KV_HEADS, NSA_GROUP, HEAD_DIM)
    y = combine(pr, y_pool, unchunk(o_c), unchunk(o_s), o_w, unchunk(o_m), lp['w_out'])
    nsa_rows = jnp.stack([pr['kc'], pr['vc'], pr['ks'], pr['vs']], axis=2)
    moba_rows = jnp.stack([pr['mk'], pr['mv']], axis=2)
    win_rows = jnp.stack([pr['kw'], pr['vw']], axis=2)[:, -min(WIN, t):]
    pool_rows = pr['xpool'][:, -POOL_HIST:]
    return y, nsa_rows, moba_rows, win_rows, pool_rows


def mix_sample(u, lp, nsa_past, moba_past, win_buf, pool_buf):
    b, t, _ = u.shape
    p_len = nsa_past.shape[1]
    pos = p_len + jnp.arange(t, dtype=jnp.int32)
    pr = prep(u, pos, lp)
    y_pool = pool_mix(pool_buf, pr['xpool'], p_len, lp['pool_w'], lp['pool_scale'])
    cat = lambda past, new: jnp.concatenate([past, new], axis=1)
    k_cmp, v_cmp, ks_blk, vs_blk, mk_blk, mv_blk, mk_mean = kv_side(
        cat(nsa_past[:, :, 0], pr['kc']), cat(nsa_past[:, :, 1], pr['vc']),
        cat(nsa_past[:, :, 2], pr['ks']), cat(nsa_past[:, :, 3], pr['vs']),
        cat(moba_past[:, :, 0], pr['mk']), cat(moba_past[:, :, 1], pr['mv']), lp)
    o_c, o_s = nsa_cmp_sel(pr['q_cmp'], pr['q_rot'], pos, k_cmp, v_cmp, ks_blk, vs_blk)
    o_m = moba_attn(pr['mq'], pos, mk_blk, mv_blk, mk_mean)
    wb = win_buf.shape[1]
    kpos = (p_len - wb + jnp.arange(wb + t, dtype=jnp.int32))[None]
    o_w = window_attn(pr['q_rot'][:, None], pos[None], cat(win_buf[:, :, 0], pr['kw'])[:, None],
                      cat(win_buf[:, :, 1], pr['vw'])[:, None], kpos)[:, 0]
    y = combine(pr, y_pool, o_c, o_s, o_w, o_m, lp['w_out'])
    nsa_rows = jnp.stack([pr['kc'], pr['vc'], pr['ks'], pr['vs']], axis=2)
    moba_rows = jnp.stack([pr['mk'], pr['mv']], axis=2)
    win_rows = cat(win_buf, jnp.stack([pr['kw'], pr['vw']], axis=2))[:, -wb:]
    pool_rows = cat(pool_buf, pr['xpool'])[:, -POOL_HIST:]
    return y, nsa_rows, moba_rows, win_rows, pool_rows


def setup_inputs(seed: int = 0) -> dict:
    key = jax.random.key(seed)
    keys = iter(jax.random.split(key, 40))
    n_pages = PAST_LEN // PAGE_SIZE
    n_used = DEC_BATCH * n_pages
    n_pool = n_used + n_used // 4
    win_rows = min(WIN, PAST_LEN)

    def nrm(shape, s):
        return jax.random.normal(next(keys), shape, jnp.float32) * s

    def gain(shape):
        return 1.0 + nrm(shape, 0.1)

    inputs = {}
    inputs['x_prompt'] = nrm((BATCH, SEQ, D_MODEL), 1.0)
    inputs['x_sample'] = nrm((DEC_BATCH, DEC_SEQ, D_MODEL), 1.0)
    inputs['cache_nsa_kv'] = nrm((DEPTH, n_pool, PAGE_SIZE, 4, NSA_KV_HEADS, HEAD_DIM), 1.0)
    inputs['cache_moba_kv'] = nrm((DEPTH, n_pool, PAGE_SIZE, 2, MOBA_HEADS, HEAD_DIM), 1.0)
    inputs['state_nsa_win'] = nrm((DEPTH, DEC_BATCH, win_rows, 2, NSA_KV_HEADS, HEAD_DIM), 1.0)
    inputs['state_pool'] = nrm((DEPTH, DEC_BATCH, POOL_HIST, POOL_CH), 1.0)
    inputs['page_table'] = jax.random.permutation(next(keys), n_pool)[:n_used].reshape(DEC_BATCH, n_pages).astype(jnp.int32)
    inputs['g_ffa'] = gain((DEPTH, D_MODEL))
    inputs['w_ffa_gate'] = nrm((DEPTH, D_MODEL, D_FF), D_MODEL ** -0.5)
    inputs['w_ffa_up'] = nrm((DEPTH, D_MODEL, D_FF), D_MODEL ** -0.5)
    inputs['w_ffa_down'] = nrm((DEPTH, D_FF, D_MODEL), D_FF ** -0.5)
    inputs['g_mix'] = gain((DEPTH, D_MODEL))
    inputs['w_in'] = nrm((DEPTH, D_MODEL, D_IN), D_MODEL ** -0.5)
    inputs['w_out'] = nrm((DEPTH, D_MIX, D_MODEL), D_MIX ** -0.5)
    inputs['pool_w'] = nrm((DEPTH, len(POOL_WINDOWS), POOL_GROUP, POOL_GROUP), POOL_GROUP ** -0.5)
    inputs['pool_scale'] = gain((DEPTH, POOL_CH))
    inputs['g_nsa_q'] = gain((DEPTH, HEAD_DIM))
    inputs['g_nsa_kc'] = gain((DEPTH, HEAD_DIM))
    inputs['g_nsa_ks'] = gain((DEPTH, HEAD_DIM))
    inputs['g_nsa_kw'] = gain((DEPTH, HEAD_DIM))
    inputs['cmp_pos'] = nrm((DEPTH, 2, CMP_LEN, HEAD_DIM), 0.1)
    inputs['cmp_w1'] = nrm((DEPTH, 2, CMP_LEN, HEAD_DIM, HEAD_DIM), (CMP_LEN * HEAD_DIM) ** -0.5)
    inputs['cmp_w2'] = nrm((DEPTH, 2, HEAD_DIM, HEAD_DIM), HEAD_DIM ** -0.5)
    inputs['g_moba_q'] = gain((DEPTH, HEAD_DIM))
    inputs['g_moba_k'] = gain((DEPTH, HEAD_DIM))
    inputs['g_ffb'] = gain((DEPTH, D_MODEL))
    inputs['w_ffb_gate'] = nrm((DEPTH, D_MODEL, D_FF), D_MODEL ** -0.5)
    inputs['w_ffb_up'] = nrm((DEPTH, D_MODEL, D_FF), D_MODEL ** -0.5)
    inputs['w_ffb_down'] = nrm((DEPTH, D_FF, D_MODEL), D_FF ** -0.5)
    return inputs


def reference(x_prompt, x_sample, cache_nsa_kv, cache_moba_kv, state_nsa_win, state_pool, page_table,
              g_ffa, w_ffa_gate, w_ffa_up, w_ffa_down, g_mix, w_in, w_out, pool_w, pool_scale,
              g_nsa_q, g_nsa_kc, g_nsa_ks, g_nsa_kw, cmp_pos, cmp_w1, cmp_w2, g_moba_q, g_moba_k,
              g_ffb, w_ffb_gate, w_ffb_up, w_ffb_down):
    n_seq, n_pages = page_table.shape
    past = n_pages * cache_nsa_kv.shape[2]
    xp, xs = x_prompt, x_sample
    nsa_p, nsa_s, moba_p, moba_s, win_p, win_s, pool_p, pool_s = [], [], [], [], [], [], [], []
    for l in range(DEPTH):
        lp = {'w_in': w_in[l], 'w_out': w_out[l], 'pool_w': pool_w[l], 'pool_scale': pool_scale[l],
              'g_nsa_q': g_nsa_q[l], 'g_nsa_kc': g_nsa_kc[l], 'g_nsa_ks': g_nsa_ks[l], 'g_nsa_kw': g_nsa_kw[l],
              'cmp_pos': cmp_pos[l], 'cmp_w1': cmp_w1[l], 'cmp_w2': cmp_w2[l],
              'g_moba_q': g_moba_q[l], 'g_moba_k': g_moba_k[l]}
        xp = xp + 0.5 * swiglu(rms_norm(xp, g_ffa[l]), w_ffa_gate[l], w_ffa_up[l], w_ffa_down[l])
        xs = xs + 0.5 * swiglu(rms_norm(xs, g_ffa[l]), w_ffa_gate[l], w_ffa_up[l], w_ffa_down[l])
        yp, a_nsa, a_moba, a_win, a_pool = mix_prompt(rms_norm(xp, g_mix[l]), lp)
        nsa_past = cache_nsa_kv[l, page_table].reshape(n_seq, past, 4, NSA_KV_HEADS, HEAD_DIM)
        moba_past = cache_moba_kv[l, page_table].reshape(n_seq, past, 2, MOBA_HEADS, HEAD_DIM)
        ys, s_nsa, s_moba, s_win, s_pool = mix_sample(rms_norm(xs, g_mix[l]), lp, nsa_past, moba_past,
                                                      state_nsa_win[l], state_pool[l])
        xp = xp + yp
        xs = xs + ys
        xp = xp + 0.5 * swiglu(rms_norm(xp, g_ffb[l]), w_ffb_gate[l], w_ffb_up[l], w_ffb_down[l])
        xs = xs + 0.5 * swiglu(rms_norm(xs, g_ffb[l]), w_ffb_gate[l], w_ffb_up[l], w_ffb_down[l])
        nsa_p.append(a_nsa)
        nsa_s.append(s_nsa)
        moba_p.append(a_moba)
        moba_s.append(s_moba)
        win_p.append(a_win)
        win_s.append(s_win)
        pool_p.append(a_pool)
        pool_s.append(s_pool)
    new_nsa_kv_prompt = jnp.stack(nsa_p, axis=0)
    new_nsa_kv_sample = jnp.stack(nsa_s, axis=0)
    new_moba_kv_prompt = jnp.stack(moba_p, axis=0)
    new_moba_kv_sample = jnp.stack(moba_s, axis=0)
    new_nsa_win_prompt = jnp.stack(win_p, axis=0)
    new_nsa_win_sample = jnp.stack(win_s, axis=0)
    new_pool_prompt = jnp.stack(pool_p, axis=0)
    new_pool_sample = jnp.stack(pool_s, axis=0)
    return (xp, xs, new_nsa_kv_prompt, new_nsa_kv_sample, new_moba_kv_prompt, new_moba_kv_sample,
            new_nsa_win_prompt, new_nsa_win_sample, new_pool_prompt, new_pool_sample)
```

```python
import functools

import numpy as np
import jax
import jax.numpy as jnp
from jax import lax
from jax.experimental import pallas as pl
from jax.experimental.pallas import tpu as pltpu

F32 = jnp.float32
BF = jnp.bfloat16

HEAD_DIM = 64
LANES = 128
POOL_CH = 256
POOL_WINDOWS = (2, 4, 8, 16)
POOL_GROUP = 64
POOL_HIST = 15
NSA_KV_HEADS = 2
NSA_GROUP = 3
NSA_HEADS = 6
MOBA_HEADS = 6
CMP_LEN = 32
CMP_STRIDE = 16
SEL_BLOCK = 64
SEL_TOPN = 16
WIN = 512
MOBA_BLOCK = 256
MOBA_TOPK = 3
ROPE_THETA = 500000.0
ROPE_DIM = 16
EPS = 1e-6
NEG = -1e30
M_INIT = -1e29
SEL_FORCE = 1e4
SCALE = HEAD_DIM ** -0.5
GATE_COLS = 3 * NSA_HEADS
D_QKV = 2560
D_IN_PAD = D_QKV + LANES
VMEM_LIMIT = 56 * 1024 * 1024


def _cparams(*sem):
    return pltpu.CompilerParams(dimension_semantics=sem, vmem_limit_bytes=VMEM_LIMIT)


def _dot(a, b):
    return jnp.dot(a, b, preferred_element_type=F32)


def _dot_nt(a, b):
    return lax.dot_general(a, b, (((1,), (1,)), ((), ())), preferred_element_type=F32)


def _split3(x):
    hi = x.astype(BF)
    r = x - hi.astype(F32)
    mid = r.astype(BF)
    lo = (r - mid.astype(F32)).astype(BF)
    return hi, mid, lo


def _dot_sel(x, m):
    hi, mid, lo = _split3(x)
    return _dot(hi, m) + _dot(mid, m) + _dot(lo, m)


def _dot3_nt(a, b):
    ah = a.astype(BF)
    al = (a - ah.astype(F32)).astype(BF)
    bh = b.astype(BF)
    bl = (b - bh.astype(F32)).astype(BF)
    return _dot_nt(ah, bh) + _dot_nt(ah, bl) + _dot_nt(al, bh)


def _sigmoid(x):
    return 1.0 / (1.0 + jnp.exp(-x))


def _silu(x):
    return x * _sigmoid(x)


def _rms_rows(x, g):
    return x * lax.rsqrt(jnp.mean(x * x, axis=-1, keepdims=True) + EPS) * g


def _ones_bd():
    r = lax.broadcasted_iota(jnp.int32, (LANES, LANES), 0) >> 6
    c = lax.broadcasted_iota(jnp.int32, (LANES, LANES), 1) >> 6
    return jnp.where(r == c, 1.0, 0.0).astype(BF)


def _head_norm(x, gain, ones_bd):
    ms = _dot_sel(x * x, ones_bd) * (1.0 / HEAD_DIM)
    return x * lax.rsqrt(ms + EPS) * gain


def _rope(x, c, a, b):
    return x * c + pltpu.roll(x, LANES - ROPE_DIM // 2, axis=1) * a + pltpu.roll(x, ROPE_DIM // 2, axis=1) * b


def _softmax_rows(s, valid):
    s = jnp.where(valid, s, NEG)
    m = jnp.max(s, axis=1, keepdims=True)
    e = jnp.where(valid, jnp.exp(s - m), 0.0)
    den = jnp.sum(e, axis=1, keepdims=True)
    return jnp.where(den > 0.0, e / den, 0.0)


def _flash_step(q, k, v, ok, carry):
    m, l, acc = carry
    s = _dot_nt(q, k) * SCALE
    s = jnp.where(ok, s, NEG)
    m_new = jnp.maximum(m, jnp.max(s, axis=1, keepdims=True))
    alpha = jnp.exp(m - m_new)
    p = jnp.exp(s - m_new)
    l = alpha * l + jnp.sum(p, axis=1, keepdims=True)
    acc = alpha * acc + _dot(p.astype(BF), v)
    return m_new, l, acc


def _flash_init(rows, width):
    return (jnp.full((rows, 1), M_INIT, F32), jnp.zeros((rows, 1), F32), jnp.zeros((rows, width), F32))


def _flash_out(carry):
    _, l, acc = carry
    return jnp.where(l > 0.0, acc / l, 0.0)


def _ffn_body(x, g, wg_ref, wu_ref, wd_ref, f_chunk):
    xn = _rms_rows(x, g).astype(BF)
    d_ff = wg_ref.shape[1]
    acc = jnp.zeros(x.shape, F32)
    for c in range(d_ff // f_chunk):
        sl = slice(c * f_chunk, (c + 1) * f_chunk)
        gate = _dot(xn, wg_ref[:, sl])
        up = _dot(xn, wu_ref[:, sl])
        h = (_silu(gate) * up).astype(BF)
        acc = acc + _dot(h, wd_ref[sl, :])
    return x + 0.5 * acc


def _ffn_kernel(x_ref, g_ref, wg_ref, wu_ref, wd_ref, o_ref, *, f_chunk):
    o_ref[...] = _ffn_body(x_ref[...], g_ref[...], wg_ref, wu_ref, wd_ref, f_chunk)


def _f_chunk(d_ff):
    for c in (1408, 1024, 512, 256, 128):
        if d_ff % c == 0:
            return c
    return d_ff


def _const_spec(shape, layer):
    nd = len(shape)
    return pl.BlockSpec((None,) + tuple(shape), lambda *_: (layer,) + (0,) * nd)


def _ffn(x, g, wg, wu, wd, layer, tm):
    n, d = x.shape
    d_ff = wg.shape[2]
    return pl.pallas_call(
        functools.partial(_ffn_kernel, f_chunk=_f_chunk(d_ff)),
        grid=(pl.cdiv(n, tm),),
        in_specs=[pl.BlockSpec((tm, d), lambda i: (i, 0)),
                  _const_spec((1, d), layer),
                  _const_spec((d, d_ff), layer), _const_spec((d, d_ff), layer), _const_spec((d_ff, d), layer)],
        out_specs=pl.BlockSpec((tm, d), lambda i: (i, 0)),
        out_shape=jax.ShapeDtypeStruct((n, d), F32),
        compiler_params=_cparams("parallel"),
        name="ffn",
    )(x, g, wg, wu, wd)


def _prep_kernel(x_ref, g_ref, w_ref, gains_ref, c_ref, a_ref, b_ref,
                 xpool_ref, qc_ref, qr_ref, nsa_ref, nsab_ref, win_ref, winb_ref,
                 mq_ref, moba_ref, mobab_ref, gate_ref):
    xn = _rms_rows(x_ref[...], g_ref[...]).astype(BF)
    z = _dot(xn, w_ref[...])
    ones_bd = _ones_bd()
    c, a, b = c_ref[...], a_ref[...], b_ref[...]
    gains = gains_ref[...]
    slab = lambda i: z[:, i * LANES:(i + 1) * LANES]
    norm_rope = lambda s, gi: _rope(_head_norm(s, gains[gi:gi + 1], ones_bd), c, a, b)

    xpool_ref[...] = z[:, 0:POOL_CH]
    for gi in range(3):
        qn = _head_norm(slab(2 + gi), gains[0:1], ones_bd)
        qc_ref[:, gi * LANES:(gi + 1) * LANES] = qn
        qr_ref[:, gi * LANES:(gi + 1) * LANES] = _rope(qn, c, a, b).astype(BF)
    nsa = [slab(5), slab(6), norm_rope(slab(7), 1), slab(8)]
    for i, s in enumerate(nsa):
        nsa_ref[:, i * LANES:(i + 1) * LANES] = s
        nsab_ref[:, i * LANES:(i + 1) * LANES] = s.astype(BF)
    win = [norm_rope(slab(9), 2), slab(10)]
    for i, s in enumerate(win):
        win_ref[:, i * LANES:(i + 1) * LANES] = s
        winb_ref[:, i * LANES:(i + 1) * LANES] = s.astype(BF)
    for i in range(3):
        mq_ref[:, i * LANES:(i + 1) * LANES] = norm_rope(slab(11 + i), 3).astype(BF)
    for i in range(6):
        s = norm_rope(slab(14 + i), 4) if i < 3 else slab(14 + i)
        moba_ref[:, i * LANES:(i + 1) * LANES] = s
        mobab_ref[:, i * LANES:(i + 1) * LANES] = s.astype(BF)
    gate_ref[...] = _sigmoid(slab(20))


def _prep(x, g, w_in, gains, rope_c, rope_a, rope_b, layer, tm):
    n, d = x.shape
    row = lambda w: pl.BlockSpec((tm, w), lambda i: (i, 0))
    widths_dtypes = [(POOL_CH, F32), (384, F32), (384, BF), (512, F32), (512, BF), (256, F32), (256, BF),
                     (384, BF), (768, F32), (768, BF), (LANES, F32)]
    return pl.pallas_call(
        _prep_kernel,
        grid=(pl.cdiv(n, tm),),
        in_specs=[row(d), _const_spec((1, d), layer), _const_spec((d, D_IN_PAD), layer),
                  _const_spec((8, LANES), layer), row(LANES), row(LANES), row(LANES)],
        out_specs=[row(w) for w, _ in widths_dtypes],
        out_shape=[jax.ShapeDtypeStruct((n, w), dt) for w, dt in widths_dtypes],
        compiler_params=_cparams("parallel"),
        name="prep",
    )(x, g, w_in, gains, rope_c, rope_a, rope_b)


def _compress_tokens(load_rows, n_groups, pos_ref, w1_ref, w2_ref):
    pa = jnp.zeros((n_groups, LANES), F32)
    pb = jnp.zeros((n_groups, LANES), F32)
    for l in range(CMP_STRIDE):
        xl = load_rows(l)
        pa = pa + _dot((xl + pos_ref[l:l + 1, :]).astype(BF), w1_ref[l])
        pb = pb + _dot((xl + pos_ref[CMP_STRIDE + l:CMP_STRIDE + l + 1, :]).astype(BF), w1_ref[CMP_STRIDE + l])
    pre = pa + pltpu.roll(pb, n_groups - 1, axis=0)
    return _dot(_silu(pre).astype(BF), w2_ref[...])


def _compress_prompt_kernel(kc_ref, vc_ref, posk_ref, w1k_ref, w2k_ref, posv_ref, w1v_ref, w2v_ref, gkc_ref,
                            kcmp_ref, vcmp_ref, *, n_groups):
    kraw = _compress_tokens(lambda l: kc_ref[pl.ds(l, n_groups, stride=CMP_STRIDE), :],
                            n_groups, posk_ref, w1k_ref, w2k_ref)
    kcmp_ref[...] = _head_norm(kraw, gkc_ref[...], _ones_bd())
    vcmp_ref[...] = _compress_tokens(lambda l: vc_ref[pl.ds(l, n_groups, stride=CMP_STRIDE), :],
                                     n_groups, posv_ref, w1v_ref, w2v_ref)


def _cmp_weight_specs(layer):
    return [_const_spec((CMP_LEN, LANES), layer), _const_spec((CMP_LEN, LANES, LANES), layer),
            _const_spec((LANES, LANES), layer)]


def _compress_prompt(nsa_rows, cw, layer, batch, t):
    n_groups = t // CMP_STRIDE
    return pl.pallas_call(
        functools.partial(_compress_prompt_kernel, n_groups=n_groups),
        grid=(batch,),
        in_specs=[pl.BlockSpec((t, LANES), lambda b: (b, 0)), pl.BlockSpec((t, LANES), lambda b: (b, 1))]
                 + _cmp_weight_specs(layer) + _cmp_weight_specs(layer) + [_const_spec((1, LANES), layer)],
        out_specs=[pl.BlockSpec((n_groups, LANES), lambda b: (b, 0))] * 2,
        out_shape=[jax.ShapeDtypeStruct((batch * n_groups, LANES), F32)] * 2,
        compiler_params=_cparams("parallel"),
        name="compress_prompt",
    )(nsa_rows, nsa_rows, cw["posk"], cw["w1k"], cw["w2k"], cw["posv"], cw["w1v"], cw["w2v"], cw["gkc"])


def _rank_select(imp, j, own, n_blocks, top_n):
    forced = (j == 0) | (j == own) | (j == own - 1)
    imp = jnp.where(forced, imp + SEL_FORCE, imp)
    imp = jnp.where(j <= own, imp, -jnp.inf)
    cnt = jnp.zeros(imp.shape, F32)
    for d in range(1, SEL_BLOCK):
        if d > n_blocks - 1 and d < SEL_BLOCK - (n_blocks - 1):
            continue
        nowrap = j >= d
        r = jnp.where(nowrap, pltpu.roll(imp, d, axis=1), pltpu.roll(imp, (d - SEL_BLOCK) % LANES, axis=1))
        cnt = cnt + jnp.where(nowrap, jnp.where(r >= imp, 1.0, 0.0), jnp.where(r > imp, 1.0, 0.0))
    return jnp.where(j <= own, jnp.where(cnt < top_n, 1.0, 0.0), 0.0)


def _nsa_prompt_kernel(qc_ref, qr_ref, gate_ref, kcmp_ref, vcmp_ref, nsab_ref, winb_ref, ovl_ref, gexp_ref,
                       o_ref, *, tq, tk_sel, tk_win, n_sel_blocks):
    q0 = pl.program_id(1) * tq
    rows = NSA_GROUP * tq
    lane = lax.broadcasted_iota(jnp.int32, (1, LANES), 1)
    lane_h = lane >> 6
    j = lane & (SEL_BLOCK - 1)
    qpos = q0 + lax.broadcasted_iota(jnp.int32, (tq, 1), 0)
    qpos3 = jnp.concatenate([qpos] * NSA_GROUP, axis=0)

    def stack_q(ref, h, zero):
        return jnp.concatenate(
            [jnp.where(lane_h == h, ref[:, g * LANES:(g + 1) * LANES], zero) for g in range(NSA_GROUP)], axis=0)

    kc = kcmp_ref[...]
    vc = vcmp_ref[...].astype(BF)
    n_cmp = kc.shape[0]
    c_end = lax.broadcasted_iota(jnp.int32, (1, n_cmp), 1) * CMP_STRIDE + CMP_LEN
    valid_c = c_end <= qpos3 + 1
    o_cmp, psum = [], []
    for h in range(NSA_KV_HEADS):
        p = _softmax_rows(_dot3_nt(stack_q(qc_ref, h, 0.0), kc) * SCALE, valid_c)
        o_cmp.append(_dot(p.astype(BF), vc))
        psum.append(p[0:tq] + p[tq:2 * tq] + p[2 * tq:3 * tq])
    imp = _dot_sel(jnp.concatenate(psum, axis=1), ovl_ref[...])
    sel = _rank_select(imp, j, qpos >> 6, n_sel_blocks, SEL_TOPN)

    lrow = lax.broadcasted_iota(jnp.int32, (LANES, 1), 0) & (SEL_BLOCK - 1)
    c_hi_sel = (q0 + tq - 1) // tk_sel + 1
    c_lo_win = jnp.maximum(q0 - (WIN - 1), 0) // tk_win
    c_hi_win = (q0 + tq - 1) // tk_win + 1
    o_sel, o_win = [], []
    for h in range(NSA_KV_HEADS):
        q = stack_q(qr_ref, h, jnp.zeros((), BF))
        sel3 = jnp.concatenate([jnp.where(lane_h == h, sel, 0.0).astype(BF)] * NSA_GROUP, axis=0)

        def sel_step(c, carry):
            start = pl.multiple_of(c * tk_sel, tk_sel)
            k = nsab_ref[pl.ds(start, tk_sel), 2 * LANES:3 * LANES]
            v = nsab_ref[pl.ds(start, tk_sel), 3 * LANES:4 * LANES]
            kpos = start + lax.broadcasted_iota(jnp.int32, (1, tk_sel), 1)
            expand = jnp.where(lrow == (kpos >> 6), 1.0, 0.0).astype(BF)
            picked = _dot(sel3, expand) > 0.5
            ok = picked & (kpos <= qpos3)
            return _flash_step(q, k, v, ok, carry)

        o_sel.append(_flash_out(lax.fori_loop(0, c_hi_sel, sel_step, _flash_init(rows, LANES))))

        def win_step(c, carry):
            start = pl.multiple_of(c * tk_win, tk_win)
            k = winb_ref[pl.ds(start, tk_win), 0:LANES]
            v = winb_ref[pl.ds(start, tk_win), LANES:2 * LANES]
            kpos = start + lax.broadcasted_iota(jnp.int32, (1, tk_win), 1)
            ok = (kpos <= qpos3) & (kpos > qpos3 - WIN)
            return _flash_step(q, k, v, ok, carry)

        o_win.append(_flash_out(lax.fori_loop(c_lo_win, c_hi_win, win_step, _flash_init(rows, LANES))))

    gate = gate_ref[...]
    gh = gate.astype(BF)
    gl = (gate - gh.astype(F32)).astype(BF)
    gexp = _dot(gh, gexp_ref[...]) + _dot(gl, gexp_ref[...])
    for g in range(NSA_GROUP):
        rs = slice(g * tq, (g + 1) * tq)
        out = jnp.zeros((tq, LANES), F32)
        for br, o in enumerate((o_cmp, o_sel, o_win)):
            both = jnp.where(lane_h == 0, o[0][rs], o[1][rs])
            out = out + gexp[:, (g * 3 + br) * LANES:(g * 3 + br + 1) * LANES] * both
        o_ref[:, g * LANES:(g + 1) * LANES] = out.astype(BF)


def _nsa_prompt(qc, qr, gate, kcmp, vcmp, nsab, winb, ovl, gexp, batch, t, tq):
    nq = t // tq
    n_groups = t // CMP_STRIDE
    tile = lambda w: pl.BlockSpec((tq, w), lambda b, i: (b * nq + i, 0))
    per_b = lambda r, w: pl.BlockSpec((r, w), lambda b, i: (b, 0))
    const = lambda a: pl.BlockSpec(a.shape, lambda b, i: (0, 0))
    return pl.pallas_call(
        functools.partial(_nsa_prompt_kernel, tq=tq, tk_sel=min(256, t), tk_win=128, n_sel_blocks=t // SEL_BLOCK),
        grid=(batch, nq),
        in_specs=[tile(384), tile(384), tile(LANES), per_b(n_groups, LANES), per_b(n_groups, LANES),
                  per_b(t, 512), per_b(t, 256), const(ovl), const(gexp)],
        out_specs=tile(384),
        out_shape=jax.ShapeDtypeStruct((batch * t, 384), BF),
        compiler_params=_cparams("parallel", "parallel"),
        name="nsa_prompt",
    )(qc, qr, gate, kcmp, vcmp, nsab, winb, ovl, gexp)


def _moba_prompt_kernel(mq_ref, mobab_ref, o_ref, kmbd_ref, *, tq, t):
    qi = pl.program_id(1)
    q0 = qi * tq
    tk = MOBA_BLOCK
    lane = lax.broadcasted_iota(jnp.int32, (1, LANES), 1)
    lane_h = lane >> 6
    j = lane & 15
    hh = lane >> 4
    qpos = q0 + lax.broadcasted_iota(jnp.int32, (tq, 1), 0)
    own = qpos >> 8

    @pl.when(qi == 0)
    def _():
        lr = lax.broadcasted_iota(jnp.int32, (LANES, 1), 0) & 15
        tt = lax.broadcasted_iota(jnp.int32, (1, t), 1) >> 8
        ind = jnp.where(lr == tt, 1.0, 0.0).astype(BF)
        km = _dot(ind, mobab_ref[:, 0:384]) * (1.0 / MOBA_BLOCK)
        kmt = km.T
        rc = lax.broadcasted_iota(jnp.int32, (384, 1), 0) >> 6
        kmbd_ref[...] = jnp.where(rc == hh, kmt, 0.0).astype(BF)

    gate = _dot(mq_ref[...], kmbd_ref[...])
    cnt = jnp.zeros((tq, LANES), F32)
    for d in range(1, 16):
        nowrap = j >= d
        r = jnp.where(nowrap, pltpu.roll(gate, d, axis=1), pltpu.roll(gate, (d - 16) % LANES, axis=1))
        jp = jnp.where(nowrap, j - d, j - d + 16)
        beats = jnp.where(nowrap, jnp.where(r >= gate, 1.0, 0.0), jnp.where(r > gate, 1.0, 0.0))
        cnt = cnt + jnp.where(jp < own, beats, 0.0)
    picked = jnp.where(j == own, 1.0, jnp.where(j < own, jnp.where(cnt < MOBA_TOPK, 1.0, 0.0), 0.0))

    qpos2 = jnp.concatenate([qpos, qpos], axis=0)
    lrow = lax.broadcasted_iota(jnp.int32, (LANES, 1), 0) & 15
    c_hi = (q0 + tq - 1) // tk + 1
    for p in range(MOBA_HEADS // 2):
        qp = mq_ref[:, p * LANES:(p + 1) * LANES]
        zero = jnp.zeros((), BF)
        q = jnp.concatenate([jnp.where(lane_h == 0, qp, zero), jnp.where(lane_h == 1, qp, zero)], axis=0)
        pk = jnp.concatenate([jnp.where(hh == 2 * p, picked, 0.0), jnp.where(hh == 2 * p + 1, picked, 0.0)],
                             axis=0).astype(BF)

        def step(c, carry):
            start = pl.multiple_of(c * tk, tk)
            k = mobab_ref[pl.ds(start, tk), p * LANES:(p + 1) * LANES]
            v = mobab_ref[pl.ds(start, tk), 384 + p * LANES:384 + (p + 1) * LANES]
            kpos = start + lax.broadcasted_iota(jnp.int32, (1, tk), 1)
            expand = jnp.broadcast_to(jnp.where(lrow == c, 1.0, 0.0), (LANES, tk)).astype(BF)
            ok = (_dot(pk, expand) > 0.5) & (kpos <= qpos2)
            return _flash_step(q, k, v, ok, carry)

        out = _flash_out(lax.fori_loop(0, c_hi, step, _flash_init(2 * tq, LANES)))
        o_ref[:, p * LANES:(p + 1) * LANES] = jnp.where(lane_h == 0, out[0:tq], out[tq:2 * tq]).astype(BF)


def _moba_prompt(mq, mobab, batch, t, tq):
    nq = t // tq
    assert t // MOBA_BLOCK <= 16
    tile = lambda w: pl.BlockSpec((tq, w), lambda b, i: (b * nq + i, 0))
    return pl.pallas_call(
        functools.partial(_moba_prompt_kernel, tq=tq, t=t),
        grid=(batch, nq),
        in_specs=[tile(384), pl.BlockSpec((t, 768), lambda b, i: (b, 0))],
        out_specs=tile(384),
        out_shape=jax.ShapeDtypeStruct((batch * t, 384), BF),
        scratch_shapes=[pltpu.VMEM((384, LANES), BF)],
        compiler_params=_cparams("parallel", "arbitrary"),
        name="moba_prompt",
    )(mq, mobab)


def _pool_mix(win_sums, x, count_pos, pw_ref, sc_ref):
    lane_g = lax.broadcasted_iota(jnp.int32, (1, POOL_CH), 1) >> 6
    s = win_sums[POOL_WINDOWS[-1]]
    w = jnp.full((1, POOL_CH), POOL_WINDOWS[-1], jnp.int32)
    for gi in range(len(POOL_WINDOWS) - 2, -1, -1):
        s = jnp.where(lane_g == gi, win_sums[POOL_WINDOWS[gi]], s)
        w = jnp.where(lane_g == gi, POOL_WINDOWS[gi], w)
    count = jnp.minimum(count_pos, w).astype(F32)
    d = s / count - x
    return _dot(d.astype(BF), pw_ref[...]) * sc_ref[...]


def _pool_prompt_kernel(x_ref, halo_ref, pw_ref, sc_ref, o_ref, buf_ref, *, tp):
    i = pl.program_id(1)
    x = x_ref[...]
    buf_ref[0:16, :] = jnp.where(i > 0, halo_ref[...], 0.0)
    buf_ref[16:16 + tp, :] = x
    acc = x
    sums = {}
    for k in range(1, POOL_WINDOWS[-1]):
        acc = acc + buf_ref[16 - k:16 - k + tp, :]
        if k + 1 in POOL_WINDOWS:
            sums[k + 1] = acc
    pos1 = i * tp + lax.broadcasted_iota(jnp.int32, (tp, 1), 0) + 1
    o_ref[...] = _pool_mix(sums, x, pos1, pw_ref, sc_ref).astype(BF)


def _pool_prompt(xpool, pw, sc, layer, batch, t, tp):
    nt = t // tp
    return pl.pallas_call(
        functools.partial(_pool_prompt_kernel, tp=tp),
        grid=(batch, nt),
        in_specs=[pl.BlockSpec((tp, POOL_CH), lambda b, i: (b * nt + i, 0)),
                  pl.BlockSpec((16, POOL_CH), lambda b, i: (jnp.maximum((b * nt + i) * (tp // 16) - 1, 0), 0)),
                  _const_spec((POOL_CH, POOL_CH), layer), _const_spec((1, POOL_CH), layer)],
        out_specs=pl.BlockSpec((tp, POOL_CH), lambda b, i: (b * nt + i, 0)),
        out_shape=jax.ShapeDtypeStruct((batch * t, POOL_CH), BF),
        scratch_shapes=[pltpu.VMEM((tp + 16, POOL_CH), F32)],
        compiler_params=_cparams("parallel", "parallel"),
        name="pool_prompt",
    )(xpool, xpool, pw, sc)


def _sample_kernel(pt_ref, qc_ref, qr_ref, gate_ref, mq_ref, nsa_new_ref, win_new_ref, moba_new_ref, xpool_ref,
                   winst_ref, poolst_ref,
                   posk_ref, w1k_ref, w2k_ref, posv_ref, w1v_ref, w2v_ref, gkc_ref,
                   ovl_ref, gexp_ref, esel_ref, emoba_ref, pw_ref, sc_ref, *rest, n_pages, page, past):
    kc_pages = rest[:n_pages]
    vc_pages = rest[n_pages:2 * n_pages]
    sel_pages = rest[2 * n_pages:3 * n_pages]
    moba_pages = rest[3 * n_pages:4 * n_pages]
    onsa_ref, omoba_ref, ypool_ref = rest[4 * n_pages:]
    del pt_ref
    row = lax.broadcasted_iota(jnp.int32, (8, 1), 0)
    lane = lax.broadcasted_iota(jnp.int32, (1, LANES), 1)
    lane_h = lane >> 6
    row_h = jnp.where(row >= NSA_GROUP, 1, 0) + jnp.where(row >= 2 * NSA_GROUP, 1, 0)
    row_g = row - row_h * NSA_GROUP
    live6 = row < NSA_HEADS
    n_groups = past // CMP_STRIDE
    gpp = page // CMP_STRIDE

    def stack_q(ref):
        s = [ref[:, g * LANES:(g + 1) * LANES].astype(F32) for g in range(NSA_GROUP)]
        q = jnp.where(row_g == 0, s[0], jnp.where(row_g == 1, s[1], s[2]))
        return jnp.where(live6 & (row_h == lane_h), q, 0.0)

    def by_group(o):
        return [jnp.sum(jnp.where(live6 & (row == lane_h * NSA_GROUP + g), o, 0.0), axis=0, keepdims=True)
                for g in range(NSA_GROUP)]

    def strided(pages, l):
        return jnp.concatenate([pg[pl.ds(l, gpp, stride=CMP_STRIDE), :] for pg in pages], axis=0)

    kraw = _compress_tokens(lambda l: strided(kc_pages, l), n_groups, posk_ref, w1k_ref, w2k_ref)
    kcmp = _head_norm(kraw, gkc_ref[...], _ones_bd())
    vcmp = _compress_tokens(lambda l: strided(vc_pages, l), n_groups, posv_ref, w1v_ref, w2v_ref)

    qc8 = stack_q(qc_ref)
    c_end = lax.broadcasted_iota(jnp.int32, (1, n_groups), 1) * CMP_STRIDE + CMP_LEN
    p_c = _softmax_rows(_dot3_nt(qc8, kcmp) * SCALE, c_end <= past + 1)
    o_cmp = _dot(p_c.astype(BF), vcmp.astype(BF))
    imp8 = _dot_sel(p_c, ovl_ref[...])
    imp = jnp.sum(jnp.where(live6 & (row_h == lane_h), imp8, 0.0), axis=0, keepdims=True)
    jblk = lane & (SEL_BLOCK - 1)
    own = past // SEL_BLOCK
    sel = _rank_select(jnp.broadcast_to(imp, (8, LANES)), jblk, own, own + 1, SEL_TOPN)[0:1]
    sel8 = jnp.where(live6 & (row_h == lane_h), sel, 0.0)

    qr8 = stack_q(qr_ref)
    qr8b = qr8.astype(BF)
    nsa_new = nsa_new_ref[...]
    picked = _dot(sel8.astype(BF), esel_ref[...]) > 0.5
    s = jnp.concatenate([_dot_nt(qr8b, pg[:, 0:LANES].astype(BF)) for pg in sel_pages], axis=1) * SCALE
    s = jnp.where(picked, s, NEG)
    own_picked = jnp.sum(jnp.where(jblk == own, sel8, 0.0), axis=1, keepdims=True) > 0.5
    s_new = jnp.where(own_picked, jnp.sum(qr8 * nsa_new[:, 2 * LANES:3 * LANES], axis=1, keepdims=True) * SCALE, NEG)
    m = jnp.maximum(jnp.maximum(jnp.max(s, axis=1, keepdims=True), s_new), M_INIT)
    e = jnp.exp(s - m)
    e_new = jnp.exp(s_new - m)
    den = jnp.sum(e, axis=1, keepdims=True) + e_new
    eb = e.astype(BF)
    acc = e_new * nsa_new[:, 3 * LANES:4 * LANES]
    for pi, pg in enumerate(sel_pages):
        acc = acc + _dot(eb[:, pi * page:(pi + 1) * page], pg[:, LANES:2 * LANES].astype(BF))
    o_sel = jnp.where(den > 0.0, acc / den, 0.0)

    win_new = win_new_ref[...]
    n_win = winst_ref.shape[0]
    kpos_w = past - n_win + lax.broadcasted_iota(jnp.int32, (1, n_win), 1)
    ok_w = (past - kpos_w) < WIN
    s = jnp.where(ok_w, _dot_nt(qr8b, winst_ref[:, 0:LANES].astype(BF)) * SCALE, NEG)
    s_new = jnp.sum(qr8 * win_new[:, 0:LANES], axis=1, keepdims=True) * SCALE
    m = jnp.maximum(jnp.max(s, axis=1, keepdims=True), s_new)
    e = jnp.exp(s - m)
    e_new = jnp.exp(s_new - m)
    den = jnp.sum(e, axis=1, keepdims=True) + e_new
    o_win = (_dot(e.astype(BF), winst_ref[:, LANES:2 * LANES].astype(BF)) + e_new * win_new[:, LANES:2 * LANES]) / den

    gate8 = jnp.broadcast_to(gate_ref[...], (8, LANES))
    gh = gate8.astype(BF)
    gl = (gate8 - gh.astype(F32)).astype(BF)
    gexp = (_dot(gh, gexp_ref[...]) + _dot(gl, gexp_ref[...]))[0:1]
    branches = [by_group(o_cmp), by_group(o_sel), by_group(o_win)]
    for g in range(NSA_GROUP):
        out = jnp.zeros((1, LANES), F32)
        for br in range(3):
            out = out + gexp[:, (g * 3 + br) * LANES:(g * 3 + br + 1) * LANES] * branches[br][g]
        onsa_ref[:, g * LANES:(g + 1) * LANES] = out.astype(BF)

    mq = mq_ref[...].astype(F32)
    lane3 = lax.broadcasted_iota(jnp.int32, (1, 384), 1)
    mq8 = jnp.where(row == (lane3 >> 6), mq, 0.0)
    mq8b = mq8.astype(BF)
    n_blk = past // MOBA_BLOCK
    ppb = MOBA_BLOCK // page
    means = []
    for bi in range(n_blk):
        tot = jnp.zeros((1, 384), F32)
        for pg in moba_pages[bi * ppb:(bi + 1) * ppb]:
            tot = tot + jnp.sum(pg[:, 0:384], axis=0, keepdims=True)
        means.append(tot * (1.0 / MOBA_BLOCK))
    r128 = lax.broadcasted_iota(jnp.int32, (LANES, 1), 0)
    km = jnp.zeros((LANES, 384), F32)
    for bi in range(n_blk):
        km = jnp.where(r128 == bi, means[bi], km)
    gate_m = _dot_nt(mq8b, km.astype(BF))
    cnt = jnp.zeros((8, LANES), F32)
    for d in range(1, n_blk):
        lower = pltpu.roll(gate_m, d, axis=1)
        upper = pltpu.roll(gate_m, LANES - d, axis=1)
        cnt = cnt + jnp.where((lane >= d) & (lower >= gate_m), 1.0, 0.0)
        cnt = cnt + jnp.where((lane + d < n_blk) & (upper > gate_m), 1.0, 0.0)
    pick_m = jnp.where((lane < n_blk) & (cnt < MOBA_TOPK), 1.0, 0.0)
    ok_m = _dot(pick_m.astype(BF), emoba_ref[...]) > 0.5
    moba_new = moba_new_ref[...]
    s = jnp.concatenate([_dot_nt(mq8b, pg[:, 0:384].astype(BF)) for pg in moba_pages], axis=1) * SCALE
    s = jnp.where(ok_m, s, NEG)
    s_new = jnp.sum(mq8 * moba_new[:, 0:384], axis=1, keepdims=True) * SCALE
    m = jnp.maximum(jnp.max(s, axis=1, keepdims=True), s_new)
    e = jnp.exp(s - m)
    e_new = jnp.exp(s_new - m)
    den = jnp.sum(e, axis=1, keepdims=True) + e_new
    eb = e.astype(BF)
    acc = e_new * moba_new[:, 384:768]
    for pi, pg in enumerate(moba_pages):
        acc = acc + _dot(eb[:, pi * page:(pi + 1) * page], pg[:, 384:768].astype(BF))
    o_m = acc / den
    omoba_ref[...] = jnp.sum(jnp.where(row == (lane3 >> 6), o_m, 0.0), axis=0, keepdims=True).astype(BF)

    x = xpool_ref[...]
    hist = poolst_ref[...]
    hrow = lax.broadcasted_iota(jnp.int32, (POOL_HIST, 1), 0)
    sums = {w: x + jnp.sum(jnp.where(hrow >= POOL_HIST - (w - 1), hist, 0.0), axis=0, keepdims=True)
            for w in POOL_WINDOWS}
    y = _pool_mix(sums, x, jnp.full((1, 1), past + 1, jnp.int32), pw_ref, sc_ref)
    ypool_ref[...] = y.astype(BF)


def _sample(page_table, rows, winst, poolst, cache_nsa, cache_moba, cw, consts, pw, sc, layer, past):
    n_seq, n_pages = page_table.shape
    page = cache_nsa.shape[2]
    per_seq = lambda w: pl.BlockSpec((None, 1, w), lambda i, pt: (i, 0, 0))
    lconst = lambda shape: pl.BlockSpec((None,) + tuple(shape), lambda i, pt: (layer,) + (0,) * len(shape))
    const = lambda a: pl.BlockSpec(a.shape, lambda i, pt: (0,) * a.ndim)
    cmp_specs = [lconst((CMP_LEN, LANES)), lconst((CMP_LEN, LANES, LANES)), lconst((LANES, LANES))]

    def page_spec(width, lane_block, pj):
        return pl.BlockSpec((None, None, page, width), lambda i, pt: (layer, pt[i, pj], 0, lane_block))

    in_specs = ([per_seq(384), per_seq(384), per_seq(LANES), per_seq(384), per_seq(512), per_seq(256), per_seq(768),
                 per_seq(POOL_CH),
                 pl.BlockSpec((None, None, winst.shape[2], 256), lambda i, pt: (layer, i, 0, 0)),
                 pl.BlockSpec((None, None, POOL_HIST, POOL_CH), lambda i, pt: (layer, i, 0, 0))]
                + cmp_specs + cmp_specs + [lconst((1, LANES))]
                + [const(consts["ovl_s"]), const(consts["gexp"]), const(consts["esel"]), const(consts["emoba"]),
                   lconst((POOL_CH, POOL_CH)), lconst((1, POOL_CH))]
                + [page_spec(LANES, 0, pj) for pj in range(n_pages)] + [page_spec(LANES, 1, pj) for pj in range(n_pages)]
                + [page_spec(2 * LANES, 1, pj) for pj in range(n_pages)]
                + [page_spec(768, 0, pj) for pj in range(n_pages)])
    grid_spec = pltpu.PrefetchScalarGridSpec(
        num_scalar_prefetch=1, grid=(n_seq,), in_specs=in_specs,
        out_specs=[per_seq(384), per_seq(384), per_seq(POOL_CH)])
    return pl.pallas_call(
        functools.partial(_sample_kernel, n_pages=n_pages, page=page, past=past),
        grid_spec=grid_spec,
        out_shape=[jax.ShapeDtypeStruct((n_seq, 1, 384), BF), jax.ShapeDtypeStruct((n_seq, 1, 384), BF),
                   jax.ShapeDtypeStruct((n_seq, 1, POOL_CH), BF)],
        compiler_params=_cparams("parallel"),
        name="sample_step",
    )(page_table, rows["qc"], rows["qr"], rows["gate"], rows["mq"], rows["nsa"], rows["win"], rows["moba"],
      rows["xpool"], winst, poolst,
      cw["posk"], cw["w1k"], cw["w2k"], cw["posv"], cw["w1v"], cw["w2v"], cw["gkc"],
      consts["ovl_s"], consts["gexp"], consts["esel"], consts["emoba"], pw, sc,
      *([cache_nsa] * (3 * n_pages)), *([cache_moba] * n_pages))


def _combine_kernel(x_ref, yp_ref, on_ref, om_ref, w_ref, o_ref):
    y = _dot(yp_ref[...], w_ref[0:POOL_CH, :]) + _dot(on_ref[...], w_ref[POOL_CH:POOL_CH + 384, :])
    y = y + _dot(om_ref[...], w_ref[POOL_CH + 384:POOL_CH + 768, :])
    o_ref[...] = x_ref[...] + y


def _combine(x, ypool, onsa, omoba, w_out, layer, tm):
    n, d = x.shape
    row = lambda w: pl.BlockSpec((tm, w), lambda i: (i, 0))
    return pl.pallas_call(
        _combine_kernel,
        grid=(pl.cdiv(n, tm),),
        in_specs=[row(d), row(POOL_CH), row(384), row(384), _const_spec((POOL_CH + 768, d), layer)],
        out_specs=row(d),
        out_shape=jax.ShapeDtypeStruct((n, d), F32),
        compiler_params=_cparams("parallel"),
        name="combine",
    )(x, ypool, onsa, omoba, w_out)


def _in_col_perm():
    cols = list(range(0, POOL_CH))
    for g in range(NSA_GROUP):
        for h in range(NSA_KV_HEADS):
            base = POOL_CH + (h * NSA_GROUP + g) * HEAD_DIM
            cols += range(base, base + HEAD_DIM)
    nsa_end = POOL_CH + NSA_HEADS * HEAD_DIM
    kv_end = nsa_end + 6 * NSA_KV_HEADS * HEAD_DIM
    cols += range(nsa_end, kv_end)
    cols += range(kv_end + GATE_COLS, kv_end + GATE_COLS + 3 * MOBA_HEADS * HEAD_DIM)
    gate_cols = list(range(kv_end, kv_end + GATE_COLS))
    return np.array(cols, np.int32), np.array(gate_cols, np.int32)


def _out_row_perm():
    rows = list(range(0, POOL_CH))
    for g in range(NSA_GROUP):
        for h in range(NSA_KV_HEADS):
            base = POOL_CH + (h * NSA_GROUP + g) * HEAD_DIM
            rows += range(base, base + HEAD_DIM)
    rows += range(POOL_CH + NSA_HEADS * HEAD_DIM, POOL_CH + NSA_HEADS * HEAD_DIM + MOBA_HEADS * HEAD_DIM)
    return np.array(rows, np.int32)


def _overlap(n_cmp_rows, n_cmp_valid, n_blocks):
    n = np.arange(n_cmp_rows)
    c_start = n * CMP_STRIDE
    c_end = c_start + CMP_LEN
    jb = np.arange(SEL_BLOCK)
    ov = (c_start[:, None] < (jb[None, :] + 1) * SEL_BLOCK) & (c_end[:, None] > jb[None, :] * SEL_BLOCK)
    ov &= (n[:, None] < n_cmp_valid) & (jb[None, :] < n_blocks)
    return ov.astype(np.float32)


def _constants(t, past):
    g_p = t // CMP_STRIDE
    ov = _overlap(g_p, g_p - 1, t // SEL_BLOCK)
    ovl_p = np.zeros((2 * g_p, LANES), np.float32)
    ovl_p[:g_p, :SEL_BLOCK] = ov
    ovl_p[g_p:, SEL_BLOCK:] = ov
    g_s = past // CMP_STRIDE
    ov_s = _overlap(g_s, g_s - 1, past // SEL_BLOCK + 1)
    ovl_s = np.concatenate([ov_s, ov_s], axis=1)
    lane = np.arange(LANES)
    gexp = np.zeros((LANES, 9 * LANES), np.float32)
    for g in range(NSA_GROUP):
        for br in range(3):
            src = (lane // HEAD_DIM) * 9 + g * 3 + br
            gexp[src, (g * 3 + br) * LANES + lane] = 1.0
    kp = np.arange(past)
    esel = ((lane[:, None] % SEL_BLOCK) == (kp[None, :] // SEL_BLOCK)).astype(np.float32)
    emoba = (lane[:, None] == (kp[None, :] // MOBA_BLOCK)).astype(np.float32)
    as_bf = lambda a: jnp.asarray(a, BF)
    return {"ovl_p": as_bf(ovl_p), "ovl_s": as_bf(ovl_s), "gexp": as_bf(gexp), "esel": as_bf(esel), "emoba": as_bf(emoba)}


def _rope_tables(pos):
    half = ROPE_DIM // 2
    inv_freq = ROPE_THETA ** (-jnp.arange(half, dtype=F32) / half)
    ang = pos.astype(F32)[:, None] * inv_freq[None, :]
    cos, sin, zero = jnp.cos(ang), jnp.sin(ang), jnp.zeros_like(ang)
    rest = HEAD_DIM - ROPE_DIM
    c = jnp.concatenate([cos, cos, jnp.ones((ang.shape[0], rest), F32)], axis=1)
    a = jnp.concatenate([-sin, zero, jnp.zeros((ang.shape[0], rest), F32)], axis=1)
    b = jnp.concatenate([zero, sin, jnp.zeros((ang.shape[0], rest), F32)], axis=1)
    tile2 = lambda m: jnp.concatenate([m, m], axis=1)
    return tile2(c), tile2(a), tile2(b)


def _cmp_weights(cmp_pos, cmp_w1, cmp_w2, g_nsa_kc):
    bd = lambda w: jnp.concatenate([jnp.concatenate([w, jnp.zeros_like(w)], axis=-1),
                                    jnp.concatenate([jnp.zeros_like(w), w], axis=-1)], axis=-2)
    out = {}
    for i, name in enumerate(("k", "v")):
        out["pos" + name] = jnp.concatenate([cmp_pos[:, i], cmp_pos[:, i]], axis=-1)
        out["w1" + name] = bd(cmp_w1[:, i]).astype(BF)
        out["w2" + name] = bd(cmp_w2[:, i]).astype(BF)
    out["gkc"] = jnp.concatenate([g_nsa_kc, g_nsa_kc], axis=-1)[:, None, :]
    return out


def kernel(x_prompt, x_sample, cache_nsa_kv, cache_moba_kv, state_nsa_win, state_pool, page_table, g_ffa, w_ffa_gate, w_ffa_up, w_ffa_down, g_mix, w_in, w_out, pool_w, pool_scale, g_nsa_q, g_nsa_kc, g_nsa_ks, g_nsa_kw, cmp_pos, cmp_w1, cmp_w2, g_moba_q, g_moba_k, g_ffb, w_ffb_gate, w_ffb_up, w_ffb_down):
    batch, t, d = x_prompt.shape
    n_seq = x_sample.shape[0]
    depth, n_pool, page = cache_nsa_kv.shape[:3]
    n_pages = page_table.shape[1]
    past = n_pages * page
    n_p = batch * t
    n_win = state_nsa_win.shape[2]
    assert x_sample.shape[1] == 1 and t % MOBA_BLOCK == 0 and past % MOBA_BLOCK == 0 and n_win == WIN

    tm = next((c for c in (512, 384, 256) if (n_p + n_seq) % c == 0), 512)
    tp = next(c for c in (512, 256) if t % c == 0)
    tq_nsa = 128
    tq_moba = MOBA_BLOCK

    col_perm, gate_cols = _in_col_perm()
    w_in_p = jnp.concatenate([w_in[:, :, col_perm], w_in[:, :, gate_cols],
                              jnp.zeros((depth, d, LANES - GATE_COLS), w_in.dtype)], axis=-1).astype(BF)
    w_out_p = w_out[:, _out_row_perm(), :].astype(BF)
    bf = lambda w: w.astype(BF)
    wa = (bf(w_ffa_gate), bf(w_ffa_up), bf(w_ffa_down))
    wb = (bf(w_ffb_gate), bf(w_ffb_up), bf(w_ffb_down))
    tile2 = lambda g: jnp.concatenate([g, g], axis=-1)
    gains = jnp.stack([tile2(g_nsa_q), tile2(g_nsa_ks), tile2(g_nsa_kw), tile2(g_moba_q), tile2(g_moba_k)]
                      + [jnp.ones((depth, LANES), F32)] * 3, axis=1)
    cw = _cmp_weights(cmp_pos, cmp_w1, cmp_w2, g_nsa_kc)
    eye = jnp.eye(len(POOL_WINDOWS), dtype=pool_w.dtype)
    pw_bd = jnp.einsum("lgij,gh->lgihj", pool_w, eye).reshape(depth, POOL_CH, POOL_CH).astype(BF)
    sc = pool_scale[:, None, :]
    consts = _constants(t, past)
    pos = jnp.concatenate([jnp.tile(jnp.arange(t, dtype=jnp.int32), batch), jnp.full((n_seq,), past, jnp.int32)])
    rope_c, rope_a, rope_b = _rope_tables(pos)

    cache_nsa = cache_nsa_kv.reshape(depth, n_pool, page, 512)
    cache_moba = cache_moba_kv.reshape(depth, n_pool, page, 768)
    winst = state_nsa_win.reshape(depth, n_seq, n_win, 256)

    x = jnp.concatenate([x_prompt.reshape(n_p, d), x_sample.reshape(n_seq, d)], axis=0)
    outs = [[] for _ in range(8)]
    for l in range(depth):
        x = _ffn(x, g_ffa[:, None, :], *wa, l, tm)
        (xpool, qc, qr, nsa, nsab, win, winb, mq, moba, mobab, gate) = _prep(
            x, g_mix[:, None, :], w_in_p, gains, rope_c, rope_a, rope_b, l, tm)

        kcmp, vcmp = _compress_prompt(nsa, cw, l, batch, t)
        onsa_p = _nsa_prompt(qc, qr, gate, kcmp, vcmp, nsab, winb, consts["ovl_p"], consts["gexp"], batch, t, tq_nsa)
        omoba_p = _moba_prompt(mq, mobab, batch, t, tq_moba)
        ypool_p = _pool_prompt(xpool, pw_bd, sc, l, batch, t, tp)

        srow = lambda a: a[n_p:].reshape(n_seq, 1, a.shape[1])
        rows = {"qc": srow(qc), "qr": srow(qr), "gate": srow(gate), "mq": srow(mq), "nsa": srow(nsa),
                "win": srow(win), "moba": srow(moba), "xpool": srow(xpool)}
        onsa_s, omoba_s, ypool_s = _sample(page_table, rows, winst, state_pool, cache_nsa, cache_moba, cw, consts,
                                           pw_bd, sc, l, past)

        cat = lambda p, s: jnp.concatenate([p, s.reshape(n_seq, s.shape[2])], axis=0)
        x = _combine(x, cat(ypool_p, ypool_s), cat(onsa_p, onsa_s), cat(omoba_p, omoba_s), w_out_p, l, tm)
        x = _ffn(x, g_ffb[:, None, :], *wb, l, tm)

        outs[0].append(nsa[:n_p].reshape(batch, t, 4, NSA_KV_HEADS, HEAD_DIM))
        outs[1].append(nsa[n_p:].reshape(n_seq, 1, 4, NSA_KV_HEADS, HEAD_DIM))
        outs[2].append(moba[:n_p].reshape(batch, t, 2, MOBA_HEADS, HEAD_DIM))
        outs[3].append(moba[n_p:].reshape(n_seq, 1, 2, MOBA_HEADS, HEAD_DIM))
        outs[4].append(win[:n_p].reshape(batch, t, 2, NSA_KV_HEADS, HEAD_DIM)[:, t - min(WIN, t):])
        new_win = win[n_p:].reshape(n_seq, 1, 2, NSA_KV_HEADS, HEAD_DIM)
        outs[5].append(jnp.concatenate([state_nsa_win[l], new_win], axis=1)[:, 1:])
        outs[6].append(xpool[:n_p].reshape(batch, t, POOL_CH)[:, t - POOL_HIST:])
        outs[7].append(jnp.concatenate([state_pool[l], xpool[n_p:].reshape(n_seq, 1, POOL_CH)], axis=1)[:, 1:])

    stacked = [jnp.stack(o, axis=0) for o in outs]
    return (x[:n_p].reshape(batch, t, d), x[n_p:].reshape(n_seq, 1, d), *stacked)
```

```python
import functools

import numpy as np
import jax
import jax.numpy as jnp
from jax import lax
from jax.experimental import pallas as pl
from jax.experimental.pallas import tpu as pltpu

F32 = jnp.float32
BF = jnp.bfloat16

HEAD_DIM = 64
LANES = 128
POOL_CH = 256
POOL_WINDOWS = (2, 4, 8, 16)
POOL_GROUP = 64
POOL_HIST = 15
NSA_KV_HEADS = 2
NSA_GROUP = 3
NSA_HEADS = 6
MOBA_HEADS = 6
CMP_LEN = 32
CMP_STRIDE = 16
SEL_BLOCK = 64
SEL_TOPN = 16
WIN = 512
MOBA_BLOCK = 256
MOBA_TOPK = 3
ROPE_THETA = 500000.0
ROPE_DIM = 16
EPS = 1e-6
NEG = -1e30
M_INIT = -1e29
SEL_FORCE = 1e4
SCALE = HEAD_DIM ** -0.5
Q_PRESCALE = SCALE * 1.4426950408889634
GATE_COLS = 3 * NSA_HEADS
D_QKV = 2560
D_IN_PAD = D_QKV + LANES
VMEM_LIMIT = 56 * 1024 * 1024


def _cparams(*sem):
    return pltpu.CompilerParams(dimension_semantics=sem, vmem_limit_bytes=VMEM_LIMIT)


def _dot(a, b):
    return jnp.dot(a, b, preferred_element_type=F32)


def _dot_nt(a, b):
    return lax.dot_general(a, b, (((1,), (1,)), ((), ())), preferred_element_type=F32)


def _split3(x):
    hi = x.astype(BF)
    r = x - hi.astype(F32)
    mid = r.astype(BF)
    lo = (r - mid.astype(F32)).astype(BF)
    return hi, mid, lo


def _dot_sel(x, m):
    hi, mid, lo = _split3(x)
    return _dot(hi, m) + _dot(mid, m) + _dot(lo, m)


def _dot3_nt(a, b):
    ah = a.astype(BF)
    al = (a - ah.astype(F32)).astype(BF)
    bh = b.astype(BF)
    bl = (b - bh.astype(F32)).astype(BF)
    return _dot_nt(ah, bh) + _dot_nt(ah, bl) + _dot_nt(al, bh)


def _sigmoid(x):
    return 1.0 / (1.0 + jnp.exp(-x))


def _silu(x):
    return x * _sigmoid(x)


def _rms_rows(x, g):
    return x * lax.rsqrt(jnp.mean(x * x, axis=-1, keepdims=True) + EPS) * g


def _ones_bd():
    r = lax.broadcasted_iota(jnp.int32, (LANES, LANES), 0) >> 6
    c = lax.broadcasted_iota(jnp.int32, (LANES, LANES), 1) >> 6
    return jnp.where(r == c, 1.0, 0.0).astype(BF)


def _head_norm(x, gain, ones_bd):
    ms = _dot_sel(x * x, ones_bd) * (1.0 / HEAD_DIM)
    return x * lax.rsqrt(ms + EPS) * gain


def _rope(x, c, a, b):
    return x * c + pltpu.roll(x, LANES - ROPE_DIM // 2, axis=1) * a + pltpu.roll(x, ROPE_DIM // 2, axis=1) * b


def _softmax_rows(s, valid):
    s = jnp.where(valid, s, NEG)
    m = jnp.max(s, axis=1, keepdims=True)
    e = jnp.where(valid, jnp.exp(s - m), 0.0)
    den = jnp.sum(e, axis=1, keepdims=True)
    return jnp.where(den > 0.0, e / den, 0.0)


def _ffn_body(x, g, wg_ref, wu_ref, wd_ref, f_chunk):
    xn = _rms_rows(x, g).astype(BF)
    d_ff = wg_ref.shape[1]
    acc = jnp.zeros(x.shape, F32)
    for c in range(d_ff // f_chunk):
        sl = slice(c * f_chunk, (c + 1) * f_chunk)
        gate = _dot(xn, wg_ref[:, sl])
        up = _dot(xn, wu_ref[:, sl])
        h = (_silu(gate) * up).astype(BF)
        acc = acc + _dot(h, wd_ref[sl, :])
    return x + 0.5 * acc


def _ffn_kernel(x_ref, g_ref, wg_ref, wu_ref, wd_ref, o_ref, *, f_chunk):
    o_ref[...] = _ffn_body(x_ref[...], g_ref[...], wg_ref, wu_ref, wd_ref, f_chunk)


def _f_chunk(d_ff):
    for c in (1408, 1024, 512, 256, 128):
        if d_ff % c == 0:
            return c
    return d_ff


def _const_spec(shape, layer):
    nd = len(shape)
    return pl.BlockSpec((None,) + tuple(shape), lambda *_: (layer,) + (0,) * nd, pipeline_mode=pl.Buffered(1))


def _ffn(x, g, wg, wu, wd, layer, tm):
    n, d = x.shape
    d_ff = wg.shape[2]
    return pl.pallas_call(
        functools.partial(_ffn_kernel, f_chunk=_f_chunk(d_ff)),
        grid=(pl.cdiv(n, tm),),
        in_specs=[pl.BlockSpec((tm, d), lambda i: (i, 0)),
                  _const_spec((1, d), layer),
                  _const_spec((d, d_ff), layer), _const_spec((d, d_ff), layer), _const_spec((d_ff, d), layer)],
        out_specs=pl.BlockSpec((tm, d), lambda i: (i, 0)),
        out_shape=jax.ShapeDtypeStruct((n, d), F32),
        compiler_params=_cparams("parallel"),
        name="ffn",
    )(x, g, wg, wu, wd)


def _prep_kernel(x_ref, g_ref, w_ref, gains_ref, c_ref, a_ref, b_ref,
                 xpool_ref, qc_ref, qr_ref, nsa_ref, nsab_ref, win_ref, winb_ref,
                 mq_ref, moba_ref, mobab_ref, gate_ref):
    xn = _rms_rows(x_ref[...], g_ref[...]).astype(BF)
    z = _dot(xn, w_ref[...])
    ones_bd = _ones_bd()
    c, a, b = c_ref[...], a_ref[...], b_ref[...]
    gains = gains_ref[...]
    slab = lambda i: z[:, i * LANES:(i + 1) * LANES]
    norm_rope = lambda s, gi: _rope(_head_norm(s, gains[gi:gi + 1], ones_bd), c, a, b)

    xpool_ref[...] = z[:, 0:POOL_CH]
    for gi in range(3):
        qn = _head_norm(slab(2 + gi), gains[0:1], ones_bd)
        qc_ref[:, gi * LANES:(gi + 1) * LANES] = qn
        qr_ref[:, gi * LANES:(gi + 1) * LANES] = (_rope(qn, c, a, b) * Q_PRESCALE).astype(BF)
    nsa = [slab(5), slab(6), norm_rope(slab(7), 1), slab(8)]
    for i, s in enumerate(nsa):
        nsa_ref[:, i * LANES:(i + 1) * LANES] = s
        nsab_ref[:, i * LANES:(i + 1) * LANES] = s.astype(BF)
    win = [norm_rope(slab(9), 2), slab(10)]
    for i, s in enumerate(win):
        win_ref[:, i * LANES:(i + 1) * LANES] = s
        winb_ref[:, i * LANES:(i + 1) * LANES] = s.astype(BF)
    for i in range(3):
        mq_ref[:, i * LANES:(i + 1) * LANES] = (norm_rope(slab(11 + i), 3) * Q_PRESCALE).astype(BF)
    for i in range(6):
        s = norm_rope(slab(14 + i), 4) if i < 3 else slab(14 + i)
        moba_ref[:, i * LANES:(i + 1) * LANES] = s
        mobab_ref[:, i * LANES:(i + 1) * LANES] = s.astype(BF)
    gate_ref[...] = _sigmoid(slab(20))


def _prep(x, g, w_in, gains, rope_c, rope_a, rope_b, layer, tm):
    n, d = x.shape
    row = lambda w: pl.BlockSpec((tm, w), lambda i: (i, 0))
    widths_dtypes = [(POOL_CH, F32), (384, F32), (384, BF), (512, F32), (512, BF), (256, F32), (256, BF),
                     (384, BF), (768, F32), (768, BF), (LANES, F32)]
    return pl.pallas_call(
        _prep_kernel,
        grid=(pl.cdiv(n, tm),),
        in_specs=[row(d), _const_spec((1, d), layer), _const_spec((d, D_IN_PAD), layer),
                  _const_spec((8, LANES), layer), row(LANES), row(LANES), row(LANES)],
        out_specs=[row(w) for w, _ in widths_dtypes],
        out_shape=[jax.ShapeDtypeStruct((n, w), dt) for w, dt in widths_dtypes],
        compiler_params=_cparams("parallel"),
        name="prep",
    )(x, g, w_in, gains, rope_c, rope_a, rope_b)


def _compress_tokens(load_rows, n_groups, pos_ref, w1_ref, w2_ref):
    pa = jnp.zeros((n_groups, LANES), F32)
    pb = jnp.zeros((n_groups, LANES), F32)
    for l in range(CMP_STRIDE):
        xl = load_rows(l)
        pa = pa + _dot((xl + pos_ref[l:l + 1, :]).astype(BF), w1_ref[l])
        pb = pb + _dot((xl + pos_ref[CMP_STRIDE + l:CMP_STRIDE + l + 1, :]).astype(BF), w1_ref[CMP_STRIDE + l])
    pre = pa + pltpu.roll(pb, n_groups - 1, axis=0)
    return _dot(_silu(pre).astype(BF), w2_ref[...])


def _compress_prompt_kernel(kc_ref, vc_ref, posk_ref, w1k_ref, w2k_ref, posv_ref, w1v_ref, w2v_ref, gkc_ref,
                            kcmp_ref, vcmp_ref, *, n_groups):
    kraw = _compress_tokens(lambda l: kc_ref[pl.ds(l, n_groups, stride=CMP_STRIDE), :],
                            n_groups, posk_ref, w1k_ref, w2k_ref)
    kcmp_ref[...] = _head_norm(kraw, gkc_ref[...], _ones_bd())
    vcmp_ref[...] = _compress_tokens(lambda l: vc_ref[pl.ds(l, n_groups, stride=CMP_STRIDE), :],
                                     n_groups, posv_ref, w1v_ref, w2v_ref)


def _cmp_weight_specs(layer):
    return [_const_spec((CMP_LEN, LANES), layer), _const_spec((CMP_LEN, LANES, LANES), layer),
            _const_spec((LANES, LANES), layer)]


def _compress_prompt(nsa_rows, cw, layer, batch, t):
    n_groups = t // CMP_STRIDE
    return pl.pallas_call(
        functools.partial(_compress_prompt_kernel, n_groups=n_groups),
        grid=(batch,),
        in_specs=[pl.BlockSpec((t, LANES), lambda b: (b, 0)), pl.BlockSpec((t, LANES), lambda b: (b, 1))]
                 + _cmp_weight_specs(layer) + _cmp_weight_specs(layer) + [_const_spec((1, LANES), layer)],
        out_specs=[pl.BlockSpec((n_groups, LANES), lambda b: (b, 0))] * 2,
        out_shape=[jax.ShapeDtypeStruct((batch * n_groups, LANES), F32)] * 2,
        compiler_params=_cparams("parallel"),
        name="compress_prompt",
    )(nsa_rows, nsa_rows, cw["posk"], cw["w1k"], cw["w2k"], cw["posv"], cw["w1v"], cw["w2v"], cw["gkc"])


def _rank_select(imp, j, own, n_blocks, top_n):
    forced = (j == 0) | (j == own) | (j == own - 1)
    imp = jnp.where(forced, imp + SEL_FORCE, imp)
    imp = jnp.where(j <= own, imp, -jnp.inf)
    cnt = jnp.zeros(imp.shape, F32)
    for d in range(1, SEL_BLOCK):
        if d > n_blocks - 1 and d < SEL_BLOCK - (n_blocks - 1):
            continue
        nowrap = j >= d
        r = jnp.where(nowrap, pltpu.roll(imp, d, axis=1), pltpu.roll(imp, (d - SEL_BLOCK) % LANES, axis=1))
        cnt = cnt + jnp.where(nowrap, jnp.where(r >= imp, 1.0, 0.0), jnp.where(r > imp, 1.0, 0.0))
    return jnp.where(j <= own, jnp.where(cnt < top_n, 1.0, 0.0), 0.0)


def _online_update(s, v, m_ref, l_ref, acc_ref, idx):
    m_prev = m_ref[idx]
    m_new = jnp.maximum(m_prev, jnp.max(s, axis=1, keepdims=True))
    alpha = jnp.exp2(m_prev - m_new)
    ps = [jnp.exp2(s[:, i * LANES:(i + 1) * LANES] - m_new) for i in range(s.shape[1] // LANES)]
    l_ref[idx] = alpha * l_ref[idx] + functools.reduce(lambda a, b: a + b, ps)
    acc_ref[idx] = alpha * acc_ref[idx] + _dot(jnp.concatenate(ps, axis=1).astype(BF), v)
    m_ref[idx] = m_new


def _online_init(m_ref, l_ref, acc_ref):
    m_ref[...] = jnp.full(m_ref.shape, M_INIT, F32)
    l_ref[...] = jnp.zeros(l_ref.shape, F32)
    acc_ref[...] = jnp.zeros(acc_ref.shape, F32)


def _online_out(l_ref, acc_ref, idx):
    l = jnp.sum(l_ref[idx], axis=1, keepdims=True)
    return jnp.where(l > 0.0, acc_ref[idx] / l, 0.0)


def _nsa_prompt_kernel(qc_ref, qr_ref, gate_ref, kcmp_ref, vcmp_ref, nsab_ref, winb_ref, ovlt_ref, gexp_ref, et_ref,
                       o_ref, impt_ref, m_ref, l_ref, acc_ref, *, tq, tk):
    qi = pl.program_id(1)
    q0 = qi * tq
    lane = lax.broadcasted_iota(jnp.int32, (1, LANES), 1)
    lane_h = lane >> 6
    qpos = q0 + lax.broadcasted_iota(jnp.int32, (tq, 1), 0)
    qpos3 = jnp.concatenate([qpos] * NSA_GROUP, axis=0)

    def stack_q(ref, h, zero):
        return jnp.concatenate(
            [jnp.where(lane_h == h, ref[:, g * LANES:(g + 1) * LANES], zero) for g in range(NSA_GROUP)], axis=0)

    kc = kcmp_ref[...]
    vc = vcmp_ref[...].astype(BF)
    n_cmp = kc.shape[0]
    c_end = lax.broadcasted_iota(jnp.int32, (1, n_cmp), 1) * CMP_STRIDE + CMP_LEN
    valid_c = c_end <= qpos3 + 1
    o_cmp, psum = [], []
    for h in range(NSA_KV_HEADS):
        p = _softmax_rows(_dot3_nt(stack_q(qc_ref, h, 0.0), kc) * SCALE, valid_c)
        o_cmp.append(_dot(p.astype(BF), vc))
        psum.append(p[0:tq] + p[tq:2 * tq] + p[2 * tq:3 * tq])
    hi, mid, lo = _split3(jnp.concatenate(psum, axis=1))
    ovlt = ovlt_ref[...]
    impt = _dot_nt(ovlt, hi) + _dot_nt(ovlt, mid) + _dot_nt(ovlt, lo)
    brow = lax.broadcasted_iota(jnp.int32, (LANES, 1), 0)
    jrow = brow & (SEL_BLOCK - 1)
    own_l = (q0 + lax.broadcasted_iota(jnp.int32, (1, tq), 1)) >> 6
    forced = (jrow == 0) | (jrow == own_l) | (jrow == own_l - 1)
    impt = jnp.where(forced, impt + SEL_FORCE, impt)
    impt = jnp.where(jrow <= own_l, impt, -jnp.inf)
    impt_ref[...] = impt

    def rank_step(jp, cnt):
        comp = jnp.where(brow < SEL_BLOCK, impt_ref[pl.ds(jp, 1), :], impt_ref[pl.ds(SEL_BLOCK + jp, 1), :])
        return cnt + jnp.where(jrow > jp, jnp.where(comp >= impt, 1.0, 0.0), jnp.where(comp > impt, 1.0, 0.0))

    n_live = (q0 + tq - 1) // SEL_BLOCK + 1
    cnt = lax.fori_loop(0, n_live, rank_step, jnp.zeros((LANES, tq), F32))
    sel_t = jnp.where(jrow <= own_l, jnp.where(cnt < SEL_TOPN, 1.0, 0.0), 0.0)
    sel_neg = jnp.where(sel_t.T > 0.5, 0.0, NEG)

    zero = jnp.zeros((), BF)
    q_rot = [stack_q(qr_ref, h, zero) for h in range(NSA_KV_HEADS)]
    q_aug = [jnp.concatenate([q_rot[h], jnp.concatenate([jnp.where(lane_h == h, sel_neg, 0.0).astype(BF)] * NSA_GROUP,
                                                         axis=0)], axis=1) for h in range(NSA_KV_HEADS)]
    _online_init(m_ref, l_ref, acc_ref)

    q_all = jnp.concatenate(q_aug, axis=0)
    qpos6 = jnp.concatenate([qpos3, qpos3], axis=0)

    def sel_chunk(c, causal):
        start = pl.multiple_of(c * tk, tk)
        k_aug = jnp.concatenate([nsab_ref[pl.ds(start, tk), 2 * LANES:3 * LANES], et_ref[pl.ds(start, tk), :]], axis=1)
        v = nsab_ref[pl.ds(start, tk), 3 * LANES:4 * LANES]
        s = _dot_nt(q_all, k_aug)
        if causal:
            kpos = start + lax.broadcasted_iota(jnp.int32, (1, tk), 1)
            s = jnp.where(kpos <= qpos6, s, NEG)
        _online_update(s, v, m_ref, l_ref, acc_ref, 0)

    c_last = (q0 + tq - 1) // tk
    lax.fori_loop(0, c_last, lambda c, carry: (sel_chunk(c, False), carry)[1], 0)
    sel_chunk(c_last, True)
    o_all = _online_out(l_ref, acc_ref, 0)
    o_sel = [o_all[h * NSA_GROUP * tq:(h + 1) * NSA_GROUP * tq] for h in range(NSA_KV_HEADS)]

    n_slots = WIN // tq + 1
    s_w = [[] for _ in range(NSA_KV_HEADS)]
    v_w = []
    for slot in range(n_slots):
        c = qi - (n_slots - 1) + slot
        start = pl.multiple_of(jnp.maximum(c, 0) * tq, tq)
        k = winb_ref[pl.ds(start, tq), 0:LANES]
        v_w.append(winb_ref[pl.ds(start, tq), LANES:2 * LANES])
        kpos = start + lax.broadcasted_iota(jnp.int32, (1, tq), 1)
        for h in range(NSA_KV_HEADS):
            s = _dot_nt(q_rot[h], k)
            if slot == 0:
                s = jnp.where(kpos > qpos3 - WIN, s, NEG)
            if slot == n_slots - 1:
                s = jnp.where(kpos <= qpos3, s, NEG)
            else:
                s = jnp.where(c >= 0, s, NEG)
            s_w[h].append(s)
    v_band = jnp.concatenate(v_w, axis=0)
    o_win = []
    for h in range(NSA_KV_HEADS):
        m = jnp.max(functools.reduce(jnp.maximum, s_w[h]), axis=1, keepdims=True)
        ps = [jnp.exp2(s - m) for s in s_w[h]]
        l = jnp.sum(functools.reduce(lambda a, b: a + b, ps), axis=1, keepdims=True)
        o_win.append(_dot(jnp.concatenate(ps, axis=1).astype(BF), v_band) / l)

    gate = gate_ref[...]
    gh = gate.astype(BF)
    gl = (gate - gh.astype(F32)).astype(BF)
    gexp = _dot(gh, gexp_ref[...]) + _dot(gl, gexp_ref[...])
    for g in range(NSA_GROUP):
        rs = slice(g * tq, (g + 1) * tq)
        out = jnp.zeros((tq, LANES), F32)
        for br, o in enumerate((o_cmp, o_sel, o_win)):
            both = jnp.where(lane_h == 0, o[0][rs], o[1][rs])
            out = out + gexp[:, (g * 3 + br) * LANES:(g * 3 + br + 1) * LANES] * both
        o_ref[:, g * LANES:(g + 1) * LANES] = out.astype(BF)


def _nsa_prompt(qc, qr, gate, kcmp, vcmp, nsab, winb, ovlt, gexp, et, batch, t, tq):
    nq = t // tq
    n_groups = t // CMP_STRIDE
    rows = NSA_GROUP * tq
    assert WIN % tq == 0 and t // SEL_BLOCK <= SEL_BLOCK
    tile = lambda w: pl.BlockSpec((tq, w), lambda b, i: (b * nq + i, 0))
    per_b = lambda r, w: pl.BlockSpec((r, w), lambda b, i: (b, 0))
    const = lambda a: pl.BlockSpec(a.shape, lambda b, i: (0, 0))
    return pl.pallas_call(
        functools.partial(_nsa_prompt_kernel, tq=tq, tk=min(512, t)),
        grid=(batch, nq),
        in_specs=[tile(384), tile(384), tile(LANES), per_b(n_groups, LANES), per_b(n_groups, LANES),
                  per_b(t, 512), per_b(t, 256), const(ovlt), const(gexp), const(et)],
        out_specs=tile(384),
        out_shape=jax.ShapeDtypeStruct((batch * t, 384), BF),
        scratch_shapes=[pltpu.VMEM((LANES, tq), F32)] + [pltpu.VMEM((1, NSA_HEADS * tq, LANES), F32)] * 3,
        compiler_params=_cparams("parallel", "parallel"),
        name="nsa_prompt",
    )(qc, qr, gate, kcmp, vcmp, nsab, winb, ovlt, gexp, et)


def _moba_prompt_kernel(mq_ref, mobab_ref, et_ref, o_ref, kmbd_ref, gate_ref, m_ref, l_ref, acc_ref, *, tq, t, tk, rb):
    qi = pl.program_id(1)
    q0 = qi * tq
    n_pairs = MOBA_HEADS // 2
    lane = lax.broadcasted_iota(jnp.int32, (1, LANES), 1)
    lane_h = lane >> 6
    hh = lane >> 4
    qpos = q0 + lax.broadcasted_iota(jnp.int32, (tq, 1), 0)
    qpos2 = jnp.concatenate([qpos, qpos], axis=0)
    brow = lax.broadcasted_iota(jnp.int32, (LANES, 1), 0)
    jrow = brow & 15

    @pl.when(qi == 0)
    def _():
        tt = lax.broadcasted_iota(jnp.int32, (1, t), 1) >> 8
        ind = jnp.where(jrow == tt, 1.0, 0.0).astype(BF)
        km = _dot(ind, mobab_ref[:, 0:384]) * (1.0 / MOBA_BLOCK)
        col_h = lax.broadcasted_iota(jnp.int32, (1, 384), 1) >> 6
        kmbd_ref[...] = jnp.where(col_h == (brow >> 4), km, 0.0).astype(BF)

    gate = _dot_nt(kmbd_ref[...], mq_ref[...])
    gate_ref[...] = gate
    own_l = (q0 + lax.broadcasted_iota(jnp.int32, (1, tq), 1)) >> 8

    def rank_step(jp, cnt):
        comp = jnp.concatenate([jnp.broadcast_to(gate_ref[pl.ds(hd * 16 + jp, 1), :], (16, tq))
                                for hd in range(LANES // 16)], axis=0)
        beats = jnp.where(jrow > jp, jnp.where(comp >= gate, 1.0, 0.0), jnp.where(comp > gate, 1.0, 0.0))
        return cnt + jnp.where(jp < own_l, beats, 0.0)

    n_past = (q0 + tq - 1) // MOBA_BLOCK
    cnt = lax.fori_loop(0, n_past, rank_step, jnp.zeros((LANES, tq), F32))
    picked_t = jnp.where(jrow == own_l, 1.0, jnp.where(jrow < own_l, jnp.where(cnt < MOBA_TOPK, 1.0, 0.0), 0.0))
    pick_neg = jnp.where(picked_t.T > 0.5, 0.0, NEG)

    zero = jnp.zeros((), BF)
    q_aug = []
    for p in range(n_pairs):
        qp = mq_ref[:, p * LANES:(p + 1) * LANES]
        q2 = jnp.concatenate([jnp.where(lane_h == 0, qp, zero), jnp.where(lane_h == 1, qp, zero)], axis=0)
        pk = jnp.concatenate([jnp.where(hh == 2 * p, pick_neg, 0.0), jnp.where(hh == 2 * p + 1, pick_neg, 0.0)], axis=0)
        q_aug.append(jnp.concatenate([q2, pk.astype(BF)], axis=1))
    _online_init(m_ref, l_ref, acc_ref)

    n_rb = 2 * tq // rb

    def chunk(c, causal):
        start = pl.multiple_of(c * tk, tk)
        et = et_ref[pl.ds(start, tk), :]
        for p in range(n_pairs):
            k_aug = jnp.concatenate([mobab_ref[pl.ds(start, tk), p * LANES:(p + 1) * LANES], et], axis=1)
            v = mobab_ref[pl.ds(start, tk), 384 + p * LANES:384 + (p + 1) * LANES]
            for r in range(n_rb):
                s = _dot_nt(q_aug[p][r * rb:(r + 1) * rb], k_aug)
                if causal:
                    kpos = start + lax.broadcasted_iota(jnp.int32, (1, tk), 1)
                    s = jnp.where(kpos <= qpos2[r * rb:(r + 1) * rb], s, NEG)
                _online_update(s, v, m_ref, l_ref, acc_ref, p * n_rb + r)

    c_last = (q0 + tq - 1) // tk
    lax.fori_loop(0, c_last, lambda c, carry: (chunk(c, False), carry)[1], 0)
    chunk(c_last, True)
    for p in range(n_pairs):
        out = jnp.concatenate([_online_out(l_ref, acc_ref, p * n_rb + r) for r in range(n_rb)], axis=0)
        o_ref[:, p * LANES:(p + 1) * LANES] = jnp.where(lane_h == 0, out[0:tq], out[tq:2 * tq]).astype(BF)


def _moba_prompt(mq, mobab, et, batch, t, tq):
    nq = t // tq
    assert t // MOBA_BLOCK <= 16 and tq <= MOBA_BLOCK
    tile = lambda w: pl.BlockSpec((tq, w), lambda b, i: (b * nq + i, 0))
    rb = 2 * tq
    return pl.pallas_call(
        functools.partial(_moba_prompt_kernel, tq=tq, t=t, tk=min(512, t), rb=rb),
        grid=(batch, nq),
        in_specs=[tile(384), pl.BlockSpec((t, 768), lambda b, i: (b, 0)), pl.BlockSpec(et.shape, lambda b, i: (0, 0))],
        out_specs=tile(384),
        out_shape=jax.ShapeDtypeStruct((batch * t, 384), BF),
        scratch_shapes=[pltpu.VMEM((LANES, 384), BF), pltpu.VMEM((LANES, tq), F32)]
                       + [pltpu.VMEM((MOBA_HEADS * tq // rb, rb, LANES), F32)] * 3,
        compiler_params=_cparams("parallel", "arbitrary"),
        name="moba_prompt",
    )(mq, mobab, et)


def _pool_mix(win_sums, x, count_pos, pw_ref, sc_ref):
    lane_g = lax.broadcasted_iota(jnp.int32, (1, POOL_CH), 1) >> 6
    s = win_sums[POOL_WINDOWS[-1]]
    w = jnp.full((1, POOL_CH), POOL_WINDOWS[-1], jnp.int32)
    for gi in range(len(POOL_WINDOWS) - 2, -1, -1):
        s = jnp.where(lane_g == gi, win_sums[POOL_WINDOWS[gi]], s)
        w = jnp.where(lane_g == gi, POOL_WINDOWS[gi], w)
    count = jnp.minimum(count_pos, w).astype(F32)
    d = s / count - x
    return _dot(d.astype(BF), pw_ref[...]) * sc_ref[...]


def _pool_prompt_kernel(x_ref, halo_ref, pw_ref, sc_ref, o_ref, buf_ref, *, tp):
    i = pl.program_id(1)
    x = x_ref[...]
    buf_ref[0:16, :] = jnp.where(i > 0, halo_ref[...], 0.0)
    buf_ref[16:16 + tp, :] = x
    acc = x
    sums = {}
    for k in range(1, POOL_WINDOWS[-1]):
        acc = acc + buf_ref[16 - k:16 - k + tp, :]
        if k + 1 in POOL_WINDOWS:
            sums[k + 1] = acc
    pos1 = i * tp + lax.broadcasted_iota(jnp.int32, (tp, 1), 0) + 1
    o_ref[...] = _pool_mix(sums, x, pos1, pw_ref, sc_ref).astype(BF)


def _pool_prompt(xpool, pw, sc, layer, batch, t, tp):
    nt = t // tp
    return pl.pallas_call(
        functools.partial(_pool_prompt_kernel, tp=tp),
        grid=(batch, nt),
        in_specs=[pl.BlockSpec((tp, POOL_CH), lambda b, i: (b * nt + i, 0)),
                  pl.BlockSpec((16, POOL_CH), lambda b, i: (jnp.maximum((b * nt + i) * (tp // 16) - 1, 0), 0)),
                  _const_spec((POOL_CH, POOL_CH), layer), _const_spec((1, POOL_CH), layer)],
        out_specs=pl.BlockSpec((tp, POOL_CH), lambda b, i: (b * nt + i, 0)),
        out_shape=jax.ShapeDtypeStruct((batch * t, POOL_CH), BF),
        scratch_shapes=[pltpu.VMEM((tp + 16, POOL_CH), F32)],
        compiler_params=_cparams("parallel", "parallel"),
        name="pool_prompt",
    )(xpool, xpool, pw, sc)


def _sample_kernel(pt_ref, qc_ref, qr_ref, gate_ref, mq_ref, nsa_new_ref, win_new_ref, moba_new_ref, xpool_ref,
                   winst_ref, poolst_ref,
                   posk_ref, w1k_ref, w2k_ref, posv_ref, w1v_ref, w2v_ref, gkc_ref,
                   ovl_ref, gexp_ref, esel_ref, emoba_ref, pw_ref, sc_ref, *rest, n_pages, page, past):
    kc_pages = rest[:n_pages]
    vc_pages = rest[n_pages:2 * n_pages]
    sel_pages = rest[2 * n_pages:3 * n_pages]
    moba_pages = rest[3 * n_pages:4 * n_pages]
    onsa_ref, omoba_ref, ypool_ref = rest[4 * n_pages:]
    del pt_ref
    row = lax.broadcasted_iota(jnp.int32, (8, 1), 0)
    lane = lax.broadcasted_iota(jnp.int32, (1, LANES), 1)
    lane_h = lane >> 6
    row_h = jnp.where(row >= NSA_GROUP, 1, 0) + jnp.where(row >= 2 * NSA_GROUP, 1, 0)
    row_g = row - row_h * NSA_GROUP
    live6 = row < NSA_HEADS
    n_groups = past // CMP_STRIDE
    gpp = page // CMP_STRIDE

    def stack_q(ref):
        s = [ref[:, g * LANES:(g + 1) * LANES].astype(F32) for g in range(NSA_GROUP)]
        q = jnp.where(row_g == 0, s[0], jnp.where(row_g == 1, s[1], s[2]))
        return jnp.where(live6 & (row_h == lane_h), q, 0.0)

    def by_group(o):
        return [jnp.sum(jnp.where(live6 & (row == lane_h * NSA_GROUP + g), o, 0.0), axis=0, keepdims=True)
                for g in range(NSA_GROUP)]

    def strided(pages, l):
        return jnp.concatenate([pg[pl.ds(l, gpp, stride=CMP_STRIDE), :] for pg in pages], axis=0)

    kraw = _compress_tokens(lambda l: strided(kc_pages, l), n_groups, posk_ref, w1k_ref, w2k_ref)
    kcmp = _head_norm(kraw, gkc_ref[...], _ones_bd())
    vcmp = _compress_tokens(lambda l: strided(vc_pages, l), n_groups, posv_ref, w1v_ref, w2v_ref)

    qc8 = stack_q(qc_ref)
    c_end = lax.broadcasted_iota(jnp.int32, (1, n_groups), 1) * CMP_STRIDE + CMP_LEN
    p_c = _softmax_rows(_dot3_nt(qc8, kcmp) * SCALE, c_end <= past + 1)
    o_cmp = _dot(p_c.astype(BF), vcmp.astype(BF))
    imp8 = _dot_sel(p_c, ovl_ref[...])
    imp = jnp.sum(jnp.where(live6 & (row_h == lane_h), imp8, 0.0), axis=0, keepdims=True)
    jblk = lane & (SEL_BLOCK - 1)
    own = past // SEL_BLOCK
    sel = _rank_select(jnp.broadcast_to(imp, (8, LANES)), jblk, own, own + 1, SEL_TOPN)[0:1]
    sel8 = jnp.where(live6 & (row_h == lane_h), sel, 0.0)

    qr8 = stack_q(qr_ref)
    qr8b = qr8.astype(BF)
    nsa_new = nsa_new_ref[...]
    picked = _dot(sel8.astype(BF), esel_ref[...]) > 0.5
    s = jnp.concatenate([_dot_nt(qr8b, pg[:, 0:LANES]) for pg in sel_pages], axis=1)
    s = jnp.where(picked, s, NEG)
    own_picked = jnp.sum(jnp.where(jblk == own, sel8, 0.0), axis=1, keepdims=True) > 0.5
    s_new = jnp.where(own_picked, jnp.sum(qr8 * nsa_new[:, 2 * LANES:3 * LANES], axis=1, keepdims=True), NEG)
    m = jnp.maximum(jnp.maximum(jnp.max(s, axis=1, keepdims=True), s_new), M_INIT)
    e = jnp.exp2(s - m)
    e_new = jnp.exp2(s_new - m)
    den = jnp.sum(e, axis=1, keepdims=True) + e_new
    eb = e.astype(BF)
    acc = e_new * nsa_new[:, 3 * LANES:4 * LANES]
    for pi, pg in enumerate(sel_pages):
        acc = acc + _dot(eb[:, pi * page:(pi + 1) * page], pg[:, LANES:2 * LANES])
    o_sel = jnp.where(den > 0.0, acc / den, 0.0)

    win_new = win_new_ref[...]
    n_win = winst_ref.shape[0]
    kpos_w = past - n_win + lax.broadcasted_iota(jnp.int32, (1, n_win), 1)
    ok_w = (past - kpos_w) < WIN
    s = jnp.where(ok_w, _dot_nt(qr8b, winst_ref[:, 0:LANES].astype(BF)), NEG)
    s_new = jnp.sum(qr8 * win_new[:, 0:LANES], axis=1, keepdims=True)
    m = jnp.maximum(jnp.max(s, axis=1, keepdims=True), s_new)
    e = jnp.exp2(s - m)
    e_new = jnp.exp2(s_new - m)
    den = jnp.sum(e, axis=1, keepdims=True) + e_new
    o_win = (_dot(e.astype(BF), winst_ref[:, LANES:2 * LANES].astype(BF)) + e_new * win_new[:, LANES:2 * LANES]) / den

    gate8 = jnp.broadcast_to(gate_ref[...], (8, LANES))
    gh = gate8.astype(BF)
    gl = (gate8 - gh.astype(F32)).astype(BF)
    gexp = (_dot(gh, gexp_ref[...]) + _dot(gl, gexp_ref[...]))[0:1]
    branches = [by_group(o_cmp), by_group(o_sel), by_group(o_win)]
    for g in range(NSA_GROUP):
        out = jnp.zeros((1, LANES), F32)
        for br in range(3):
            out = out + gexp[:, (g * 3 + br) * LANES:(g * 3 + br + 1) * LANES] * branches[br][g]
        onsa_ref[:, g * LANES:(g + 1) * LANES] = out.astype(BF)

    mq = mq_ref[...].astype(F32)
    lane3 = lax.broadcasted_iota(jnp.int32, (1, 384), 1)
    mq8 = jnp.where(row == (lane3 >> 6), mq, 0.0)
    mq8b = mq8.astype(BF)
    n_blk = past // MOBA_BLOCK
    ppb = MOBA_BLOCK // page
    means = []
    for bi in range(n_blk):
        tot = jnp.zeros((1, 384), F32)
        for pg in moba_pages[bi * ppb:(bi + 1) * ppb]:
            tot = tot + jnp.sum(pg[:, 0:384].astype(F32), axis=0, keepdims=True)
        means.append(tot * (1.0 / MOBA_BLOCK))
    r128 = lax.broadcasted_iota(jnp.int32, (LANES, 1), 0)
    km = jnp.zeros((LANES, 384), F32)
    for bi in range(n_blk):
        km = jnp.where(r128 == bi, means[bi], km)
    gate_m = _dot_nt(mq8b, km.astype(BF))
    cnt = jnp.zeros((8, LANES), F32)
    for d in range(1, n_blk):
        lower = pltpu.roll(gate_m, d, axis=1)
        upper = pltpu.roll(gate_m, LANES - d, axis=1)
        cnt = cnt + jnp.where((lane >= d) & (lower >= gate_m), 1.0, 0.0)
        cnt = cnt + jnp.where((lane + d < n_blk) & (upper > gate_m), 1.0, 0.0)
    pick_m = jnp.where((lane < n_blk) & (cnt < MOBA_TOPK), 1.0, 0.0)
    ok_m = _dot(pick_m.astype(BF), emoba_ref[...]) > 0.5
    moba_new = moba_new_ref[...]
    s = jnp.concatenate([_dot_nt(mq8b, pg[:, 0:384]) for pg in moba_pages], axis=1)
    s = jnp.where(ok_m, s, NEG)
    s_new = jnp.sum(mq8 * moba_new[:, 0:384], axis=1, keepdims=True)
    m = jnp.maximum(jnp.max(s, axis=1, keepdims=True), s_new)
    e = jnp.exp2(s - m)
    e_new = jnp.exp2(s_new - m)
    den = jnp.sum(e, axis=1, keepdims=True) + e_new
    eb = e.astype(BF)
    acc = e_new * moba_new[:, 384:768]
    for pi, pg in enumerate(moba_pages):
        acc = acc + _dot(eb[:, pi * page:(pi + 1) * page], pg[:, 384:768])
    o_m = acc / den
    omoba_ref[...] = jnp.sum(jnp.where(row == (lane3 >> 6), o_m, 0.0), axis=0, keepdims=True).astype(BF)

    x = xpool_ref[...]
    hist = poolst_ref[...]
    hrow = lax.broadcasted_iota(jnp.int32, (POOL_HIST, 1), 0)
    sums = {w: x + jnp.sum(jnp.where(hrow >= POOL_HIST - (w - 1), hist, 0.0), axis=0, keepdims=True)
            for w in POOL_WINDOWS}
    y = _pool_mix(sums, x, jnp.full((1, 1), past + 1, jnp.int32), pw_ref, sc_ref)
    ypool_ref[...] = y.astype(BF)


def _sample(page_table, rows, winst, poolst, cache_cmp, cache_sel, cache_moba, cw, consts, pw, sc, layer, past):
    n_seq, n_pages = page_table.shape
    page = cache_cmp.shape[2]
    per_seq = lambda w: pl.BlockSpec((None, 1, w), lambda i, pt: (i, 0, 0))
    lconst = lambda shape: pl.BlockSpec((None,) + tuple(shape), lambda i, pt: (layer,) + (0,) * len(shape))
    const = lambda a: pl.BlockSpec(a.shape, lambda i, pt: (0,) * a.ndim)
    cmp_specs = [lconst((CMP_LEN, LANES)), lconst((CMP_LEN, LANES, LANES)), lconst((LANES, LANES))]

    def page_spec(width, lane_block, pj):
        return pl.BlockSpec((None, None, page, width), lambda i, pt: (layer, pt[i, pj], 0, lane_block))

    in_specs = ([per_seq(384), per_seq(384), per_seq(LANES), per_seq(384), per_seq(512), per_seq(256), per_seq(768),
                 per_seq(POOL_CH),
                 pl.BlockSpec((None, None, winst.shape[2], 256), lambda i, pt: (layer, i, 0, 0)),
                 pl.BlockSpec((None, None, POOL_HIST, POOL_CH), lambda i, pt: (layer, i, 0, 0))]
                + cmp_specs + cmp_specs + [lconst((1, LANES))]
                + [const(consts["ovl_s"]), const(consts["gexp"]), const(consts["esel"]), const(consts["emoba"]),
                   lconst((POOL_CH, POOL_CH)), lconst((1, POOL_CH))]
                + [page_spec(LANES, 0, pj) for pj in range(n_pages)] + [page_spec(LANES, 1, pj) for pj in range(n_pages)]
                + [page_spec(2 * LANES, 0, pj) for pj in range(n_pages)]
                + [page_spec(768, 0, pj) for pj in range(n_pages)])
    grid_spec = pltpu.PrefetchScalarGridSpec(
        num_scalar_prefetch=1, grid=(n_seq,), in_specs=in_specs,
        out_specs=[per_seq(384), per_seq(384), per_seq(POOL_CH)])
    return pl.pallas_call(
        functools.partial(_sample_kernel, n_pages=n_pages, page=page, past=past),
        grid_spec=grid_spec,
        out_shape=[jax.ShapeDtypeStruct((n_seq, 1, 384), BF), jax.ShapeDtypeStruct((n_seq, 1, 384), BF),
                   jax.ShapeDtypeStruct((n_seq, 1, POOL_CH), BF)],
        compiler_params=_cparams("parallel"),
        name="sample_step",
    )(page_table, rows["qc"], rows["qr"], rows["gate"], rows["mq"], rows["nsa"], rows["win"], rows["moba"],
      rows["xpool"], winst, poolst,
      cw["posk"], cw["w1k"], cw["w2k"], cw["posv"], cw["w1v"], cw["w2v"], cw["gkc"],
      consts["ovl_s"], consts["gexp"], consts["esel"], consts["emoba"], pw, sc,
      *([cache_cmp] * (2 * n_pages)), *([cache_sel] * n_pages), *([cache_moba] * n_pages))


def _combine_ffn_kernel(x_ref, yp_ref, on_ref, om_ref, w_ref, g_ref, wg_ref, wu_ref, wd_ref, o_ref, *, f_chunk):
    y = _dot(yp_ref[...], w_ref[0:POOL_CH, :]) + _dot(on_ref[...], w_ref[POOL_CH:POOL_CH + 384, :])
    y = y + _dot(om_ref[...], w_ref[POOL_CH + 384:POOL_CH + 768, :])
    o_ref[...] = _ffn_body(x_ref[...] + y, g_ref[...], wg_ref, wu_ref, wd_ref, f_chunk)


def _combine_ffn(x, ypool, onsa, omoba, w_out, g, wg, wu, wd, layer, tm):
    n, d = x.shape
    d_ff = wg.shape[2]
    row = lambda w: pl.BlockSpec((tm, w), lambda i: (i, 0))
    return pl.pallas_call(
        functools.partial(_combine_ffn_kernel, f_chunk=_f_chunk(d_ff)),
        grid=(pl.cdiv(n, tm),),
        in_specs=[row(d), row(POOL_CH), row(384), row(384), _const_spec((POOL_CH + 768, d), layer),
                  _const_spec((1, d), layer),
                  _const_spec((d, d_ff), layer), _const_spec((d, d_ff), layer), _const_spec((d_ff, d), layer)],
        out_specs=row(d),
        out_shape=jax.ShapeDtypeStruct((n, d), F32),
        compiler_params=_cparams("parallel"),
        name="combine_ffn",
    )(x, ypool, onsa, omoba, w_out, g, wg, wu, wd)


def _in_col_perm():
    cols = list(range(0, POOL_CH))
    for g in range(NSA_GROUP):
        for h in range(NSA_KV_HEADS):
            base = POOL_CH + (h * NSA_GROUP + g) * HEAD_DIM
            cols += range(base, base + HEAD_DIM)
    nsa_end = POOL_CH + NSA_HEADS * HEAD_DIM
    kv_end = nsa_end + 6 * NSA_KV_HEADS * HEAD_DIM
    cols += range(nsa_end, kv_end)
    cols += range(kv_end + GATE_COLS, kv_end + GATE_COLS + 3 * MOBA_HEADS * HEAD_DIM)
    gate_cols = list(range(kv_end, kv_end + GATE_COLS))
    return np.array(cols, np.int32), np.array(gate_cols, np.int32)


def _out_row_perm():
    rows = list(range(0, POOL_CH))
    for g in range(NSA_GROUP):
        for h in range(NSA_KV_HEADS):
            base = POOL_CH + (h * NSA_GROUP + g) * HEAD_DIM
            rows += range(base, base + HEAD_DIM)
    rows += range(POOL_CH + NSA_HEADS * HEAD_DIM, POOL_CH + NSA_HEADS * HEAD_DIM + MOBA_HEADS * HEAD_DIM)
    return np.array(rows, np.int32)


def _overlap(n_cmp_rows, n_cmp_valid, n_blocks):
    n = np.arange(n_cmp_rows)
    c_start = n * CMP_STRIDE
    c_end = c_start + CMP_LEN
    jb = np.arange(SEL_BLOCK)
    ov = (c_start[:, None] < (jb[None, :] + 1) * SEL_BLOCK) & (c_end[:, None] > jb[None, :] * SEL_BLOCK)
    ov &= (n[:, None] < n_cmp_valid) & (jb[None, :] < n_blocks)
    return ov.astype(np.float32)


def _constants(t, past):
    g_p = t // CMP_STRIDE
    ov = _overlap(g_p, g_p - 1, t // SEL_BLOCK)
    ovl_p = np.zeros((2 * g_p, LANES), np.float32)
    ovl_p[:g_p, :SEL_BLOCK] = ov
    ovl_p[g_p:, SEL_BLOCK:] = ov
    g_s = past // CMP_STRIDE
    ov_s = _overlap(g_s, g_s - 1, past // SEL_BLOCK + 1)
    ovl_s = np.concatenate([ov_s, ov_s], axis=1)
    lane = np.arange(LANES)
    gexp = np.zeros((LANES, 9 * LANES), np.float32)
    for g in range(NSA_GROUP):
        for br in range(3):
            src = (lane // HEAD_DIM) * 9 + g * 3 + br
            gexp[src, (g * 3 + br) * LANES + lane] = 1.0
    kp = np.arange(past)
    esel = ((lane[:, None] % SEL_BLOCK) == (kp[None, :] // SEL_BLOCK)).astype(np.float32)
    emoba = (lane[:, None] == (kp[None, :] // MOBA_BLOCK)).astype(np.float32)
    kt = np.arange(t)
    et_sel = ((kt[:, None] // SEL_BLOCK) == (lane[None, :] % SEL_BLOCK)).astype(np.float32)
    et_moba = ((kt[:, None] // MOBA_BLOCK) == (lane[None, :] % 16)).astype(np.float32)
    as_bf = lambda a: jnp.asarray(a, BF)
    return {"ovlt_p": as_bf(ovl_p.T), "ovl_s": as_bf(ovl_s), "gexp": as_bf(gexp), "esel": as_bf(esel),
            "emoba": as_bf(emoba), "et_sel": as_bf(et_sel), "et_moba": as_bf(et_moba)}


def _rope_tables(pos):
    half = ROPE_DIM // 2
    inv_freq = ROPE_THETA ** (-jnp.arange(half, dtype=F32) / half)
    ang = pos.astype(F32)[:, None] * inv_freq[None, :]
    cos, sin, zero = jnp.cos(ang), jnp.sin(ang), jnp.zeros_like(ang)
    rest = HEAD_DIM - ROPE_DIM
    c = jnp.concatenate([cos, cos, jnp.ones((ang.shape[0], rest), F32)], axis=1)
    a = jnp.concatenate([-sin, zero, jnp.zeros((ang.shape[0], rest), F32)], axis=1)
    b = jnp.concatenate([zero, sin, jnp.zeros((ang.shape[0], rest), F32)], axis=1)
    tile2 = lambda m: jnp.concatenate([m, m], axis=1)
    return tile2(c), tile2(a), tile2(b)


def _cmp_weights(cmp_pos, cmp_w1, cmp_w2, g_nsa_kc):
    bd = lambda w: jnp.concatenate([jnp.concatenate([w, jnp.zeros_like(w)], axis=-1),
                                    jnp.concatenate([jnp.zeros_like(w), w], axis=-1)], axis=-2)
    out = {}
    for i, name in enumerate(("k", "v")):
        out["pos" + name] = jnp.concatenate([cmp_pos[:, i], cmp_pos[:, i]], axis=-1)
        out["w1" + name] = bd(cmp_w1[:, i]).astype(BF)
        out["w2" + name] = bd(cmp_w2[:, i]).astype(BF)
    out["gkc"] = jnp.concatenate([g_nsa_kc, g_nsa_kc], axis=-1)[:, None, :]
    return out


def kernel(x_prompt, x_sample, cache_nsa_kv, cache_moba_kv, state_nsa_win, state_pool, page_table, g_ffa, w_ffa_gate, w_ffa_up, w_ffa_down, g_mix, w_in, w_out, pool_w, pool_scale, g_nsa_q, g_nsa_kc, g_nsa_ks, g_nsa_kw, cmp_pos, cmp_w1, cmp_w2, g_moba_q, g_moba_k, g_ffb, w_ffb_gate, w_ffb_up, w_ffb_down):
    batch, t, d = x_prompt.shape
    n_seq = x_sample.shape[0]
    depth, n_pool, page = cache_nsa_kv.shape[:3]
    n_pages = page_table.shape[1]
    past = n_pages * page
    n_p = batch * t
    n_win = state_nsa_win.shape[2]
    assert x_sample.shape[1] == 1 and t % MOBA_BLOCK == 0 and past % MOBA_BLOCK == 0 and n_win == WIN

    tm = next((c for c in (512, 384, 256) if (n_p + n_seq) % c == 0), 512)
    tp = next(c for c in (512, 256) if t % c == 0)
    tq_nsa = 128
    tq_moba = MOBA_BLOCK

    col_perm, gate_cols = _in_col_perm()
    w_in_p = jnp.concatenate([w_in[:, :, col_perm], w_in[:, :, gate_cols],
                              jnp.zeros((depth, d, LANES - GATE_COLS), w_in.dtype)], axis=-1).astype(BF)
    w_out_p = w_out[:, _out_row_perm(), :].astype(BF)
    bf = lambda w: w.astype(BF)
    wa = (bf(w_ffa_gate), bf(w_ffa_up), bf(w_ffa_down))
    wb = (bf(w_ffb_gate), bf(w_ffb_up), bf(w_ffb_down))
    tile2 = lambda g: jnp.concatenate([g, g], axis=-1)
    gains = jnp.stack([tile2(g_nsa_q), tile2(g_nsa_ks), tile2(g_nsa_kw), tile2(g_moba_q), tile2(g_moba_k)]
                      + [jnp.ones((depth, LANES), F32)] * 3, axis=1)
    cw = _cmp_weights(cmp_pos, cmp_w1, cmp_w2, g_nsa_kc)
    eye = jnp.eye(len(POOL_WINDOWS), dtype=pool_w.dtype)
    pw_bd = jnp.einsum("lgij,gh->lgihj", pool_w, eye).reshape(depth, POOL_CH, POOL_CH).astype(BF)
    sc = pool_scale[:, None, :]
    consts = _constants(t, past)
    pos = jnp.concatenate([jnp.tile(jnp.arange(t, dtype=jnp.int32), batch), jnp.full((n_seq,), past, jnp.int32)])
    rope_c, rope_a, rope_b = _rope_tables(pos)

    cache_cmp = cache_nsa_kv[:, :, :, 0:2].reshape(depth, n_pool, page, 256)
    cache_sel = cache_nsa_kv[:, :, :, 2:4].astype(BF).reshape(depth, n_pool, page, 256)
    cache_moba = cache_moba_kv.astype(BF).reshape(depth, n_pool, page, 768)
    winst = state_nsa_win.reshape(depth, n_seq, n_win, 256)

    x = jnp.concatenate([x_prompt.reshape(n_p, d), x_sample.reshape(n_seq, d)], axis=0)
    outs = [[] for _ in range(8)]
    for l in range(depth):
        x = _ffn(x, g_ffa[:, None, :], *wa, l, tm)
        (xpool, qc, qr, nsa, nsab, win, winb, mq, moba, mobab, gate) = _prep(
            x, g_mix[:, None, :], w_in_p, gains, rope_c, rope_a, rope_b, l, tm)

        kcmp, vcmp = _compress_prompt(nsa, cw, l, batch, t)
        onsa_p = _nsa_prompt(qc, qr, gate, kcmp, vcmp, nsab, winb, consts["ovlt_p"], consts["gexp"], consts["et_sel"],
                             batch, t, tq_nsa)
        omoba_p = _moba_prompt(mq, mobab, consts["et_moba"], batch, t, tq_moba)
        ypool_p = _pool_prompt(xpool, pw_bd, sc, l, batch, t, tp)

        srow = lambda a: a[n_p:].reshape(n_seq, 1, a.shape[1])
        rows = {"qc": srow(qc), "qr": srow(qr), "gate": srow(gate), "mq": srow(mq), "nsa": srow(nsa),
                "win": srow(win), "moba": srow(moba), "xpool": srow(xpool)}
        onsa_s, omoba_s, ypool_s = _sample(page_table, rows, winst, state_pool, cache_cmp, cache_sel, cache_moba, cw,
                                           consts, pw_bd, sc, l, past)

        cat = lambda p, s: jnp.concatenate([p, s.reshape(n_seq, s.shape[2])], axis=0)
        x = _combine_ffn(x, cat(ypool_p, ypool_s), cat(onsa_p, onsa_s), cat(omoba_p, omoba_s), w_out_p,
                         g_ffb[:, None, :], *wb, l, tm)

        outs[0].append(nsa[:n_p].reshape(batch, t, 4, NSA_KV_HEADS, HEAD_DIM))
        outs[1].append(nsa[n_p:].reshape(n_seq, 1, 4, NSA_KV_HEADS, HEAD_DIM))
        outs[2].append(moba[:n_p].reshape(batch, t, 2, MOBA_HEADS, HEAD_DIM))
        outs[3].append(moba[n_p:].reshape(n_seq, 1, 2, MOBA_HEADS, HEAD_DIM))
        outs[4].append(win[:n_p].reshape(batch, t, 2, NSA_KV_HEADS, HEAD_DIM)[:, t - min(WIN, t):])
        new_win = win[n_p:].reshape(n_seq, 1, 2, NSA_KV_HEADS, HEAD_DIM)
        outs[5].append(jnp.concatenate([state_nsa_win[l], new_win], axis=1)[:, 1:])
        outs[6].append(xpool[:n_p].reshape(batch, t, POOL_CH)[:, t - POOL_HIST:])
        outs[7].append(jnp.concatenate([state_pool[l], xpool[n_p:].reshape(n_seq, 1, POOL_CH)], axis=1)[:, 1:])

    stacked = [jnp.stack(o, axis=0) for o in outs]
    return (x[:n_p].reshape(batch, t, d), x[n_p:].reshape(n_seq, 1, d), *stacked)
```

```python
import functools

import numpy as np
import jax
import jax.numpy as jnp
from jax import lax
from jax.experimental import pallas as pl
from jax.experimental.pallas import tpu as pltpu

F32 = jnp.float32
BF = jnp.bfloat16

HEAD_DIM = 64
LANES = 128
POOL_CH = 256
POOL_WINDOWS = (2, 4, 8, 16)
POOL_GROUP = 64
POOL_HIST = 15
NSA_KV_HEADS = 2
NSA_GROUP = 3
NSA_HEADS = 6
MOBA_HEADS = 6
CMP_LEN = 32
CMP_STRIDE = 16
SEL_BLOCK = 64
SEL_TOPN = 16
WIN = 512
MOBA_BLOCK = 256
MOBA_TOPK = 3
ROPE_THETA = 500000.0
ROPE_DIM = 16
EPS = 1e-6
NEG = -1e30
M_INIT = -1e29
SEL_FORCE = 1e4
SCALE = HEAD_DIM ** -0.5
Q_PRESCALE = SCALE * 1.4426950408889634
GATE_COLS = 3 * NSA_HEADS
D_QKV = 2560
D_IN_PAD = D_QKV + LANES
VMEM_LIMIT = 56 * 1024 * 1024


def _cparams(*sem):
    return pltpu.CompilerParams(dimension_semantics=sem, vmem_limit_bytes=VMEM_LIMIT)


def _dot(a, b):
    return jnp.dot(a, b, preferred_element_type=F32)


def _dot_nt(a, b):
    return lax.dot_general(a, b, (((1,), (1,)), ((), ())), preferred_element_type=F32)


def _split3(x):
    hi = x.astype(BF)
    r = x - hi.astype(F32)
    mid = r.astype(BF)
    lo = (r - mid.astype(F32)).astype(BF)
    return hi, mid, lo


def _dot_sel(x, m):
    hi, mid, lo = _split3(x)
    return _dot(hi, m) + _dot(mid, m) + _dot(lo, m)


def _dot3_nt(a, b):
    ah = a.astype(BF)
    al = (a - ah.astype(F32)).astype(BF)
    bh = b.astype(BF)
    bl = (b - bh.astype(F32)).astype(BF)
    return _dot_nt(ah, bh) + _dot_nt(ah, bl) + _dot_nt(al, bh)


def _sigmoid(x):
    return 1.0 / (1.0 + jnp.exp(-x))


def _silu(x):
    return x * _sigmoid(x)


def _rms_rows(x, g):
    return x * lax.rsqrt(jnp.mean(x * x, axis=-1, keepdims=True) + EPS) * g


def _ones_bd():
    r = lax.broadcasted_iota(jnp.int32, (LANES, LANES), 0) >> 6
    c = lax.broadcasted_iota(jnp.int32, (LANES, LANES), 1) >> 6
    return jnp.where(r == c, 1.0, 0.0).astype(BF)


def _head_norm(x, gain, ones_bd):
    ms = _dot_sel(x * x, ones_bd) * (1.0 / HEAD_DIM)
    return x * lax.rsqrt(ms + EPS) * gain


def _rope(x, c, a, b):
    return x * c + pltpu.roll(x, LANES - ROPE_DIM // 2, axis=1) * a + pltpu.roll(x, ROPE_DIM // 2, axis=1) * b


def _softmax_rows(s, valid):
    s = jnp.where(valid, s, NEG)
    m = jnp.max(s, axis=1, keepdims=True)
    e = jnp.where(valid, jnp.exp(s - m), 0.0)
    den = jnp.sum(e, axis=1, keepdims=True)
    return jnp.where(den > 0.0, e / den, 0.0)


def _ffn_body(x, g, wg_ref, wu_ref, wd_ref, f_chunk):
    xn = _rms_rows(x, g).astype(BF)
    d_ff = wg_ref.shape[1]
    acc = jnp.zeros(x.shape, F32)
    for c in range(d_ff // f_chunk):
        sl = slice(c * f_chunk, (c + 1) * f_chunk)
        gate = _dot(xn, wg_ref[:, sl])
        up = _dot(xn, wu_ref[:, sl])
        h = (_silu(gate) * up).astype(BF)
        acc = acc + _dot(h, wd_ref[sl, :])
    return x + 0.5 * acc


def _ffn_kernel(x_ref, g_ref, wg_ref, wu_ref, wd_ref, o_ref, *, f_chunk):
    o_ref[...] = _ffn_body(x_ref[...], g_ref[...], wg_ref, wu_ref, wd_ref, f_chunk)


def _f_chunk(d_ff):
    for c in (1408, 1024, 512, 256, 128):
        if d_ff % c == 0:
            return c
    return d_ff


def _const_spec(shape, layer):
    nd = len(shape)
    return pl.BlockSpec((None,) + tuple(shape), lambda *_: (layer,) + (0,) * nd, pipeline_mode=pl.Buffered(1))


def _ffn(x, g, wg, wu, wd, layer, tm):
    n, d = x.shape
    d_ff = wg.shape[2]
    return pl.pallas_call(
        functools.partial(_ffn_kernel, f_chunk=_f_chunk(d_ff)),
        grid=(pl.cdiv(n, tm),),
        in_specs=[pl.BlockSpec((tm, d), lambda i: (i, 0)),
                  _const_spec((1, d), layer),
                  _const_spec((d, d_ff), layer), _const_spec((d, d_ff), layer), _const_spec((d_ff, d), layer)],
        out_specs=pl.BlockSpec((tm, d), lambda i: (i, 0)),
        out_shape=jax.ShapeDtypeStruct((n, d), F32),
        compiler_params=_cparams("parallel"),
        name="ffn",
    )(x, g, wg, wu, wd)


def _prep_kernel(x_ref, g_ref, w_ref, gains_ref, c_ref, a_ref, b_ref,
                 xpool_ref, qc_ref, qr_ref, nsa_ref, nsab_ref, win_ref, winb_ref,
                 mq_ref, moba_ref, mobab_ref, gate_ref):
    xn = _rms_rows(x_ref[...], g_ref[...]).astype(BF)
    z = _dot(xn, w_ref[...])
    ones_bd = _ones_bd()
    c, a, b = c_ref[...], a_ref[...], b_ref[...]
    gains = gains_ref[...]
    slab = lambda i: z[:, i * LANES:(i + 1) * LANES]
    norm_rope = lambda s, gi: _rope(_head_norm(s, gains[gi:gi + 1], ones_bd), c, a, b)

    xpool_ref[...] = z[:, 0:POOL_CH]
    for gi in range(3):
        qn = _head_norm(slab(2 + gi), gains[0:1], ones_bd)
        qc_ref[:, gi * LANES:(gi + 1) * LANES] = qn
        qr_ref[:, gi * LANES:(gi + 1) * LANES] = (_rope(qn, c, a, b) * Q_PRESCALE).astype(BF)
    nsa = [slab(5), slab(6), norm_rope(slab(7), 1), slab(8)]
    for i, s in enumerate(nsa):
        nsa_ref[:, i * LANES:(i + 1) * LANES] = s
        nsab_ref[:, i * LANES:(i + 1) * LANES] = s.astype(BF)
    win = [norm_rope(slab(9), 2), slab(10)]
    for i, s in enumerate(win):
        win_ref[:, i * LANES:(i + 1) * LANES] = s
        winb_ref[:, i * LANES:(i + 1) * LANES] = s.astype(BF)
    for i in range(3):
        mq_ref[:, i * LANES:(i + 1) * LANES] = (norm_rope(slab(11 + i), 3) * Q_PRESCALE).astype(BF)
    for i in range(6):
        s = norm_rope(slab(14 + i), 4) if i < 3 else slab(14 + i)
        moba_ref[:, i * LANES:(i + 1) * LANES] = s
        mobab_ref[:, i * LANES:(i + 1) * LANES] = s.astype(BF)
    gate_ref[...] = _sigmoid(slab(20))


def _prep(x, g, w_in, gains, rope_c, rope_a, rope_b, layer, tm):
    n, d = x.shape
    row = lambda w: pl.BlockSpec((tm, w), lambda i: (i, 0))
    widths_dtypes = [(POOL_CH, F32), (384, F32), (384, BF), (512, F32), (512, BF), (256, F32), (256, BF),
                     (384, BF), (768, F32), (768, BF), (LANES, F32)]
    return pl.pallas_call(
        _prep_kernel,
        grid=(pl.cdiv(n, tm),),
        in_specs=[row(d), _const_spec((1, d), layer), _const_spec((d, D_IN_PAD), layer),
                  _const_spec((8, LANES), layer), row(LANES), row(LANES), row(LANES)],
        out_specs=[row(w) for w, _ in widths_dtypes],
        out_shape=[jax.ShapeDtypeStruct((n, w), dt) for w, dt in widths_dtypes],
        compiler_params=_cparams("parallel"),
        name="prep",
    )(x, g, w_in, gains, rope_c, rope_a, rope_b)


def _compress_tokens(load_rows, n_groups, pos_ref, w1_ref, w2_ref):
    pa = jnp.zeros((n_groups, LANES), F32)
    pb = jnp.zeros((n_groups, LANES), F32)
    for l in range(CMP_STRIDE):
        xl = load_rows(l)
        pa = pa + _dot((xl + pos_ref[l:l + 1, :]).astype(BF), w1_ref[l])
        pb = pb + _dot((xl + pos_ref[CMP_STRIDE + l:CMP_STRIDE + l + 1, :]).astype(BF), w1_ref[CMP_STRIDE + l])
    pre = pa + pltpu.roll(pb, n_groups - 1, axis=0)
    return _dot(_silu(pre).astype(BF), w2_ref[...])


def _compress_prompt_kernel(kc_ref, vc_ref, posk_ref, w1k_ref, w2k_ref, posv_ref, w1v_ref, w2v_ref, gkc_ref,
                            kcmp_ref, vcmp_ref, *, n_groups):
    kraw = _compress_tokens(lambda l: kc_ref[pl.ds(l, n_groups, stride=CMP_STRIDE), :],
                            n_groups, posk_ref, w1k_ref, w2k_ref)
    kcmp_ref[...] = _head_norm(kraw, gkc_ref[...], _ones_bd())
    vcmp_ref[...] = _compress_tokens(lambda l: vc_ref[pl.ds(l, n_groups, stride=CMP_STRIDE), :],
                                     n_groups, posv_ref, w1v_ref, w2v_ref)


def _cmp_weight_specs(layer):
    return [_const_spec((CMP_LEN, LANES), layer), _const_spec((CMP_LEN, LANES, LANES), layer),
            _const_spec((LANES, LANES), layer)]


def _compress_prompt(nsa_rows, cw, layer, batch, t):
    n_groups = t // CMP_STRIDE
    return pl.pallas_call(
        functools.partial(_compress_prompt_kernel, n_groups=n_groups),
        grid=(batch,),
        in_specs=[pl.BlockSpec((t, LANES), lambda b: (b, 0)), pl.BlockSpec((t, LANES), lambda b: (b, 1))]
                 + _cmp_weight_specs(layer) + _cmp_weight_specs(layer) + [_const_spec((1, LANES), layer)],
        out_specs=[pl.BlockSpec((n_groups, LANES), lambda b: (b, 0))] * 2,
        out_shape=[jax.ShapeDtypeStruct((batch * n_groups, LANES), F32)] * 2,
        compiler_params=_cparams("parallel"),
        name="compress_prompt",
    )(nsa_rows, nsa_rows, cw["posk"], cw["w1k"], cw["w2k"], cw["posv"], cw["w1v"], cw["w2v"], cw["gkc"])


def _rank_select(imp, j, own, n_blocks, top_n):
    forced = (j == 0) | (j == own) | (j == own - 1)
    imp = jnp.where(forced, imp + SEL_FORCE, imp)
    imp = jnp.where(j <= own, imp, -jnp.inf)
    cnt = jnp.zeros(imp.shape, F32)
    for d in range(1, SEL_BLOCK):
        if d > n_blocks - 1 and d < SEL_BLOCK - (n_blocks - 1):
            continue
        nowrap = j >= d
        r = jnp.where(nowrap, pltpu.roll(imp, d, axis=1), pltpu.roll(imp, (d - SEL_BLOCK) % LANES, axis=1))
        cnt = cnt + jnp.where(nowrap, jnp.where(r >= imp, 1.0, 0.0), jnp.where(r > imp, 1.0, 0.0))
    return jnp.where(j <= own, jnp.where(cnt < top_n, 1.0, 0.0), 0.0)


def _online_update(s, v, m_ref, l_ref, acc_ref, idx):
    m_prev = m_ref[idx]
    m_new = jnp.maximum(m_prev, jnp.max(s, axis=1, keepdims=True))
    alpha = jnp.exp2(m_prev - m_new)
    ps = [jnp.exp2(s[:, i * LANES:(i + 1) * LANES] - m_new) for i in range(s.shape[1] // LANES)]
    l_ref[idx] = alpha * l_ref[idx] + functools.reduce(lambda a, b: a + b, ps)
    acc_ref[idx] = alpha * acc_ref[idx] + _dot(jnp.concatenate(ps, axis=1).astype(BF), v)
    m_ref[idx] = m_new


def _online_init(m_ref, l_ref, acc_ref):
    m_ref[...] = jnp.full(m_ref.shape, M_INIT, F32)
    l_ref[...] = jnp.zeros(l_ref.shape, F32)
    acc_ref[...] = jnp.zeros(acc_ref.shape, F32)


def _online_out(l_ref, acc_ref, idx):
    l = jnp.sum(l_ref[idx], axis=1, keepdims=True)
    return jnp.where(l > 0.0, acc_ref[idx] / l, 0.0)


def _nsa_prompt_kernel(qc_ref, qr_ref, gate_ref, kcmp_ref, vcmp_ref, nsab_ref, winb_ref, ovlt_ref, gexp_ref, et_ref,
                       o_ref, impt_ref, m_ref, l_ref, acc_ref, *, tq, tk):
    qi = pl.program_id(1)
    q0 = qi * tq
    lane = lax.broadcasted_iota(jnp.int32, (1, LANES), 1)
    lane_h = lane >> 6
    qpos = q0 + lax.broadcasted_iota(jnp.int32, (tq, 1), 0)
    qpos3 = jnp.concatenate([qpos] * NSA_GROUP, axis=0)

    def stack_q(ref, h, zero):
        return jnp.concatenate(
            [jnp.where(lane_h == h, ref[:, g * LANES:(g + 1) * LANES], zero) for g in range(NSA_GROUP)], axis=0)

    kc = kcmp_ref[...]
    vc = vcmp_ref[...].astype(BF)
    n_cmp = kc.shape[0]
    c_end = lax.broadcasted_iota(jnp.int32, (1, n_cmp), 1) * CMP_STRIDE + CMP_LEN
    valid_c = c_end <= qpos3 + 1
    o_cmp, psum = [], []
    for h in range(NSA_KV_HEADS):
        p = _softmax_rows(_dot3_nt(stack_q(qc_ref, h, 0.0), kc) * SCALE, valid_c)
        o_cmp.append(_dot(p.astype(BF), vc))
        psum.append(p[0:tq] + p[tq:2 * tq] + p[2 * tq:3 * tq])
    hi, mid, lo = _split3(jnp.concatenate(psum, axis=1))
    ovlt = ovlt_ref[...]
    impt = _dot_nt(ovlt, hi) + _dot_nt(ovlt, mid) + _dot_nt(ovlt, lo)
    brow = lax.broadcasted_iota(jnp.int32, (LANES, 1), 0)
    jrow = brow & (SEL_BLOCK - 1)
    own_l = (q0 + lax.broadcasted_iota(jnp.int32, (1, tq), 1)) >> 6
    forced = (jrow == 0) | (jrow == own_l) | (jrow == own_l - 1)
    impt = jnp.where(forced, impt + SEL_FORCE, impt)
    impt = jnp.where(jrow <= own_l, impt, -jnp.inf)
    impt_ref[...] = impt

    def rank_step(jp, cnt):
        comp = jnp.where(brow < SEL_BLOCK, impt_ref[pl.ds(jp, 1), :], impt_ref[pl.ds(SEL_BLOCK + jp, 1), :])
        return cnt + jnp.where(jrow > jp, jnp.where(comp >= impt, 1.0, 0.0), jnp.where(comp > impt, 1.0, 0.0))

    n_live = (q0 + tq - 1) // SEL_BLOCK + 1
    cnt = lax.fori_loop(0, n_live, rank_step, jnp.zeros((LANES, tq), F32))
    sel_t = jnp.where(jrow <= own_l, jnp.where(cnt < SEL_TOPN, 1.0, 0.0), 0.0)
    sel_neg = jnp.where(sel_t.T > 0.5, 0.0, NEG)

    zero = jnp.zeros((), BF)
    q_rot = [stack_q(qr_ref, h, zero) for h in range(NSA_KV_HEADS)]
    q_aug = [jnp.concatenate([q_rot[h], jnp.concatenate([jnp.where(lane_h == h, sel_neg, 0.0).astype(BF)] * NSA_GROUP,
                                                         axis=0)], axis=1) for h in range(NSA_KV_HEADS)]
    _online_init(m_ref, l_ref, acc_ref)

    q_all = jnp.concatenate(q_aug, axis=0)
    qpos6 = jnp.concatenate([qpos3, qpos3], axis=0)

    def sel_chunk(c, causal):
        start = pl.multiple_of(c * tk, tk)
        k_aug = jnp.concatenate([nsab_ref[pl.ds(start, tk), 2 * LANES:3 * LANES], et_ref[pl.ds(start, tk), :]], axis=1)
        v = nsab_ref[pl.ds(start, tk), 3 * LANES:4 * LANES]
        s = _dot_nt(q_all, k_aug)
        if causal:
            kpos = start + lax.broadcasted_iota(jnp.int32, (1, tk), 1)
            s = jnp.where(kpos <= qpos6, s, NEG)
        _online_update(s, v, m_ref, l_ref, acc_ref, 0)

    c_last = (q0 + tq - 1) // tk
    lax.fori_loop(0, c_last, lambda c, carry: (sel_chunk(c, False), carry)[1], 0)
    sel_chunk(c_last, True)
    o_all = _online_out(l_ref, acc_ref, 0)
    o_sel = [o_all[h * NSA_GROUP * tq:(h + 1) * NSA_GROUP * tq] for h in range(NSA_KV_HEADS)]

    n_slots = WIN // tq + 1
    s_w = [[] for _ in range(NSA_KV_HEADS)]
    v_w = []
    for slot in range(n_slots):
        c = qi - (n_slots - 1) + slot
        start = pl.multiple_of(jnp.maximum(c, 0) * tq, tq)
        k = winb_ref[pl.ds(start, tq), 0:LANES]
        v_w.append(winb_ref[pl.ds(start, tq), LANES:2 * LANES])
        kpos = start + lax.broadcasted_iota(jnp.int32, (1, tq), 1)
        for h in range(NSA_KV_HEADS):
            s = _dot_nt(q_rot[h], k)
            if slot == 0:
                s = jnp.where(kpos > qpos3 - WIN, s, NEG)
            if slot == n_slots - 1:
                s = jnp.where(kpos <= qpos3, s, NEG)
            else:
                s = jnp.where(c >= 0, s, NEG)
            s_w[h].append(s)
    v_band = jnp.concatenate(v_w, axis=0)
    o_win = []
    for h in range(NSA_KV_HEADS):
        m = jnp.max(functools.reduce(jnp.maximum, s_w[h]), axis=1, keepdims=True)
        ps = [jnp.exp2(s - m) for s in s_w[h]]
        l = jnp.sum(functools.reduce(lambda a, b: a + b, ps), axis=1, keepdims=True)
        o_win.append(_dot(jnp.concatenate(ps, axis=1).astype(BF), v_band) / l)

    gate = gate_ref[...]
    gh = gate.astype(BF)
    gl = (gate - gh.astype(F32)).astype(BF)
    gexp = _dot(gh, gexp_ref[...]) + _dot(gl, gexp_ref[...])
    for g in range(NSA_GROUP):
        rs = slice(g * tq, (g + 1) * tq)
        out = jnp.zeros((tq, LANES), F32)
        for br, o in enumerate((o_cmp, o_sel, o_win)):
            both = jnp.where(lane_h == 0, o[0][rs], o[1][rs])
            out = out + gexp[:, (g * 3 + br) * LANES:(g * 3 + br + 1) * LANES] * both
        o_ref[:, g * LANES:(g + 1) * LANES] = out.astype(BF)


def _nsa_prompt(qc, qr, gate, kcmp, vcmp, nsab, winb, ovlt, gexp, et, batch, t, tq):
    nq = t // tq
    n_groups = t // CMP_STRIDE
    rows = NSA_GROUP * tq
    assert WIN % tq == 0 and t // SEL_BLOCK <= SEL_BLOCK
    tile = lambda w: pl.BlockSpec((tq, w), lambda b, i: (b * nq + i, 0))
    per_b = lambda r, w: pl.BlockSpec((r, w), lambda b, i: (b, 0))
    const = lambda a: pl.BlockSpec(a.shape, lambda b, i: (0, 0))
    return pl.pallas_call(
        functools.partial(_nsa_prompt_kernel, tq=tq, tk=min(512, t)),
        grid=(batch, nq),
        in_specs=[tile(384), tile(384), tile(LANES), per_b(n_groups, LANES), per_b(n_groups, LANES),
                  per_b(t, 512), per_b(t, 256), const(ovlt), const(gexp), const(et)],
        out_specs=tile(384),
        out_shape=jax.ShapeDtypeStruct((batch * t, 384), BF),
        scratch_shapes=[pltpu.VMEM((LANES, tq), F32)] + [pltpu.VMEM((1, NSA_HEADS * tq, LANES), F32)] * 3,
        compiler_params=_cparams("parallel", "parallel"),
        name="nsa_prompt",
    )(qc, qr, gate, kcmp, vcmp, nsab, winb, ovlt, gexp, et)


def _moba_prompt_kernel(mq_ref, mobab_ref, et_ref, o_ref, kmbd_ref, gate_ref, m_ref, l_ref, acc_ref, *, tq, t, tk, rb):
    qi = pl.program_id(1)
    q0 = qi * tq
    n_pairs = MOBA_HEADS // 2
    lane = lax.broadcasted_iota(jnp.int32, (1, LANES), 1)
    lane_h = lane >> 6
    hh = lane >> 4
    qpos = q0 + lax.broadcasted_iota(jnp.int32, (tq, 1), 0)
    qpos2 = jnp.concatenate([qpos, qpos], axis=0)
    brow = lax.broadcasted_iota(jnp.int32, (LANES, 1), 0)
    jrow = brow & 15

    @pl.when(qi == 0)
    def _():
        tt = lax.broadcasted_iota(jnp.int32, (1, t), 1) >> 8
        ind = jnp.where(jrow == tt, 1.0, 0.0).astype(BF)
        km = _dot(ind, mobab_ref[:, 0:384]) * (1.0 / MOBA_BLOCK)
        col_h = lax.broadcasted_iota(jnp.int32, (1, 384), 1) >> 6
        kmbd_ref[...] = jnp.where(col_h == (brow >> 4), km, 0.0).astype(BF)

    gate = _dot_nt(kmbd_ref[...], mq_ref[...])
    gate_ref[...] = gate
    own_l = (q0 + lax.broadcasted_iota(jnp.int32, (1, tq), 1)) >> 8

    def rank_step(jp, cnt):
        comp = jnp.concatenate([jnp.broadcast_to(gate_ref[pl.ds(hd * 16 + jp, 1), :], (16, tq))
                                for hd in range(LANES // 16)], axis=0)
        beats = jnp.where(jrow > jp, jnp.where(comp >= gate, 1.0, 0.0), jnp.where(comp > gate, 1.0, 0.0))
        return cnt + jnp.where(jp < own_l, beats, 0.0)

    n_past = (q0 + tq - 1) // MOBA_BLOCK
    cnt = lax.fori_loop(0, n_past, rank_step, jnp.zeros((LANES, tq), F32))
    picked_t = jnp.where(jrow == own_l, 1.0, jnp.where(jrow < own_l, jnp.where(cnt < MOBA_TOPK, 1.0, 0.0), 0.0))
    pick_neg = jnp.where(picked_t.T > 0.5, 0.0, NEG)

    zero = jnp.zeros((), BF)
    q_aug = []
    for p in range(n_pairs):
        qp = mq_ref[:, p * LANES:(p + 1) * LANES]
        q2 = jnp.concatenate([jnp.where(lane_h == 0, qp, zero), jnp.where(lane_h == 1, qp, zero)], axis=0)
        pk = jnp.concatenate([jnp.where(hh == 2 * p, pick_neg, 0.0), jnp.where(hh == 2 * p + 1, pick_neg, 0.0)], axis=0)
        q_aug.append(jnp.concatenate([q2, pk.astype(BF)], axis=1))
    _online_init(m_ref, l_ref, acc_ref)

    n_rb = 2 * tq // rb

    def chunk(c, causal):
        start = pl.multiple_of(c * tk, tk)
        et = et_ref[pl.ds(start, tk), :]
        for p in range(n_pairs):
            k_aug = jnp.concatenate([mobab_ref[pl.ds(start, tk), p * LANES:(p + 1) * LANES], et], axis=1)
            v = mobab_ref[pl.ds(start, tk), 384 + p * LANES:384 + (p + 1) * LANES]
            for r in range(n_rb):
                s = _dot_nt(q_aug[p][r * rb:(r + 1) * rb], k_aug)
                if causal:
                    kpos = start + lax.broadcasted_iota(jnp.int32, (1, tk), 1)
                    s = jnp.where(kpos <= qpos2[r * rb:(r + 1) * rb], s, NEG)
                _online_update(s, v, m_ref, l_ref, acc_ref, p * n_rb + r)

    c_last = (q0 + tq - 1) // tk
    lax.fori_loop(0, c_last, lambda c, carry: (chunk(c, False), carry)[1], 0)
    chunk(c_last, True)
    for p in range(n_pairs):
        out = jnp.concatenate([_online_out(l_ref, acc_ref, p * n_rb + r) for r in range(n_rb)], axis=0)
        o_ref[:, p * LANES:(p + 1) * LANES] = jnp.where(lane_h == 0, out[0:tq], out[tq:2 * tq]).astype(BF)


def _moba_prompt(mq, mobab, et, batch, t, tq):
    nq = t // tq
    assert t // MOBA_BLOCK <= 16 and tq <= MOBA_BLOCK
    tile = lambda w: pl.BlockSpec((tq, w), lambda b, i: (b * nq + i, 0))
    rb = 2 * tq
    return pl.pallas_call(
        functools.partial(_moba_prompt_kernel, tq=tq, t=t, tk=min(512, t), rb=rb),
        grid=(batch, nq),
        in_specs=[tile(384), pl.BlockSpec((t, 768), lambda b, i: (b, 0)), pl.BlockSpec(et.shape, lambda b, i: (0, 0))],
        out_specs=tile(384),
        out_shape=jax.ShapeDtypeStruct((batch * t, 384), BF),
        scratch_shapes=[pltpu.VMEM((LANES, 384), BF), pltpu.VMEM((LANES, tq), F32)]
                       + [pltpu.VMEM((MOBA_HEADS * tq // rb, rb, LANES), F32)] * 3,
        compiler_params=_cparams("parallel", "arbitrary"),
        name="moba_prompt",
    )(mq, mobab, et)


def _pool_mix(win_sums, x, count_pos, pw_ref, sc_ref):
    lane_g = lax.broadcasted_iota(jnp.int32, (1, POOL_CH), 1) >> 6
    s = win_sums[POOL_WINDOWS[-1]]
    w = jnp.full((1, POOL_CH), POOL_WINDOWS[-1], jnp.int32)
    for gi in range(len(POOL_WINDOWS) - 2, -1, -1):
        s = jnp.where(lane_g == gi, win_sums[POOL_WINDOWS[gi]], s)
        w = jnp.where(lane_g == gi, POOL_WINDOWS[gi], w)
    count = jnp.minimum(count_pos, w).astype(F32)
    d = s / count - x
    return _dot(d.astype(BF), pw_ref[...]) * sc_ref[...]


def _pool_prompt_kernel(x_ref, halo_ref, pw_ref, sc_ref, o_ref, buf_ref, *, tp):
    i = pl.program_id(1)
    x = x_ref[...]
    buf_ref[0:16, :] = jnp.where(i > 0, halo_ref[...], 0.0)
    buf_ref[16:16 + tp, :] = x
    acc = x
    sums = {}
    for k in range(1, POOL_WINDOWS[-1]):
        acc = acc + buf_ref[16 - k:16 - k + tp, :]
        if k + 1 in POOL_WINDOWS:
            sums[k + 1] = acc
    pos1 = i * tp + lax.broadcasted_iota(jnp.int32, (tp, 1), 0) + 1
    o_ref[...] = _pool_mix(sums, x, pos1, pw_ref, sc_ref).astype(BF)


def _pool_prompt(xpool, pw, sc, layer, batch, t, tp):
    nt = t // tp
    return pl.pallas_call(
        functools.partial(_pool_prompt_kernel, tp=tp),
        grid=(batch, nt),
        in_specs=[pl.BlockSpec((tp, POOL_CH), lambda b, i: (b * nt + i, 0)),
                  pl.BlockSpec((16, POOL_CH), lambda b, i: (jnp.maximum((b * nt + i) * (tp // 16) - 1, 0), 0)),
                  _const_spec((POOL_CH, POOL_CH), layer), _const_spec((1, POOL_CH), layer)],
        out_specs=pl.BlockSpec((tp, POOL_CH), lambda b, i: (b * nt + i, 0)),
        out_shape=jax.ShapeDtypeStruct((batch * t, POOL_CH), BF),
        scratch_shapes=[pltpu.VMEM((tp + 16, POOL_CH), F32)],
        compiler_params=_cparams("parallel", "parallel"),
        name="pool_prompt",
    )(xpool, xpool, pw, sc)


def _sample_kernel(pt_ref, qc_ref, qr_ref, gate_ref, mq_ref, nsa_new_ref, win_new_ref, moba_new_ref, xpool_ref,
                   winst_ref, poolst_ref,
                   wabk_ref, babk_ref, w2k_ref, wabv_ref, babv_ref, w2v_ref, gkc_ref,
                   ovl_ref, gexp_ref, esel_ref, emoba_ref, pw_ref, sc_ref, *rest, n_pages, page, past):
    cmp_pages = rest[:n_pages]
    sel_pages = rest[n_pages:2 * n_pages]
    moba_pages = rest[2 * n_pages:3 * n_pages]
    onsa_ref, omoba_ref, ypool_ref = rest[3 * n_pages:]
    del pt_ref, page
    row = lax.broadcasted_iota(jnp.int32, (8, 1), 0)
    lane = lax.broadcasted_iota(jnp.int32, (1, LANES), 1)
    lane_h = lane >> 6
    row_h = jnp.where(row >= NSA_GROUP, 1, 0) + jnp.where(row >= 2 * NSA_GROUP, 1, 0)
    row_g = row - row_h * NSA_GROUP
    live6 = row < NSA_HEADS
    n_groups = past // CMP_STRIDE

    def stack_q(ref):
        s = [ref[:, g * LANES:(g + 1) * LANES].astype(F32) for g in range(NSA_GROUP)]
        q = jnp.where(row_g == 0, s[0], jnp.where(row_g == 1, s[1], s[2]))
        return jnp.where(live6 & (row_h == lane_h), q, 0.0)

    def by_group(o):
        return [jnp.sum(jnp.where(live6 & (row == lane_h * NSA_GROUP + g), o, 0.0), axis=0, keepdims=True)
                for g in range(NSA_GROUP)]

    x_kv = jnp.concatenate([pg[...] for pg in cmp_pages], axis=0).astype(BF)
    w_row = CMP_STRIDE * LANES

    def compress(x, wab_ref, bab_ref, w2_ref):
        pab = _dot(x, wab_ref[...]) + bab_ref[...]
        pre = pab[:, 0:LANES] + pltpu.roll(pab[:, LANES:2 * LANES], n_groups - 1, axis=0)
        return _dot(_silu(pre).astype(BF), w2_ref[...])

    kcmp = _head_norm(compress(x_kv[:, 0:w_row], wabk_ref, babk_ref, w2k_ref), gkc_ref[...], _ones_bd())
    vcmp = compress(x_kv[:, w_row:2 * w_row], wabv_ref, babv_ref, w2v_ref)

    qc8 = stack_q(qc_ref)
    c_end = lax.broadcasted_iota(jnp.int32, (1, n_groups), 1) * CMP_STRIDE + CMP_LEN
    p_c = _softmax_rows(_dot3_nt(qc8, kcmp) * SCALE, c_end <= past + 1)
    o_cmp = _dot(p_c.astype(BF), vcmp.astype(BF))
    imp8 = _dot_sel(p_c, ovl_ref[...])
    imp = jnp.sum(jnp.where(live6 & (row_h == lane_h), imp8, 0.0), axis=0, keepdims=True)
    jblk = lane & (SEL_BLOCK - 1)
    own = past // SEL_BLOCK
    sel = _rank_select(jnp.broadcast_to(imp, (8, LANES)), jblk, own, own + 1, SEL_TOPN)[0:1]
    sel8 = jnp.where(live6 & (row_h == lane_h), sel, 0.0)

    qr8 = stack_q(qr_ref)
    qr8b = qr8.astype(BF)
    nsa_new = nsa_new_ref[...]
    picked = _dot(sel8.astype(BF), esel_ref[...]) > 0.5
    k_sel = jnp.concatenate([pg[:, 0:LANES] for pg in sel_pages], axis=0)
    v_sel = jnp.concatenate([pg[:, LANES:2 * LANES] for pg in sel_pages], axis=0)
    s = jnp.where(picked, _dot_nt(qr8b, k_sel), NEG)
    own_picked = jnp.sum(jnp.where(jblk == own, sel8, 0.0), axis=1, keepdims=True) > 0.5
    s_new = jnp.where(own_picked, jnp.sum(qr8 * nsa_new[:, 2 * LANES:3 * LANES], axis=1, keepdims=True), NEG)
    m = jnp.maximum(jnp.maximum(jnp.max(s, axis=1, keepdims=True), s_new), M_INIT)
    e = jnp.exp2(s - m)
    e_new = jnp.exp2(s_new - m)
    den = jnp.sum(e, axis=1, keepdims=True) + e_new
    acc = e_new * nsa_new[:, 3 * LANES:4 * LANES] + _dot(e.astype(BF), v_sel)
    o_sel = jnp.where(den > 0.0, acc / den, 0.0)

    win_new = win_new_ref[...]
    n_win = winst_ref.shape[0]
    kpos_w = past - n_win + lax.broadcasted_iota(jnp.int32, (1, n_win), 1)
    ok_w = (past - kpos_w) < WIN
    s = jnp.where(ok_w, _dot_nt(qr8b, winst_ref[:, 0:LANES].astype(BF)), NEG)
    s_new = jnp.sum(qr8 * win_new[:, 0:LANES], axis=1, keepdims=True)
    m = jnp.maximum(jnp.max(s, axis=1, keepdims=True), s_new)
    e = jnp.exp2(s - m)
    e_new = jnp.exp2(s_new - m)
    den = jnp.sum(e, axis=1, keepdims=True) + e_new
    o_win = (_dot(e.astype(BF), winst_ref[:, LANES:2 * LANES].astype(BF)) + e_new * win_new[:, LANES:2 * LANES]) / den

    gate8 = jnp.broadcast_to(gate_ref[...], (8, LANES))
    gh = gate8.astype(BF)
    gl = (gate8 - gh.astype(F32)).astype(BF)
    gexp = (_dot(gh, gexp_ref[...]) + _dot(gl, gexp_ref[...]))[0:1]
    branches = [by_group(o_cmp), by_group(o_sel), by_group(o_win)]
    for g in range(NSA_GROUP):
        out = jnp.zeros((1, LANES), F32)
        for br in range(3):
            out = out + gexp[:, (g * 3 + br) * LANES:(g * 3 + br + 1) * LANES] * branches[br][g]
        onsa_ref[:, g * LANES:(g + 1) * LANES] = out.astype(BF)

    mq = mq_ref[...].astype(F32)
    lane3 = lax.broadcasted_iota(jnp.int32, (1, 384), 1)
    mq8 = jnp.where(row == (lane3 >> 6), mq, 0.0)
    mq8b = mq8.astype(BF)
    n_blk = past // MOBA_BLOCK
    k_m = jnp.concatenate([pg[:, 0:384] for pg in moba_pages], axis=0)
    v_m = jnp.concatenate([pg[:, 384:768] for pg in moba_pages], axis=0)
    n_blk_pad = -(-n_blk // 16) * 16
    km = _dot(emoba_ref[0:n_blk_pad, :], k_m) * (1.0 / MOBA_BLOCK)
    km = jnp.concatenate([km.astype(BF), jnp.zeros((LANES - n_blk_pad, 384), BF)], axis=0)
    gate_m = _dot_nt(mq8b, km)
    cnt = jnp.zeros((8, LANES), F32)
    for d in range(1, n_blk):
        lower = pltpu.roll(gate_m, d, axis=1)
        upper = pltpu.roll(gate_m, LANES - d, axis=1)
        cnt = cnt + jnp.where((lane >= d) & (lower >= gate_m), 1.0, 0.0)
        cnt = cnt + jnp.where((lane + d < n_blk) & (upper > gate_m), 1.0, 0.0)
    pick_m = jnp.where((lane < n_blk) & (cnt < MOBA_TOPK), 1.0, 0.0)
    ok_m = _dot(pick_m.astype(BF), emoba_ref[...]) > 0.5
    moba_new = moba_new_ref[...]
    s = jnp.where(ok_m, _dot_nt(mq8b, k_m), NEG)
    s_new = jnp.sum(mq8 * moba_new[:, 0:384], axis=1, keepdims=True)
    m = jnp.maximum(jnp.max(s, axis=1, keepdims=True), s_new)
    e = jnp.exp2(s - m)
    e_new = jnp.exp2(s_new - m)
    den = jnp.sum(e, axis=1, keepdims=True) + e_new
    o_m = (e_new * moba_new[:, 384:768] + _dot(e.astype(BF), v_m)) / den
    omoba_ref[...] = jnp.sum(jnp.where(row == (lane3 >> 6), o_m, 0.0), axis=0, keepdims=True).astype(BF)

    x = xpool_ref[...]
    hist = poolst_ref[...]
    hrow = lax.broadcasted_iota(jnp.int32, (POOL_HIST, 1), 0)
    sums = {w: x + jnp.sum(jnp.where(hrow >= POOL_HIST - (w - 1), hist, 0.0), axis=0, keepdims=True)
            for w in POOL_WINDOWS}
    y = _pool_mix(sums, x, jnp.full((1, 1), past + 1, jnp.int32), pw_ref, sc_ref)
    ypool_ref[...] = y.astype(BF)


def _sample(page_table, rows, winst, poolst, caches, cw, consts, pw, sc, layer, past):
    n_seq, n_pages = page_table.shape
    page = caches["sel"].shape[2]
    per_seq = lambda w: pl.BlockSpec((None, 1, w), lambda i, pt: (i, 0, 0))
    lconst = lambda shape: pl.BlockSpec((None,) + tuple(shape), lambda i, pt: (layer,) + (0,) * len(shape),
                                        pipeline_mode=pl.Buffered(1))
    const = lambda a: pl.BlockSpec(a.shape, lambda i, pt: (0,) * a.ndim, pipeline_mode=pl.Buffered(1))
    cmp_specs = [lconst((CMP_STRIDE * LANES, 2 * LANES)), lconst((1, 2 * LANES)), lconst((LANES, LANES))]

    def page_spec(rows_, width, pj):
        return pl.BlockSpec((None, None, rows_, width), lambda i, pt: (layer, pt[i, pj], 0, 0))

    in_specs = ([per_seq(384), per_seq(384), per_seq(LANES), per_seq(384), per_seq(512), per_seq(256), per_seq(768),
                 per_seq(POOL_CH),
                 pl.BlockSpec((None, None, winst.shape[2], 256), lambda i, pt: (layer, i, 0, 0)),
                 pl.BlockSpec((None, None, POOL_HIST, POOL_CH), lambda i, pt: (layer, i, 0, 0))]
                + cmp_specs + cmp_specs + [lconst((1, LANES))]
                + [const(consts["ovl_s"]), const(consts["gexp"]), const(consts["esel"]), const(consts["emoba"]),
                   lconst((POOL_CH, POOL_CH)), lconst((1, POOL_CH))]
                + [page_spec(page // CMP_STRIDE, 2 * CMP_STRIDE * LANES, pj) for pj in range(n_pages)]
                + [page_spec(page, 2 * LANES, pj) for pj in range(n_pages)]
                + [page_spec(page, 768, pj) for pj in range(n_pages)])
    grid_spec = pltpu.PrefetchScalarGridSpec(
        num_scalar_prefetch=1, grid=(n_seq,), in_specs=in_specs,
        out_specs=[per_seq(384), per_seq(384), per_seq(POOL_CH)])
    return pl.pallas_call(
        functools.partial(_sample_kernel, n_pages=n_pages, page=page, past=past),
        grid_spec=grid_spec,
        out_shape=[jax.ShapeDtypeStruct((n_seq, 1, 384), BF), jax.ShapeDtypeStruct((n_seq, 1, 384), BF),
                   jax.ShapeDtypeStruct((n_seq, 1, POOL_CH), BF)],
        compiler_params=_cparams("parallel"),
        name="sample_step",
    )(page_table, rows["qc"], rows["qr"], rows["gate"], rows["mq"], rows["nsa"], rows["win"], rows["moba"],
      rows["xpool"], winst, poolst,
      cw["wabk"], cw["babk"], cw["w2k"], cw["wabv"], cw["babv"], cw["w2v"], cw["gkc"],
      consts["ovl_s"], consts["gexp"], consts["esel"], consts["emoba"], pw, sc,
      *([caches["cmp"]] * n_pages), *([caches["sel"]] * n_pages), *([caches["moba"]] * n_pages))


def _combine_ffn_kernel(x_ref, yp_ref, on_ref, om_ref, w_ref, g_ref, wg_ref, wu_ref, wd_ref, o_ref, *, f_chunk):
    y = _dot(yp_ref[...], w_ref[0:POOL_CH, :]) + _dot(on_ref[...], w_ref[POOL_CH:POOL_CH + 384, :])
    y = y + _dot(om_ref[...], w_ref[POOL_CH + 384:POOL_CH + 768, :])
    o_ref[...] = _ffn_body(x_ref[...] + y, g_ref[...], wg_ref, wu_ref, wd_ref, f_chunk)


def _combine_ffn(x, ypool, onsa, omoba, w_out, g, wg, wu, wd, layer, tm):
    n, d = x.shape
    d_ff = wg.shape[2]
    row = lambda w: pl.BlockSpec((tm, w), lambda i: (i, 0))
    return pl.pallas_call(
        functools.partial(_combine_ffn_kernel, f_chunk=_f_chunk(d_ff)),
        grid=(pl.cdiv(n, tm),),
        in_specs=[row(d), row(POOL_CH), row(384), row(384), _const_spec((POOL_CH + 768, d), layer),
                  _const_spec((1, d), layer),
                  _const_spec((d, d_ff), layer), _const_spec((d, d_ff), layer), _const_spec((d_ff, d), layer)],
        out_specs=row(d),
        out_shape=jax.ShapeDtypeStruct((n, d), F32),
        compiler_params=_cparams("parallel"),
        name="combine_ffn",
    )(x, ypool, onsa, omoba, w_out, g, wg, wu, wd)


def _in_col_perm():
    cols = list(range(0, POOL_CH))
    for g in range(NSA_GROUP):
        for h in range(NSA_KV_HEADS):
            base = POOL_CH + (h * NSA_GROUP + g) * HEAD_DIM
            cols += range(base, base + HEAD_DIM)
    nsa_end = POOL_CH + NSA_HEADS * HEAD_DIM
    kv_end = nsa_end + 6 * NSA_KV_HEADS * HEAD_DIM
    cols += range(nsa_end, kv_end)
    cols += range(kv_end + GATE_COLS, kv_end + GATE_COLS + 3 * MOBA_HEADS * HEAD_DIM)
    gate_cols = list(range(kv_end, kv_end + GATE_COLS))
    return np.array(cols, np.int32), np.array(gate_cols, np.int32)


def _out_row_perm():
    rows = list(range(0, POOL_CH))
    for g in range(NSA_GROUP):
        for h in range(NSA_KV_HEADS):
            base = POOL_CH + (h * NSA_GROUP + g) * HEAD_DIM
            rows += range(base, base + HEAD_DIM)
    rows += range(POOL_CH + NSA_HEADS * HEAD_DIM, POOL_CH + NSA_HEADS * HEAD_DIM + MOBA_HEADS * HEAD_DIM)
    return np.array(rows, np.int32)


def _overlap(n_cmp_rows, n_cmp_valid, n_blocks):
    n = np.arange(n_cmp_rows)
    c_start = n * CMP_STRIDE
    c_end = c_start + CMP_LEN
    jb = np.arange(SEL_BLOCK)
    ov = (c_start[:, None] < (jb[None, :] + 1) * SEL_BLOCK) & (c_end[:, None] > jb[None, :] * SEL_BLOCK)
    ov &= (n[:, None] < n_cmp_valid) & (jb[None, :] < n_blocks)
    return ov.astype(np.float32)


def _constants(t, past):
    g_p = t // CMP_STRIDE
    ov = _overlap(g_p, g_p - 1, t // SEL_BLOCK)
    ovl_p = np.zeros((2 * g_p, LANES), np.float32)
    ovl_p[:g_p, :SEL_BLOCK] = ov
    ovl_p[g_p:, SEL_BLOCK:] = ov
    g_s = past // CMP_STRIDE
    ov_s = _overlap(g_s, g_s - 1, past // SEL_BLOCK + 1)
    ovl_s = np.concatenate([ov_s, ov_s], axis=1)
    lane = np.arange(LANES)
    gexp = np.zeros((LANES, 9 * LANES), np.float32)
    for g in range(NSA_GROUP):
        for br in range(3):
            src = (lane // HEAD_DIM) * 9 + g * 3 + br
            gexp[src, (g * 3 + br) * LANES + lane] = 1.0
    kp = np.arange(past)
    esel = ((lane[:, None] % SEL_BLOCK) == (kp[None, :] // SEL_BLOCK)).astype(np.float32)
    emoba = (lane[:, None] == (kp[None, :] // MOBA_BLOCK)).astype(np.float32)
    kt = np.arange(t)
    et_sel = ((kt[:, None] // SEL_BLOCK) == (lane[None, :] % SEL_BLOCK)).astype(np.float32)
    et_moba = ((kt[:, None] // MOBA_BLOCK) == (lane[None, :] % 16)).astype(np.float32)
    as_bf = lambda a: jnp.asarray(a, BF)
    return {"ovlt_p": as_bf(ovl_p.T), "ovl_s": as_bf(ovl_s), "gexp": as_bf(gexp), "esel": as_bf(esel),
            "emoba": as_bf(emoba), "et_sel": as_bf(et_sel), "et_moba": as_bf(et_moba)}


def _rope_tables(pos):
    half = ROPE_DIM // 2
    inv_freq = ROPE_THETA ** (-jnp.arange(half, dtype=F32) / half)
    ang = pos.astype(F32)[:, None] * inv_freq[None, :]
    cos, sin, zero = jnp.cos(ang), jnp.sin(ang), jnp.zeros_like(ang)
    rest = HEAD_DIM - ROPE_DIM
    c = jnp.concatenate([cos, cos, jnp.ones((ang.shape[0], rest), F32)], axis=1)
    a = jnp.concatenate([-sin, zero, jnp.zeros((ang.shape[0], rest), F32)], axis=1)
    b = jnp.concatenate([zero, sin, jnp.zeros((ang.shape[0], rest), F32)], axis=1)
    tile2 = lambda m: jnp.concatenate([m, m], axis=1)
    return tile2(c), tile2(a), tile2(b)


def _cmp_weights(cmp_pos, cmp_w1, cmp_w2, g_nsa_kc):
    bd = lambda w: jnp.concatenate([jnp.concatenate([w, jnp.zeros_like(w)], axis=-1),
                                    jnp.concatenate([jnp.zeros_like(w), w], axis=-1)], axis=-2)
    out = {}
    depth = cmp_pos.shape[0]
    for i, name in enumerate(("k", "v")):
        pos2 = jnp.concatenate([cmp_pos[:, i], cmp_pos[:, i]], axis=-1)
        w1 = bd(cmp_w1[:, i])
        out["pos" + name] = pos2
        out["w1" + name] = w1.astype(BF)
        out["w2" + name] = bd(cmp_w2[:, i]).astype(BF)
        halves = [w1[:, s:s + CMP_STRIDE].reshape(depth, CMP_STRIDE * LANES, LANES) for s in (0, CMP_STRIDE)]
        out["wab" + name] = jnp.concatenate(halves, axis=-1).astype(BF)
        bias = [jnp.einsum("xlc,xlce->xe", pos2[:, s:s + CMP_STRIDE], w1[:, s:s + CMP_STRIDE],
                           precision=lax.Precision.HIGHEST) for s in (0, CMP_STRIDE)]
        out["bab" + name] = jnp.concatenate(bias, axis=-1)[:, None, :]
    out["gkc"] = jnp.concatenate([g_nsa_kc, g_nsa_kc], axis=-1)[:, None, :]
    return out


def kernel(x_prompt, x_sample, cache_nsa_kv, cache_moba_kv, state_nsa_win, state_pool, page_table, g_ffa, w_ffa_gate, w_ffa_up, w_ffa_down, g_mix, w_in, w_out, pool_w, pool_scale, g_nsa_q, g_nsa_kc, g_nsa_ks, g_nsa_kw, cmp_pos, cmp_w1, cmp_w2, g_moba_q, g_moba_k, g_ffb, w_ffb_gate, w_ffb_up, w_ffb_down):
    batch, t, d = x_prompt.shape
    n_seq = x_sample.shape[0]
    depth, n_pool, page = cache_nsa_kv.shape[:3]
    n_pages = page_table.shape[1]
    past = n_pages * page
    n_p = batch * t
    n_win = state_nsa_win.shape[2]
    assert x_sample.shape[1] == 1 and t % MOBA_BLOCK == 0 and past % MOBA_BLOCK == 0 and n_win == WIN

    tm = next((c for c in (512, 384, 256) if (n_p + n_seq) % c == 0), 512)
    tp = next(c for c in (512, 256) if t % c == 0)
    tq_nsa = 128
    tq_moba = MOBA_BLOCK

    col_perm, gate_cols = _in_col_perm()
    w_in_p = jnp.concatenate([w_in[:, :, col_perm], w_in[:, :, gate_cols],
                              jnp.zeros((depth, d, LANES - GATE_COLS), w_in.dtype)], axis=-1).astype(BF)
    w_out_p = w_out[:, _out_row_perm(), :].astype(BF)
    bf = lambda w: w.astype(BF)
    wa = (bf(w_ffa_gate), bf(w_ffa_up), bf(w_ffa_down))
    wb = (bf(w_ffb_gate), bf(w_ffb_up), bf(w_ffb_down))
    tile2 = lambda g: jnp.concatenate([g, g], axis=-1)
    gains = jnp.stack([tile2(g_nsa_q), tile2(g_nsa_ks), tile2(g_nsa_kw), tile2(g_moba_q), tile2(g_moba_k)]
                      + [jnp.ones((depth, LANES), F32)] * 3, axis=1)
    cw = _cmp_weights(cmp_pos, cmp_w1, cmp_w2, g_nsa_kc)
    eye = jnp.eye(len(POOL_WINDOWS), dtype=pool_w.dtype)
    pw_bd = jnp.einsum("lgij,gh->lgihj", pool_w, eye).reshape(depth, POOL_CH, POOL_CH).astype(BF)
    sc = pool_scale[:, None, :]
    consts = _constants(t, past)
    pos = jnp.concatenate([jnp.tile(jnp.arange(t, dtype=jnp.int32), batch), jnp.full((n_seq,), past, jnp.int32)])
    rope_c, rope_a, rope_b = _rope_tables(pos)

    im2col = lambda a: a.reshape(depth, n_pool, page // CMP_STRIDE, CMP_STRIDE * LANES)
    caches = {"cmp": jnp.concatenate([im2col(cache_nsa_kv[:, :, :, 0]), im2col(cache_nsa_kv[:, :, :, 1])], axis=-1),
              "sel": cache_nsa_kv[:, :, :, 2:4].astype(BF).reshape(depth, n_pool, page, 256),
              "moba": cache_moba_kv.astype(BF).reshape(depth, n_pool, page, 768)}
    winst = state_nsa_win.reshape(depth, n_seq, n_win, 256)

    x = jnp.concatenate([x_prompt.reshape(n_p, d), x_sample.reshape(n_seq, d)], axis=0)
    outs = [[] for _ in range(8)]
    for l in range(depth):
        x = _ffn(x, g_ffa[:, None, :], *wa, l, tm)
        (xpool, qc, qr, nsa, nsab, win, winb, mq, moba, mobab, gate) = _prep(
            x, g_mix[:, None, :], w_in_p, gains, rope_c, rope_a, rope_b, l, tm)

        kcmp, vcmp = _compress_prompt(nsa, cw, l, batch, t)
        onsa_p = _nsa_prompt(qc, qr, gate, kcmp, vcmp, nsab, winb, consts["ovlt_p"], consts["gexp"], consts["et_sel"],
                             batch, t, tq_nsa)
        omoba_p = _moba_prompt(mq, mobab, consts["et_moba"], batch, t, tq_moba)
        ypool_p = _pool_prompt(xpool, pw_bd, sc, l, batch, t, tp)

        srow = lambda a: a[n_p:].reshape(n_seq, 1, a.shape[1])
        rows = {"qc": srow(qc), "qr": srow(qr), "gate": srow(gate), "mq": srow(mq), "nsa": srow(nsa),
                "win": srow(win), "moba": srow(moba), "xpool": srow(xpool)}
        onsa_s, omoba_s, ypool_s = _sample(page_table, rows, winst, state_pool, caches, cw, consts, pw_bd, sc, l, past)

        cat = lambda p, s: jnp.concatenate([p, s.reshape(n_seq, s.shape[2])], axis=0)
        x = _combine_ffn(x, cat(ypool_p, ypool_s), cat(onsa_p, onsa_s), cat(omoba_p, omoba_s), w_out_p,
                         g_ffb[:, None, :], *wb, l, tm)

        outs[0].append(nsa[:n_p].reshape(batch, t, 4, NSA_KV_HEADS, HEAD_DIM))
        outs[1].append(nsa[n_p:].reshape(n_seq, 1, 4, NSA_KV_HEADS, HEAD_DIM))
        outs[2].append(moba[:n_p].reshape(batch, t, 2, MOBA_HEADS, HEAD_DIM))
        outs[3].append(moba[n_p:].reshape(n_seq, 1, 2, MOBA_HEADS, HEAD_DIM))
        outs[4].append(win[:n_p].reshape(batch, t, 2, NSA_KV_HEADS, HEAD_DIM)[:, t - min(WIN, t):])
        outs[5].append(win[n_p:].reshape(n_seq, 1, 2, NSA_KV_HEADS, HEAD_DIM))
        outs[6].append(xpool[:n_p].reshape(batch, t, POOL_CH)[:, t - POOL_HIST:])
        outs[7].append(xpool[n_p:].reshape(n_seq, 1, POOL_CH))

    stacked = [jnp.stack(o, axis=0) for o in outs]
    stacked[5] = jnp.concatenate([state_nsa_win[:, :, 1:], stacked[5]], axis=2)
    stacked[7] = jnp.concatenate([state_pool[:, :, 1:], stacked[7]], axis=2)
    return (x[:n_p].reshape(batch, t, d), x[n_p:].reshape(n_seq, 1, d), *stacked)
```

```python
import functools

import numpy as np
import jax
import jax.numpy as jnp
from jax import lax
from jax.experimental import pallas as pl
from jax.experimental.pallas import tpu as pltpu

F32 = jnp.float32
BF = jnp.bfloat16

HEAD_DIM = 64
LANES = 128
POOL_CH = 256
POOL_WINDOWS = (2, 4, 8, 16)
POOL_HIST = 15
NSA_KV_HEADS = 2
NSA_GROUP = 3
NSA_HEADS = 6
MOBA_HEADS = 6
CMP_LEN = 32
CMP_STRIDE = 16
SEL_BLOCK = 64
SEL_TOPN = 16
WIN = 512
MOBA_BLOCK = 256
MOBA_TOPK = 3
ROPE_THETA = 500000.0
ROPE_DIM = 16
EPS = 1e-6
NEG = -1e30
M_INIT = -1e29
SEL_FORCE = 1e4
SCALE = HEAD_DIM ** -0.5
Q_PRESCALE = SCALE * 1.4426950408889634
GATE_COLS = 3 * NSA_HEADS
D_QKV = 2560
D_IN_PAD = D_QKV + LANES
VMEM_LIMIT = 56 * 1024 * 1024


def _cparams(*sem):
    return pltpu.CompilerParams(dimension_semantics=sem, vmem_limit_bytes=VMEM_LIMIT)


def _dot(a, b):
    return jnp.dot(a, b, preferred_element_type=F32)


def _dot_nt(a, b):
    return lax.dot_general(a, b, (((1,), (1,)), ((), ())), preferred_element_type=F32)


def _split3(x):
    hi = x.astype(BF)
    r = x - hi.astype(F32)
    mid = r.astype(BF)
    lo = (r - mid.astype(F32)).astype(BF)
    return hi, mid, lo


def _dot_sel(x, m):
    hi, mid, lo = _split3(x)
    return _dot(hi, m) + _dot(mid, m) + _dot(lo, m)


def _dot3_nt(a, b):
    ah = a.astype(BF)
    al = (a - ah.astype(F32)).astype(BF)
    bh = b.astype(BF)
    bl = (b - bh.astype(F32)).astype(BF)
    return _dot_nt(ah, bh) + _dot_nt(ah, bl) + _dot_nt(al, bh)


def _sigmoid(x):
    return 1.0 / (1.0 + jnp.exp(-x))


def _silu(x):
    return x * _sigmoid(x)


def _rms_rows(x, g):
    return x * lax.rsqrt(jnp.mean(x * x, axis=-1, keepdims=True) + EPS) * g


def _ones_bd():
    r = lax.broadcasted_iota(jnp.int32, (LANES, LANES), 0) >> 6
    c = lax.broadcasted_iota(jnp.int32, (LANES, LANES), 1) >> 6
    return jnp.where(r == c, 1.0, 0.0).astype(BF)


def _head_norm(x, gain, ones_bd):
    ms = _dot_sel(x * x, ones_bd) * (1.0 / HEAD_DIM)
    return x * lax.rsqrt(ms + EPS) * gain


def _rope(x, c, a, b):
    return x * c + pltpu.roll(x, LANES - ROPE_DIM // 2, axis=1) * a + pltpu.roll(x, ROPE_DIM // 2, axis=1) * b


def _softmax_rows(s, valid):
    s = jnp.where(valid, s, NEG)
    m = jnp.max(s, axis=1, keepdims=True)
    e = jnp.where(valid, jnp.exp(s - m), 0.0)
    den = jnp.sum(e, axis=1, keepdims=True)
    return jnp.where(den > 0.0, e / den, 0.0)


def _sum_list(xs):
    return functools.reduce(lambda a, b: a + b, xs)


def _ffn_body(x, g, wg_ref, wu_ref, wd_ref, f_chunk):
    xn = _rms_rows(x, g).astype(BF)
    d_ff = wg_ref.shape[1]
    acc = jnp.zeros(x.shape, F32)
    for c in range(d_ff // f_chunk):
        sl = slice(c * f_chunk, (c + 1) * f_chunk)
        gate = _dot(xn, wg_ref[:, sl])
        up = _dot(xn, wu_ref[:, sl])
        h = (_silu(gate) * up).astype(BF)
        acc = acc + _dot(h, wd_ref[sl, :])
    return x + 0.5 * acc


def _ffn_kernel(x_ref, g_ref, wg_ref, wu_ref, wd_ref, o_ref, *, f_chunk):
    o_ref[...] = _ffn_body(x_ref[...], g_ref[...], wg_ref, wu_ref, wd_ref, f_chunk)


def _f_chunk(d_ff):
    for c in (1408, 1024, 512, 256, 128):
        if d_ff % c == 0:
            return c
    return d_ff


def _row_tile(n):
    return next((c for c in (512, 384, 256, 128) if n % c == 0), n)


def _const_spec(shape, layer):
    nd = len(shape)
    return pl.BlockSpec((None,) + tuple(shape), lambda *_: (layer,) + (0,) * nd, pipeline_mode=pl.Buffered(1))


def _ffn(x, g, wg, wu, wd, layer):
    n, d = x.shape
    d_ff = wg.shape[2]
    tm = _row_tile(n)
    return pl.pallas_call(
        functools.partial(_ffn_kernel, f_chunk=_f_chunk(d_ff)),
        grid=(n // tm,),
        in_specs=[pl.BlockSpec((tm, d), lambda i: (i, 0)),
                  _const_spec((1, d), layer),
                  _const_spec((d, d_ff), layer), _const_spec((d, d_ff), layer), _const_spec((d_ff, d), layer)],
        out_specs=pl.BlockSpec((tm, d), lambda i: (i, 0)),
        out_shape=jax.ShapeDtypeStruct((n, d), F32),
        compiler_params=_cparams("parallel"),
        name="ffn",
    )(x, g, wg, wu, wd)


def _prep_kernel(x_ref, g_ref, w_ref, gains_ref, c_ref, a_ref, b_ref,
                 xpool_ref, qc_ref, qr_ref, nsa_ref, nsab_ref, win_ref, winb_ref,
                 mq_ref, moba_ref, mobab_ref, gate_ref):
    xn = _rms_rows(x_ref[...], g_ref[...]).astype(BF)
    z = _dot(xn, w_ref[...])
    ones_bd = _ones_bd()
    c, a, b = c_ref[...], a_ref[...], b_ref[...]
    gains = gains_ref[...]
    slab = lambda i: z[:, i * LANES:(i + 1) * LANES]
    norm_rope = lambda s, gi: _rope(_head_norm(s, gains[gi:gi + 1], ones_bd), c, a, b)

    xpool_ref[...] = z[:, 0:POOL_CH]
    for gi in range(3):
        qn = _head_norm(slab(2 + gi), gains[0:1], ones_bd)
        qc_ref[:, gi * LANES:(gi + 1) * LANES] = qn
        qr_ref[:, gi * LANES:(gi + 1) * LANES] = (_rope(qn, c, a, b) * Q_PRESCALE).astype(BF)
    nsa = [slab(5), slab(6), norm_rope(slab(7), 1), slab(8)]
    for i, s in enumerate(nsa):
        nsa_ref[:, i * LANES:(i + 1) * LANES] = s
        nsab_ref[:, i * LANES:(i + 1) * LANES] = s.astype(BF)
    win = [norm_rope(slab(9), 2), slab(10)]
    for i, s in enumerate(win):
        win_ref[:, i * LANES:(i + 1) * LANES] = s
        winb_ref[:, i * LANES:(i + 1) * LANES] = s.astype(BF)
    for i in range(3):
        mq_ref[:, i * LANES:(i + 1) * LANES] = (norm_rope(slab(11 + i), 3) * Q_PRESCALE).astype(BF)
    for i in range(6):
        s = norm_rope(slab(14 + i), 4) if i < 3 else slab(14 + i)
        moba_ref[:, i * LANES:(i + 1) * LANES] = s
        mobab_ref[:, i * LANES:(i + 1) * LANES] = s.astype(BF)
    gate_ref[...] = _sigmoid(slab(20))


def _prep(x, g, w_in, gains, rope, layer):
    n, d = x.shape
    tm = _row_tile(min(n, rope[0].shape[0]))
    nb = rope[0].shape[0] // tm
    row = lambda w: pl.BlockSpec((tm, w), lambda i: (i, 0))
    tab = pl.BlockSpec((tm, LANES), lambda i: (i % nb, 0))
    widths_dtypes = [(POOL_CH, F32), (384, F32), (384, BF), (512, F32), (512, BF), (256, F32), (256, BF),
                     (384, BF), (768, F32), (768, BF), (LANES, F32)]
    return pl.pallas_call(
        _prep_kernel,
        grid=(n // tm,),
        in_specs=[row(d), _const_spec((1, d), layer), _const_spec((d, D_IN_PAD), layer),
                  _const_spec((8, LANES), layer), tab, tab, tab],
        out_specs=[row(w) for w, _ in widths_dtypes],
        out_shape=[jax.ShapeDtypeStruct((n, w), dt) for w, dt in widths_dtypes],
        compiler_params=_cparams("parallel"),
        name="prep",
    )(x, g, w_in, gains, *rope)


def _compress_tokens(load_rows, n_groups, wab_ref, bab_ref, w2_ref):
    x = jnp.concatenate([load_rows(l) for l in range(CMP_STRIDE)], axis=1).astype(BF)
    pab = _dot(x, wab_ref[...]) + bab_ref[...]
    pre = pab[:, 0:LANES] + pltpu.roll(pab[:, LANES:2 * LANES], n_groups - 1, axis=0)
    return _dot(_silu(pre).astype(BF), w2_ref[...])


def _compress_prompt_kernel(kc_ref, vc_ref, wabk_ref, babk_ref, w2k_ref, wabv_ref, babv_ref, w2v_ref, gkc_ref,
                            kcmp_ref, vcmp_ref, *, n_groups):
    kraw = _compress_tokens(lambda l: kc_ref[pl.ds(l, n_groups, stride=CMP_STRIDE), :],
                            n_groups, wabk_ref, babk_ref, w2k_ref)
    kcmp_ref[...] = _head_norm(kraw, gkc_ref[...], _ones_bd())
    vcmp_ref[...] = _compress_tokens(lambda l: vc_ref[pl.ds(l, n_groups, stride=CMP_STRIDE), :],
                                     n_groups, wabv_ref, babv_ref, w2v_ref)


def _cmp_weight_shapes():
    return [(CMP_STRIDE * LANES, 2 * LANES), (1, 2 * LANES), (LANES, LANES)]


def _compress_prompt(nsa_rows, cw, layer, batch, t):
    n_groups = t // CMP_STRIDE
    wspecs = [_const_spec(s, layer) for s in _cmp_weight_shapes()]
    return pl.pallas_call(
        functools.partial(_compress_prompt_kernel, n_groups=n_groups),
        grid=(batch,),
        in_specs=[pl.BlockSpec((t, LANES), lambda b: (b, 0)), pl.BlockSpec((t, LANES), lambda b: (b, 1))]
                 + wspecs + wspecs + [_const_spec((1, LANES), layer)],
        out_specs=[pl.BlockSpec((n_groups, LANES), lambda b: (b, 0))] * 2,
        out_shape=[jax.ShapeDtypeStruct((batch * n_groups, LANES), F32)] * 2,
        compiler_params=_cparams("parallel"),
        name="compress_prompt",
    )(nsa_rows, nsa_rows, cw["wabk"], cw["babk"], cw["w2k"], cw["wabv"], cw["babv"], cw["w2v"], cw["gkc"])


def _rank_select(imp, j, own, n_blocks, top_n):
    forced = (j == 0) | (j == own) | (j == own - 1)
    imp = jnp.where(forced, imp + SEL_FORCE, imp)
    imp = jnp.where(j <= own, imp, -jnp.inf)
    cnt = jnp.zeros(imp.shape, F32)
    for d in range(1, SEL_BLOCK):
        if d > n_blocks - 1 and d < SEL_BLOCK - (n_blocks - 1):
            continue
        nowrap = j >= d
        r = jnp.where(nowrap, pltpu.roll(imp, d, axis=1), pltpu.roll(imp, (d - SEL_BLOCK) % LANES, axis=1))
        cnt = cnt + jnp.where(nowrap, jnp.where(r >= imp, 1.0, 0.0), jnp.where(r > imp, 1.0, 0.0))
    return jnp.where(j <= own, jnp.where(cnt < top_n, 1.0, 0.0), 0.0)


def _online_update(s, v, m_ref, l_ref, acc_ref, idx):
    m_prev = m_ref[idx]
    m_new = jnp.maximum(m_prev, jnp.max(s, axis=1, keepdims=True))
    alpha = jnp.exp2(m_prev - m_new)
    ps = [jnp.exp2(s[:, i * LANES:(i + 1) * LANES] - m_new) for i in range(s.shape[1] // LANES)]
    l_ref[idx] = alpha * l_ref[idx] + _sum_list(ps)
    acc_ref[idx] = alpha * acc_ref[idx] + _dot(jnp.concatenate(ps, axis=1).astype(BF), v)
    m_ref[idx] = m_new


def _online_init(m_ref, l_ref, acc_ref):
    m_ref[...] = jnp.full(m_ref.shape, M_INIT, F32)
    l_ref[...] = jnp.zeros(l_ref.shape, F32)
    acc_ref[...] = jnp.zeros(acc_ref.shape, F32)


def _online_out(l_ref, acc_ref, idx):
    l = jnp.sum(l_ref[idx], axis=1, keepdims=True)
    return jnp.where(l > 0.0, acc_ref[idx] / l, 0.0)


def _nsa_prompt_kernel(qc_ref, qr_ref, gate_ref, kcmp_ref, vcmp_ref, nsab_ref, winb_ref, ovlt_ref, gexp_ref, et_ref,
                       o_ref, impt_ref, m_ref, l_ref, acc_ref, *, tq, tk):
    qi = pl.program_id(1)
    q0 = qi * tq
    lane_h = lax.broadcasted_iota(jnp.int32, (1, LANES), 1) >> 6
    qpos = q0 + lax.broadcasted_iota(jnp.int32, (tq, 1), 0)
    qpos6 = jnp.concatenate([qpos] * NSA_HEADS, axis=0)

    def stack6(tile, zero):
        return jnp.concatenate([jnp.where(lane_h == h, tile(g), zero)
                                for h in range(NSA_KV_HEADS) for g in range(NSA_GROUP)], axis=0)

    kc = kcmp_ref[...]
    n_cmp = kc.shape[0]
    c_end = lax.broadcasted_iota(jnp.int32, (1, n_cmp), 1) * CMP_STRIDE + CMP_LEN
    q_cmp = stack6(lambda g: qc_ref[:, g * LANES:(g + 1) * LANES], 0.0)
    p_c = _softmax_rows(_dot3_nt(q_cmp, kc) * SCALE, c_end <= qpos6 + 1)
    o_cmp = _dot(p_c.astype(BF), vcmp_ref[...].astype(BF))
    psum = [_sum_list([p_c[(h * NSA_GROUP + g) * tq:(h * NSA_GROUP + g + 1) * tq] for g in range(NSA_GROUP)])
            for h in range(NSA_KV_HEADS)]
    hi, mid, lo = _split3(jnp.concatenate(psum, axis=1))
    ovlt = ovlt_ref[...]
    impt = _dot_nt(ovlt, hi) + _dot_nt(ovlt, mid) + _dot_nt(ovlt, lo)
    brow = lax.broadcasted_iota(jnp.int32, (LANES, 1), 0)
    jrow = brow & (SEL_BLOCK - 1)
    own_l = (q0 + lax.broadcasted_iota(jnp.int32, (1, tq), 1)) >> 6
    forced = (jrow == 0) | (jrow == own_l) | (jrow == own_l - 1)
    impt = jnp.where(forced, impt + SEL_FORCE, impt)
    impt = jnp.where(jrow <= own_l, impt, -jnp.inf)
    impt_ref[...] = impt

    def rank_step(jp, cnt):
        comp = jnp.where(brow < SEL_BLOCK, impt_ref[pl.ds(jp, 1), :], impt_ref[pl.ds(SEL_BLOCK + jp, 1), :])
        return cnt + jnp.where(jrow > jp, jnp.where(comp >= impt, 1.0, 0.0), jnp.where(comp > impt, 1.0, 0.0))

    n_live = (q0 + tq - 1) // SEL_BLOCK + 1
    cnt = lax.fori_loop(0, n_live, rank_step, jnp.zeros((LANES, tq), F32))
    sel_t = jnp.where(jrow <= own_l, jnp.where(cnt < SEL_TOPN, 1.0, 0.0), 0.0)
    sel_neg = jnp.where(sel_t.T > 0.5, 0.0, NEG)

    zero = jnp.zeros((), BF)
    q_rot = stack6(lambda g: qr_ref[:, g * LANES:(g + 1) * LANES], zero)
    q_aug = jnp.concatenate([q_rot, stack6(lambda g: sel_neg, 0.0).astype(BF)], axis=1)
    _online_init(m_ref, l_ref, acc_ref)

    def sel_chunk(c, causal):
        start = pl.multiple_of(c * tk, tk)
        k_aug = jnp.concatenate([nsab_ref[pl.ds(start, tk), 2 * LANES:3 * LANES], et_ref[pl.ds(start, tk), :]], axis=1)
        v = nsab_ref[pl.ds(start, tk), 3 * LANES:4 * LANES]
        s = _dot_nt(q_aug, k_aug)
        if causal:
            kpos = start + lax.broadcasted_iota(jnp.int32, (1, tk), 1)
            s = jnp.where(kpos <= qpos6, s, NEG)
        _online_update(s, v, m_ref, l_ref, acc_ref, 0)

    c_last = (q0 + tq - 1) // tk
    lax.fori_loop(0, c_last, lambda c, carry: (sel_chunk(c, False), carry)[1], 0)
    sel_chunk(c_last, True)
    o_sel = _online_out(l_ref, acc_ref, 0)

    n_slots = WIN // tq + 1
    slot_c = [qi - (n_slots - 1) + slot for slot in range(n_slots)]
    starts = [pl.multiple_of(jnp.maximum(c, 0) * tq, tq) for c in slot_c]
    k_band = jnp.concatenate([winb_ref[pl.ds(st, tq), 0:LANES] for st in starts], axis=0)
    v_band = jnp.concatenate([winb_ref[pl.ds(st, tq), LANES:2 * LANES] for st in starts], axis=0)
    s_band = _dot_nt(q_rot, k_band)
    s_w = []
    for slot in range(n_slots):
        s = s_band[:, slot * tq:(slot + 1) * tq]
        kpos = starts[slot] + lax.broadcasted_iota(jnp.int32, (1, tq), 1)
        if slot == 0:
            s = jnp.where(kpos > qpos6 - WIN, s, NEG)
        if slot == n_slots - 1:
            s = jnp.where(kpos <= qpos6, s, NEG)
        else:
            s = jnp.where(slot_c[slot] >= 0, s, NEG)
        s_w.append(s)
    m = jnp.max(functools.reduce(jnp.maximum, s_w), axis=1, keepdims=True)
    ps = [jnp.exp2(s - m) for s in s_w]
    l = jnp.sum(_sum_list(ps), axis=1, keepdims=True)
    o_win = _dot(jnp.concatenate(ps, axis=1).astype(BF), v_band) / l

    gate = gate_ref[...]
    gh = gate.astype(BF)
    gl = (gate - gh.astype(F32)).astype(BF)
    gexp = _dot(gh, gexp_ref[...]) + _dot(gl, gexp_ref[...])
    for g in range(NSA_GROUP):
        r0, r1 = g * tq, (NSA_GROUP + g) * tq
        out = jnp.zeros((tq, LANES), F32)
        for br, o in enumerate((o_cmp, o_sel, o_win)):
            both = jnp.where(lane_h == 0, o[r0:r0 + tq], o[r1:r1 + tq])
            out = out + gexp[:, (g * 3 + br) * LANES:(g * 3 + br + 1) * LANES] * both
        o_ref[:, g * LANES:(g + 1) * LANES] = out.astype(BF)


def _nsa_prompt(qc, qr, gate, kcmp, vcmp, nsab, winb, ovlt, gexp, et, batch, t, tq):
    nq = t // tq
    n_groups = t // CMP_STRIDE
    assert WIN % tq == 0 and t // SEL_BLOCK <= SEL_BLOCK
    tile = lambda w: pl.BlockSpec((tq, w), lambda b, i: (b * nq + i, 0))
    per_b = lambda r, w: pl.BlockSpec((r, w), lambda b, i: (b, 0))
    const = lambda a: pl.BlockSpec(a.shape, lambda b, i: (0, 0), pipeline_mode=pl.Buffered(1))
    return pl.pallas_call(
        functools.partial(_nsa_prompt_kernel, tq=tq, tk=min(512, t)),
        grid=(batch, nq),
        in_specs=[tile(384), tile(384), tile(LANES), per_b(n_groups, LANES), per_b(n_groups, LANES),
                  per_b(t, 512), per_b(t, 256), const(ovlt), const(gexp), const(et)],
        out_specs=tile(384),
        out_shape=jax.ShapeDtypeStruct((batch * t, 384), BF),
        scratch_shapes=[pltpu.VMEM((LANES, tq), F32)] + [pltpu.VMEM((1, NSA_HEADS * tq, LANES), F32)] * 3,
        compiler_params=_cparams("parallel", "parallel"),
        name="nsa_prompt",
    )(qc, qr, gate, kcmp, vcmp, nsab, winb, ovlt, gexp, et)


def _moba_prompt_kernel(mq_ref, mobab_ref, et_ref, o_ref, kmbd_ref, gate_ref, m_ref, l_ref, acc_ref, *, tq, t, tk):
    qi = pl.program_id(1)
    q0 = qi * tq
    n_pairs = MOBA_HEADS // 2
    lane = lax.broadcasted_iota(jnp.int32, (1, LANES), 1)
    lane_h = lane >> 6
    hh = lane >> 4
    qpos = q0 + lax.broadcasted_iota(jnp.int32, (tq, 1), 0)
    qpos2 = jnp.concatenate([qpos, qpos], axis=0)
    brow = lax.broadcasted_iota(jnp.int32, (LANES, 1), 0)
    jrow = brow & 15

    @pl.when(qi == 0)
    def _():
        tt = lax.broadcasted_iota(jnp.int32, (1, t), 1) >> 8
        ind = jnp.where(jrow == tt, 1.0, 0.0).astype(BF)
        km = _dot(ind, mobab_ref[:, 0:384]) * (1.0 / MOBA_BLOCK)
        col_h = lax.broadcasted_iota(jnp.int32, (1, 384), 1) >> 6
        kmbd_ref[...] = jnp.where(col_h == (brow >> 4), km, 0.0).astype(BF)

    gate = _dot_nt(kmbd_ref[...], mq_ref[...])
    gate_ref[...] = gate
    own_l = (q0 + lax.broadcasted_iota(jnp.int32, (1, tq), 1)) >> 8

    def rank_step(jp, cnt):
        comp = jnp.concatenate([jnp.broadcast_to(gate_ref[pl.ds(hd * 16 + jp, 1), :], (16, tq))
                                for hd in range(LANES // 16)], axis=0)
        beats = jnp.where(jrow > jp, jnp.where(comp >= gate, 1.0, 0.0), jnp.where(comp > gate, 1.0, 0.0))
        return cnt + jnp.where(jp < own_l, beats, 0.0)

    n_past = (q0 + tq - 1) // MOBA_BLOCK
    cnt = lax.fori_loop(0, n_past, rank_step, jnp.zeros((LANES, tq), F32))
    picked_t = jnp.where(jrow == own_l, 1.0, jnp.where(jrow < own_l, jnp.where(cnt < MOBA_TOPK, 1.0, 0.0), 0.0))
    pick_neg = jnp.where(picked_t.T > 0.5, 0.0, NEG)

    zero = jnp.zeros((), BF)
    q_aug = []
    for p in range(n_pairs):
        qp = mq_ref[:, p * LANES:(p + 1) * LANES]
        q2 = jnp.concatenate([jnp.where(lane_h == 0, qp, zero), jnp.where(lane_h == 1, qp, zero)], axis=0)
        pk = jnp.concatenate([jnp.where(hh == 2 * p, pick_neg, 0.0), jnp.where(hh == 2 * p + 1, pick_neg, 0.0)], axis=0)
        q_aug.append(jnp.concatenate([q2, pk.astype(BF)], axis=1))
    _online_init(m_ref, l_ref, acc_ref)

    def chunk(c, causal):
        start = pl.multiple_of(c * tk, tk)
        et = et_ref[pl.ds(start, tk), :]
        for p in range(n_pairs):
            k_aug = jnp.concatenate([mobab_ref[pl.ds(start, tk), p * LANES:(p + 1) * LANES], et], axis=1)
            v = mobab_ref[pl.ds(start, tk), 384 + p * LANES:384 + (p + 1) * LANES]
            s = _dot_nt(q_aug[p], k_aug)
            if causal:
                kpos = start + lax.broadcasted_iota(jnp.int32, (1, tk), 1)
                s = jnp.where(kpos <= qpos2, s, NEG)
            _online_update(s, v, m_ref, l_ref, acc_ref, p)

    c_last = (q0 + tq - 1) // tk
    lax.fori_loop(0, c_last, lambda c, carry: (chunk(c, False), carry)[1], 0)
    chunk(c_last, True)
    for p in range(n_pairs):
        out = _online_out(l_ref, acc_ref, p)
        o_ref[:, p * LANES:(p + 1) * LANES] = jnp.where(lane_h == 0, out[0:tq], out[tq:2 * tq]).astype(BF)


def _moba_prompt(mq, mobab, et, batch, t, tq):
    nq = t // tq
    assert t // MOBA_BLOCK <= 16 and tq <= MOBA_BLOCK
    tile = lambda w: pl.BlockSpec((tq, w), lambda b, i: (b * nq + i, 0))
    return pl.pallas_call(
        functools.partial(_moba_prompt_kernel, tq=tq, t=t, tk=min(512, t)),
        grid=(batch, nq),
        in_specs=[tile(384), pl.BlockSpec((t, 768), lambda b, i: (b, 0)),
                  pl.BlockSpec(et.shape, lambda b, i: (0, 0), pipeline_mode=pl.Buffered(1))],
        out_specs=tile(384),
        out_shape=jax.ShapeDtypeStruct((batch * t, 384), BF),
        scratch_shapes=[pltpu.VMEM((LANES, 384), BF), pltpu.VMEM((LANES, tq), F32)]
                       + [pltpu.VMEM((MOBA_HEADS // 2, 2 * tq, LANES), F32)] * 3,
        compiler_params=_cparams("parallel", "arbitrary"),
        name="moba_prompt",
    )(mq, mobab, et)


def _pool_mix(win_sums, x, count_pos, pw_ref, sc_ref):
    lane_g = lax.broadcasted_iota(jnp.int32, (1, POOL_CH), 1) >> 6
    s = win_sums[POOL_WINDOWS[-1]]
    w = jnp.full((1, POOL_CH), POOL_WINDOWS[-1], jnp.int32)
    for gi in range(len(POOL_WINDOWS) - 2, -1, -1):
        s = jnp.where(lane_g == gi, win_sums[POOL_WINDOWS[gi]], s)
        w = jnp.where(lane_g == gi, POOL_WINDOWS[gi], w)
    count = jnp.minimum(count_pos, w).astype(F32)
    d = s / count - x
    return _dot(d.astype(BF), pw_ref[...]) * sc_ref[...]


def _pool_prompt_kernel(x_ref, halo_ref, pw_ref, sc_ref, o_ref, buf_ref, *, tp):
    i = pl.program_id(1)
    x = x_ref[...]
    buf_ref[0:16, :] = jnp.where(i > 0, halo_ref[...], 0.0)
    buf_ref[16:16 + tp, :] = x
    acc = x
    sums = {}
    for k in range(1, POOL_WINDOWS[-1]):
        acc = acc + buf_ref[16 - k:16 - k + tp, :]
        if k + 1 in POOL_WINDOWS:
            sums[k + 1] = acc
    pos1 = i * tp + lax.broadcasted_iota(jnp.int32, (tp, 1), 0) + 1
    o_ref[...] = _pool_mix(sums, x, pos1, pw_ref, sc_ref).astype(BF)


def _pool_prompt(xpool, pw, sc, layer, batch, t, tp):
    nt = t // tp
    return pl.pallas_call(
        functools.partial(_pool_prompt_kernel, tp=tp),
        grid=(batch, nt),
        in_specs=[pl.BlockSpec((tp, POOL_CH), lambda b, i: (b * nt + i, 0)),
                  pl.BlockSpec((16, POOL_CH), lambda b, i: (jnp.maximum((b * nt + i) * (tp // 16) - 1, 0), 0)),
                  _const_spec((POOL_CH, POOL_CH), layer), _const_spec((1, POOL_CH), layer)],
        out_specs=pl.BlockSpec((tp, POOL_CH), lambda b, i: (b * nt + i, 0)),
        out_shape=jax.ShapeDtypeStruct((batch * t, POOL_CH), BF),
        scratch_shapes=[pltpu.VMEM((tp + 16, POOL_CH), F32)],
        compiler_params=_cparams("parallel", "parallel"),
        name="pool_prompt",
    )(xpool, xpool, pw, sc)


def _sample_kernel(pt_ref, qc_ref, qr_ref, gate_ref, mq_ref, nsa_new_ref, win_new_ref, moba_new_ref, xpool_ref,
                   winst_ref, poolst_ref,
                   wabk_ref, babk_ref, w2k_ref, wabv_ref, babv_ref, w2v_ref, gkc_ref,
                   ovl_ref, gexp_ref, esel_ref, emoba_ref, pw_ref, sc_ref, *rest, n_pages, page, past):
    kc_pages = rest[:n_pages]
    vc_pages = rest[n_pages:2 * n_pages]
    sel_pages = rest[2 * n_pages:3 * n_pages]
    moba_pages = rest[3 * n_pages:4 * n_pages]
    onsa_ref, omoba_ref, ypool_ref = rest[4 * n_pages:]
    del pt_ref
    row = lax.broadcasted_iota(jnp.int32, (8, 1), 0)
    lane = lax.broadcasted_iota(jnp.int32, (1, LANES), 1)
    lane_h = lane >> 6
    row_h = jnp.where(row >= NSA_GROUP, 1, 0) + jnp.where(row >= 2 * NSA_GROUP, 1, 0)
    row_g = row - row_h * NSA_GROUP
    live6 = row < NSA_HEADS
    n_groups = past // CMP_STRIDE
    gpp = page // CMP_STRIDE

    def stack_q(ref):
        s = [ref[:, g * LANES:(g + 1) * LANES].astype(F32) for g in range(NSA_GROUP)]
        q = jnp.where(row_g == 0, s[0], jnp.where(row_g == 1, s[1], s[2]))
        return jnp.where(live6 & (row_h == lane_h), q, 0.0)

    def by_group(o):
        return [jnp.sum(jnp.where(live6 & (row == lane_h * NSA_GROUP + g), o, 0.0), axis=0, keepdims=True)
                for g in range(NSA_GROUP)]

    def strided(pages, l):
        return jnp.concatenate([pg[pl.ds(l, gpp, stride=CMP_STRIDE), :] for pg in pages], axis=0)

    kraw = _compress_tokens(lambda l: strided(kc_pages, l), n_groups, wabk_ref, babk_ref, w2k_ref)
    kcmp = _head_norm(kraw, gkc_ref[...], _ones_bd())
    vcmp = _compress_tokens(lambda l: strided(vc_pages, l), n_groups, wabv_ref, babv_ref, w2v_ref)

    qc8 = stack_q(qc_ref)
    c_end = lax.broadcasted_iota(jnp.int32, (1, n_groups), 1) * CMP_STRIDE + CMP_LEN
    p_c = _softmax_rows(_dot3_nt(qc8, kcmp) * SCALE, c_end <= past + 1)
    o_cmp = _dot(p_c.astype(BF), vcmp.astype(BF))
    imp8 = _dot_sel(p_c, ovl_ref[...])
    imp = jnp.sum(jnp.where(live6 & (row_h == lane_h), imp8, 0.0), axis=0, keepdims=True)
    jblk = lane & (SEL_BLOCK - 1)
    own = past // SEL_BLOCK
    sel = _rank_select(jnp.broadcast_to(imp, (8, LANES)), jblk, own, own + 1, SEL_TOPN)[0:1]
    sel8 = jnp.where(live6 & (row_h == lane_h), sel, 0.0)

    qr8 = stack_q(qr_ref)
    qr8b = qr8.astype(BF)
    nsa_new = nsa_new_ref[...]
    picked = _dot(sel8.astype(BF), esel_ref[...]) > 0.5
    k_sel = jnp.concatenate([pg[:, 0:LANES] for pg in sel_pages], axis=0).astype(BF)
    v_sel = jnp.concatenate([pg[:, LANES:2 * LANES] for pg in sel_pages], axis=0).astype(BF)
    s = jnp.where(picked, _dot_nt(qr8b, k_sel), NEG)
    own_picked = jnp.sum(jnp.where(jblk == own, sel8, 0.0), axis=1, keepdims=True) > 0.5
    s_new = jnp.where(own_picked, jnp.sum(qr8 * nsa_new[:, 2 * LANES:3 * LANES], axis=1, keepdims=True), NEG)
    m = jnp.maximum(jnp.maximum(jnp.max(s, axis=1, keepdims=True), s_new), M_INIT)
    e = jnp.exp2(s - m)
    e_new = jnp.exp2(s_new - m)
    den = jnp.sum(e, axis=1, keepdims=True) + e_new
    acc = e_new * nsa_new[:, 3 * LANES:4 * LANES] + _dot(e.astype(BF), v_sel)
    o_sel = jnp.where(den > 0.0, acc / den, 0.0)

    win_new = win_new_ref[...]
    n_win = winst_ref.shape[0]
    kpos_w = past - n_win + lax.broadcasted_iota(jnp.int32, (1, n_win), 1)
    ok_w = (past - kpos_w) < WIN
    s = jnp.where(ok_w, _dot_nt(qr8b, winst_ref[:, 0:LANES].astype(BF)), NEG)
    s_new = jnp.sum(qr8 * win_new[:, 0:LANES], axis=1, keepdims=True)
    m = jnp.maximum(jnp.max(s, axis=1, keepdims=True), s_new)
    e = jnp.exp2(s - m)
    e_new = jnp.exp2(s_new - m)
    den = jnp.sum(e, axis=1, keepdims=True) + e_new
    o_win = (_dot(e.astype(BF), winst_ref[:, LANES:2 * LANES].astype(BF)) + e_new * win_new[:, LANES:2 * LANES]) / den

    gate8 = jnp.broadcast_to(gate_ref[...], (8, LANES))
    gh = gate8.astype(BF)
    gl = (gate8 - gh.astype(F32)).astype(BF)
    gexp = (_dot(gh, gexp_ref[...]) + _dot(gl, gexp_ref[...]))[0:1]
    branches = [by_group(o_cmp), by_group(o_sel), by_group(o_win)]
    for g in range(NSA_GROUP):
        out = jnp.zeros((1, LANES), F32)
        for br in range(3):
            out = out + gexp[:, (g * 3 + br) * LANES:(g * 3 + br + 1) * LANES] * branches[br][g]
        onsa_ref[:, g * LANES:(g + 1) * LANES] = out.astype(BF)

    mq = mq_ref[...].astype(F32)
    lane3 = lax.broadcasted_iota(jnp.int32, (1, 384), 1)
    mq8 = jnp.where(row == (lane3 >> 6), mq, 0.0)
    mq8b = mq8.astype(BF)
    n_blk = past // MOBA_BLOCK
    n_blk_pad = -(-n_blk // 16) * 16
    k_m = jnp.concatenate([pg[:, 0:384] for pg in moba_pages], axis=0)
    v_m = jnp.concatenate([pg[:, 384:768] for pg in moba_pages], axis=0)
    km = _dot(emoba_ref[0:n_blk_pad, :], k_m) * (1.0 / MOBA_BLOCK)
    km = jnp.concatenate([km.astype(BF), jnp.zeros((LANES - n_blk_pad, 384), BF)], axis=0)
    gate_m = _dot_nt(mq8b, km)
    cnt = jnp.zeros((8, LANES), F32)
    for d in range(1, n_blk):
        lower = pltpu.roll(gate_m, d, axis=1)
        upper = pltpu.roll(gate_m, LANES - d, axis=1)
        cnt = cnt + jnp.where((lane >= d) & (lower >= gate_m), 1.0, 0.0)
        cnt = cnt + jnp.where((lane + d < n_blk) & (upper > gate_m), 1.0, 0.0)
    pick_m = jnp.where((lane < n_blk) & (cnt < MOBA_TOPK), 1.0, 0.0)
    ok_m = _dot(pick_m.astype(BF), emoba_ref[...]) > 0.5
    moba_new = moba_new_ref[...]
    s = jnp.where(ok_m, _dot_nt(mq8b, k_m), NEG)
    s_new = jnp.sum(mq8 * moba_new[:, 0:384], axis=1, keepdims=True)
    m = jnp.maximum(jnp.max(s, axis=1, keepdims=True), s_new)
    e = jnp.exp2(s - m)
    e_new = jnp.exp2(s_new - m)
    den = jnp.sum(e, axis=1, keepdims=True) + e_new
    o_m = (e_new * moba_new[:, 384:768] + _dot(e.astype(BF), v_m)) / den
    omoba_ref[...] = jnp.sum(jnp.where(row == (lane3 >> 6), o_m, 0.0), axis=0, keepdims=True).astype(BF)

    x = xpool_ref[...]
    hist = poolst_ref[...]
    hrow = lax.broadcasted_iota(jnp.int32, (POOL_HIST, 1), 0)
    sums = {w: x + jnp.sum(jnp.where(hrow >= POOL_HIST - (w - 1), hist, 0.0), axis=0, keepdims=True)
            for w in POOL_WINDOWS}
    y = _pool_mix(sums, x, jnp.full((1, 1), past + 1, jnp.int32), pw_ref, sc_ref)
    ypool_ref[...] = y.astype(BF)


def _sample(page_table, rows, winst, poolst, cache_nsa, cache_moba, cw, consts, pw, sc, layer, past):
    n_seq, n_pages = page_table.shape
    page = cache_nsa.shape[2]
    per_seq = lambda w: pl.BlockSpec((None, 1, w), lambda i, pt: (i, 0, 0))
    lconst = lambda shape: pl.BlockSpec((None,) + tuple(shape), lambda i, pt: (layer,) + (0,) * len(shape),
                                        pipeline_mode=pl.Buffered(1))
    const = lambda a: pl.BlockSpec(a.shape, lambda i, pt: (0,) * a.ndim, pipeline_mode=pl.Buffered(1))
    cmp_specs = [lconst(s) for s in _cmp_weight_shapes()]

    def page_spec(width, lane_block, pj):
        return pl.BlockSpec((None, None, page, width), lambda i, pt: (layer, pt[i, pj], 0, lane_block))

    in_specs = ([per_seq(384), per_seq(384), per_seq(LANES), per_seq(384), per_seq(512), per_seq(256), per_seq(768),
                 per_seq(POOL_CH),
                 pl.BlockSpec((None, None, winst.shape[2], 256), lambda i, pt: (layer, i, 0, 0)),
                 pl.BlockSpec((None, None, POOL_HIST, POOL_CH), lambda i, pt: (layer, i, 0, 0))]
                + cmp_specs + cmp_specs + [lconst((1, LANES))]
                + [const(consts["ovl_s"]), const(consts["gexp"]), const(consts["esel"]), const(consts["emoba"]),
                   lconst((POOL_CH, POOL_CH)), lconst((1, POOL_CH))]
                + [page_spec(LANES, 0, pj) for pj in range(n_pages)] + [page_spec(LANES, 1, pj) for pj in range(n_pages)]
                + [page_spec(2 * LANES, 1, pj) for pj in range(n_pages)]
                + [page_spec(768, 0, pj) for pj in range(n_pages)])
    grid_spec = pltpu.PrefetchScalarGridSpec(
        num_scalar_prefetch=1, grid=(n_seq,), in_specs=in_specs,
        out_specs=[per_seq(384), per_seq(384), per_seq(POOL_CH)])
    r3 = lambda a: a.reshape(n_seq, 1, a.shape[1])
    outs = pl.pallas_call(
        functools.partial(_sample_kernel, n_pages=n_pages, page=page, past=past),
        grid_spec=grid_spec,
        out_shape=[jax.ShapeDtypeStruct((n_seq, 1, 384), BF), jax.ShapeDtypeStruct((n_seq, 1, 384), BF),
                   jax.ShapeDtypeStruct((n_seq, 1, POOL_CH), BF)],
        compiler_params=_cparams("parallel"),
        name="sample_step",
    )(page_table, r3(rows["qc"]), r3(rows["qr"]), r3(rows["gate"]), r3(rows["mq"]), r3(rows["nsa"]), r3(rows["win"]),
      r3(rows["moba"]), r3(rows["xpool"]), winst, poolst,
      cw["wabk"], cw["babk"], cw["w2k"], cw["wabv"], cw["babv"], cw["w2v"], cw["gkc"],
      consts["ovl_s"], consts["gexp"], consts["esel"], consts["emoba"], pw, sc,
      *([cache_nsa] * (3 * n_pages)), *([cache_moba] * n_pages))
    return [o.reshape(n_seq, o.shape[2]) for o in outs]


def _combine_ffn_kernel(x_ref, yp_ref, on_ref, om_ref, w_ref, g_ref, wg_ref, wu_ref, wd_ref, o_ref, *, f_chunk):
    y = _dot(yp_ref[...], w_ref[0:POOL_CH, :]) + _dot(on_ref[...], w_ref[POOL_CH:POOL_CH + 384, :])
    y = y + _dot(om_ref[...], w_ref[POOL_CH + 384:POOL_CH + 768, :])
    o_ref[...] = _ffn_body(x_ref[...] + y, g_ref[...], wg_ref, wu_ref, wd_ref, f_chunk)


def _combine_ffn(x, ypool, onsa, omoba, w_out, g, wg, wu, wd, layer):
    n, d = x.shape
    d_ff = wg.shape[2]
    tm = _row_tile(n)
    row = lambda w: pl.BlockSpec((tm, w), lambda i: (i, 0))
    return pl.pallas_call(
        functools.partial(_combine_ffn_kernel, f_chunk=_f_chunk(d_ff)),
        grid=(n // tm,),
        in_specs=[row(d), row(POOL_CH), row(384), row(384), _const_spec((POOL_CH + 768, d), layer),
                  _const_spec((1, d), layer),
                  _const_spec((d, d_ff), layer), _const_spec((d, d_ff), layer), _const_spec((d_ff, d), layer)],
        out_specs=row(d),
        out_shape=jax.ShapeDtypeStruct((n, d), F32),
        compiler_params=_cparams("parallel"),
        name="combine_ffn",
    )(x, ypool, onsa, omoba, w_out, g, wg, wu, wd)


def _in_col_perm():
    cols = list(range(0, POOL_CH))
    for g in range(NSA_GROUP):
        for h in range(NSA_KV_HEADS):
            base = POOL_CH + (h * NSA_GROUP + g) * HEAD_DIM
            cols += range(base, base + HEAD_DIM)
    nsa_end = POOL_CH + NSA_HEADS * HEAD_DIM
    kv_end = nsa_end + 6 * NSA_KV_HEADS * HEAD_DIM
    cols += range(nsa_end, kv_end)
    cols += range(kv_end + GATE_COLS, kv_end + GATE_COLS + 3 * MOBA_HEADS * HEAD_DIM)
    gate_cols = list(range(kv_end, kv_end + GATE_COLS))
    return np.array(cols, np.int32), np.array(gate_cols, np.int32)


def _out_row_perm():
    rows = list(range(0, POOL_CH))
    for g in range(NSA_GROUP):
        for h in range(NSA_KV_HEADS):
            base = POOL_CH + (h * NSA_GROUP + g) * HEAD_DIM
            rows += range(base, base + HEAD_DIM)
    rows += range(POOL_CH + NSA_HEADS * HEAD_DIM, POOL_CH + NSA_HEADS * HEAD_DIM + MOBA_HEADS * HEAD_DIM)
    return np.array(rows, np.int32)


def _overlap(n_cmp_rows, n_cmp_valid, n_blocks):
    n = np.arange(n_cmp_rows)
    c_start = n * CMP_STRIDE
    c_end = c_start + CMP_LEN
    jb = np.arange(SEL_BLOCK)
    ov = (c_start[:, None] < (jb[None, :] + 1) * SEL_BLOCK) & (c_end[:, None] > jb[None, :] * SEL_BLOCK)
    ov &= (n[:, None] < n_cmp_valid) & (jb[None, :] < n_blocks)
    return ov.astype(np.float32)


def _constants(t, past):
    g_p = t // CMP_STRIDE
    ov = _overlap(g_p, g_p - 1, t // SEL_BLOCK)
    ovl_p = np.zeros((2 * g_p, LANES), np.float32)
    ovl_p[:g_p, :SEL_BLOCK] = ov
    ovl_p[g_p:, SEL_BLOCK:] = ov
    g_s = past // CMP_STRIDE
    ov_s = _overlap(g_s, g_s - 1, past // SEL_BLOCK + 1)
    ovl_s = np.concatenate([ov_s, ov_s], axis=1)
    lane = np.arange(LANES)
    gexp = np.zeros((LANES, 9 * LANES), np.float32)
    for g in range(NSA_GROUP):
        for br in range(3):
            src = (lane // HEAD_DIM) * 9 + g * 3 + br
            gexp[src, (g * 3 + br) * LANES + lane] = 1.0
    kp = np.arange(past)
    esel = ((lane[:, None] % SEL_BLOCK) == (kp[None, :] // SEL_BLOCK)).astype(np.float32)
    emoba = (lane[:, None] == (kp[None, :] // MOBA_BLOCK)).astype(np.float32)
    kt = np.arange(t)
    et_sel = ((kt[:, None] // SEL_BLOCK) == (lane[None, :] % SEL_BLOCK)).astype(np.float32)
    et_moba = ((kt[:, None] // MOBA_BLOCK) == (lane[None, :] % 16)).astype(np.float32)
    as_bf = lambda a: jnp.asarray(a, BF)
    return {"ovlt_p": as_bf(ovl_p.T), "ovl_s": as_bf(ovl_s), "gexp": as_bf(gexp), "esel": as_bf(esel),
            "emoba": as_bf(emoba), "et_sel": as_bf(et_sel), "et_moba": as_bf(et_moba)}


def _rope_tables(pos):
    half = ROPE_DIM // 2
    inv_freq = ROPE_THETA ** (-jnp.arange(half, dtype=F32) / half)
    ang = pos.astype(F32)[:, None] * inv_freq[None, :]
    cos, sin, zero = jnp.cos(ang), jnp.sin(ang), jnp.zeros_like(ang)
    rest = HEAD_DIM - ROPE_DIM
    c = jnp.concatenate([cos, cos, jnp.ones((ang.shape[0], rest), F32)], axis=1)
    a = jnp.concatenate([-sin, zero, jnp.zeros((ang.shape[0], rest), F32)], axis=1)
    b = jnp.concatenate([zero, sin, jnp.zeros((ang.shape[0], rest), F32)], axis=1)
    tile2 = lambda m: jnp.concatenate([m, m], axis=1)
    return tile2(c), tile2(a), tile2(b)


def _cmp_weights(cmp_pos, cmp_w1, cmp_w2, g_nsa_kc):
    bd = lambda w: jnp.concatenate([jnp.concatenate([w, jnp.zeros_like(w)], axis=-1),
                                    jnp.concatenate([jnp.zeros_like(w), w], axis=-1)], axis=-2)
    out = {}
    depth = cmp_pos.shape[0]
    for i, name in enumerate(("k", "v")):
        pos2 = jnp.concatenate([cmp_pos[:, i], cmp_pos[:, i]], axis=-1)
        w1 = bd(cmp_w1[:, i])
        out["w2" + name] = bd(cmp_w2[:, i]).astype(BF)
        halves = [w1[:, s:s + CMP_STRIDE].reshape(depth, CMP_STRIDE * LANES, LANES) for s in (0, CMP_STRIDE)]
        out["wab" + name] = jnp.concatenate(halves, axis=-1).astype(BF)
        bias = [jnp.einsum("xlc,xlce->xe", pos2[:, s:s + CMP_STRIDE], w1[:, s:s + CMP_STRIDE],
                           precision=lax.Precision.HIGHEST) for s in (0, CMP_STRIDE)]
        out["bab" + name] = jnp.concatenate(bias, axis=-1)[:, None, :]
    out["gkc"] = jnp.concatenate([g_nsa_kc, g_nsa_kc], axis=-1)[:, None, :]
    return out


def kernel(x_prompt, x_sample, cache_nsa_kv, cache_moba_kv, state_nsa_win, state_pool, page_table, g_ffa, w_ffa_gate, w_ffa_up, w_ffa_down, g_mix, w_in, w_out, pool_w, pool_scale, g_nsa_q, g_nsa_kc, g_nsa_ks, g_nsa_kw, cmp_pos, cmp_w1, cmp_w2, g_moba_q, g_moba_k, g_ffb, w_ffb_gate, w_ffb_up, w_ffb_down):
    batch, t, d = x_prompt.shape
    n_seq = x_sample.shape[0]
    depth, n_pool, page = cache_nsa_kv.shape[:3]
    n_pages = page_table.shape[1]
    past = n_pages * page
    n_p = batch * t
    n_win = state_nsa_win.shape[2]
    assert x_sample.shape[1] == 1 and t % MOBA_BLOCK == 0 and past % MOBA_BLOCK == 0 and n_win == WIN

    tp = _row_tile(t)
    tq_nsa = 128
    tq_moba = MOBA_BLOCK

    col_perm, gate_cols = _in_col_perm()
    w_in_p = jnp.concatenate([w_in[:, :, col_perm], w_in[:, :, gate_cols],
                              jnp.zeros((depth, d, LANES - GATE_COLS), w_in.dtype)], axis=-1).astype(BF)
    w_out_p = w_out[:, _out_row_perm(), :].astype(BF)
    bf = lambda w: w.astype(BF)
    wa = (bf(w_ffa_gate), bf(w_ffa_up), bf(w_ffa_down))
    wb = (bf(w_ffb_gate), bf(w_ffb_up), bf(w_ffb_down))
    tile2 = lambda g: jnp.concatenate([g, g], axis=-1)
    gains = jnp.stack([tile2(g_nsa_q), tile2(g_nsa_ks), tile2(g_nsa_kw), tile2(g_moba_q), tile2(g_moba_k)]
                      + [jnp.ones((depth, LANES), F32)] * 3, axis=1)
    cw = _cmp_weights(cmp_pos, cmp_w1, cmp_w2, g_nsa_kc)
    eye = jnp.eye(len(POOL_WINDOWS), dtype=pool_w.dtype)
    pw_bd = jnp.einsum("lgij,gh->lgihj", pool_w, eye).reshape(depth, POOL_CH, POOL_CH).astype(BF)
    sc = pool_scale[:, None, :]
    consts = _constants(t, past)
    rope_p = _rope_tables(jnp.arange(t, dtype=jnp.int32))
    rope_s = _rope_tables(jnp.full((n_seq,), past, jnp.int32))
    g_a, g_m, g_b = g_ffa[:, None, :], g_mix[:, None, :], g_ffb[:, None, :]

    cache_nsa = cache_nsa_kv.reshape(depth, n_pool, page, 512)
    cache_moba = cache_moba_kv.astype(BF).reshape(depth, n_pool, page, 768)
    winst = state_nsa_win.reshape(depth, n_seq, n_win, 256)

    xp = x_prompt.reshape(n_p, d)
    xs = x_sample.reshape(n_seq, d)
    names = ("xpool", "qc", "qr", "nsa", "nsab", "win", "winb", "mq", "moba", "mobab", "gate")
    outs = [[] for _ in range(8)]
    for l in range(depth):
        xp = _ffn(xp, g_a, *wa, l)
        xs = _ffn(xs, g_a, *wa, l)
        pp = dict(zip(names, _prep(xp, g_m, w_in_p, gains, rope_p, l)))
        ps = dict(zip(names, _prep(xs, g_m, w_in_p, gains, rope_s, l)))

        kcmp, vcmp = _compress_prompt(pp["nsa"], cw, l, batch, t)
        onsa_p = _nsa_prompt(pp["qc"], pp["qr"], pp["gate"], kcmp, vcmp, pp["nsab"], pp["winb"], consts["ovlt_p"],
                             consts["gexp"], consts["et_sel"], batch, t, tq_nsa)
        omoba_p = _moba_prompt(pp["mq"], pp["mobab"], consts["et_moba"], batch, t, tq_moba)
        ypool_p = _pool_prompt(pp["xpool"], pw_bd, sc, l, batch, t, tp)
        onsa_s, omoba_s, ypool_s = _sample(page_table, ps, winst, state_pool, cache_nsa, cache_moba, cw, consts,
                                           pw_bd, sc, l, past)

        xp = _combine_ffn(xp, ypool_p, onsa_p, omoba_p, w_out_p, g_b, *wb, l)
        xs = _combine_ffn(xs, ypool_s, onsa_s, omoba_s, w_out_p, g_b, *wb, l)

        outs[0].append(pp["nsa"].reshape(batch, t, 4, NSA_KV_HEADS, HEAD_DIM))
        outs[1].append(ps["nsa"].reshape(n_seq, 1, 4, NSA_KV_HEADS, HEAD_DIM))
        outs[2].append(pp["moba"].reshape(batch, t, 2, MOBA_HEADS, HEAD_DIM))
        outs[3].append(ps["moba"].reshape(n_seq, 1, 2, MOBA_HEADS, HEAD_DIM))
        outs[4].append(pp["win"].reshape(batch, t, 2, NSA_KV_HEADS, HEAD_DIM)[:, t - min(WIN, t):])
        outs[5].append(ps["win"].reshape(n_seq, 1, 2, NSA_KV_HEADS, HEAD_DIM))
        outs[6].append(pp["xpool"].reshape(batch, t, POOL_CH)[:, t - POOL_HIST:])
        outs[7].append(ps["xpool"].reshape(n_seq, 1, POOL_CH))

    stacked = [jnp.stack(o, axis=0) for o in outs]
    stacked[5] = jnp.concatenate([state_nsa_win[:, :, 1:], stacked[5]], axis=2)
    stacked[7] = jnp.concatenate([state_pool[:, :, 1:], stacked[7]], axis=2)
    return (xp.reshape(batch, t, d), xs.reshape(n_seq, 1, d), *stacked)
```

```python
import functools

import numpy as np
import jax
import jax.numpy as jnp
from jax import lax
from jax.experimental import pallas as pl
from jax.experimental.pallas import tpu as pltpu

F32 = jnp.float32
BF = jnp.bfloat16

HEAD_DIM = 64
LANES = 128
POOL_CH = 256
POOL_WINDOWS = (2, 4, 8, 16)
POOL_HIST = 15
NSA_KV_HEADS = 2
NSA_GROUP = 3
NSA_HEADS = 6
MOBA_HEADS = 6
CMP_LEN = 32
CMP_STRIDE = 16
SEL_BLOCK = 64
SEL_TOPN = 16
WIN = 512
MOBA_BLOCK = 256
MOBA_TOPK = 3
ROPE_THETA = 500000.0
ROPE_DIM = 16
EPS = 1e-6
NEG = -1e30
M_INIT = -1e29
SEL_FORCE = 1e4
SCALE = HEAD_DIM ** -0.5
Q_PRESCALE = SCALE * 1.4426950408889634
GATE_COLS = 3 * NSA_HEADS
D_QKV = 2560
D_IN_PAD = D_QKV + LANES
VMEM_LIMIT = 56 * 1024 * 1024


def _cparams(*sem):
    return pltpu.CompilerParams(dimension_semantics=sem, vmem_limit_bytes=VMEM_LIMIT)


def _dot(a, b):
    return jnp.dot(a, b, preferred_element_type=F32)


def _dot_nt(a, b):
    return lax.dot_general(a, b, (((1,), (1,)), ((), ())), preferred_element_type=F32)


def _split3(x):
    hi = x.astype(BF)
    r = x - hi.astype(F32)
    mid = r.astype(BF)
    lo = (r - mid.astype(F32)).astype(BF)
    return hi, mid, lo


def _dot_sel(x, m):
    hi, mid, lo = _split3(x)
    return _dot(hi, m) + _dot(mid, m) + _dot(lo, m)


def _dot3_nt(a, b):
    ah = a.astype(BF)
    al = (a - ah.astype(F32)).astype(BF)
    bh = b.astype(BF)
    bl = (b - bh.astype(F32)).astype(BF)
    return _dot_nt(ah, bh) + _dot_nt(ah, bl) + _dot_nt(al, bh)


def _sigmoid(x):
    return 1.0 / (1.0 + jnp.exp(-x))


def _silu(x):
    return x * _sigmoid(x)


def _rms_rows(x, g):
    return x * lax.rsqrt(jnp.mean(x * x, axis=-1, keepdims=True) + EPS) * g


def _ones_bd():
    r = lax.broadcasted_iota(jnp.int32, (LANES, LANES), 0) >> 6
    c = lax.broadcasted_iota(jnp.int32, (LANES, LANES), 1) >> 6
    return jnp.where(r == c, 1.0, 0.0).astype(BF)


def _head_norm(x, gain, ones_bd):
    ms = _dot_sel(x * x, ones_bd) * (1.0 / HEAD_DIM)
    return x * lax.rsqrt(ms + EPS) * gain


def _rope(x, c, a, b):
    return x * c + pltpu.roll(x, LANES - ROPE_DIM // 2, axis=1) * a + pltpu.roll(x, ROPE_DIM // 2, axis=1) * b


def _softmax_rows(s, valid):
    s = jnp.where(valid, s, NEG)
    m = jnp.max(s, axis=1, keepdims=True)
    e = jnp.where(valid, jnp.exp(s - m), 0.0)
    den = jnp.sum(e, axis=1, keepdims=True)
    return jnp.where(den > 0.0, e / den, 0.0)


def _sum_list(xs):
    return functools.reduce(lambda a, b: a + b, xs)


def _ffn_body(x, g, wg_ref, wu_ref, wd_ref, f_chunk):
    xn = _rms_rows(x, g).astype(BF)
    d_ff = wg_ref.shape[1]
    acc = jnp.zeros(x.shape, F32)
    for c in range(d_ff // f_chunk):
        sl = slice(c * f_chunk, (c + 1) * f_chunk)
        gate = _dot(xn, wg_ref[:, sl])
        up = _dot(xn, wu_ref[:, sl])
        h = (_silu(gate) * up).astype(BF)
        acc = acc + _dot(h, wd_ref[sl, :])
    return x + 0.5 * acc


def _ffn_kernel(x_ref, g_ref, wg_ref, wu_ref, wd_ref, o_ref, *, f_chunk):
    o_ref[...] = _ffn_body(x_ref[...], g_ref[...], wg_ref, wu_ref, wd_ref, f_chunk)


def _f_chunk(d_ff):
    for c in (1408, 1024, 512, 256, 128):
        if d_ff % c == 0:
            return c
    return d_ff


def _row_tile(n):
    return next((c for c in (512, 384, 256, 128) if n % c == 0), n)


def _const_spec(shape, layer):
    nd = len(shape)
    return pl.BlockSpec((None,) + tuple(shape), lambda *_: (layer,) + (0,) * nd, pipeline_mode=pl.Buffered(1))


def _ffn(x, g, wg, wu, wd, layer):
    n, d = x.shape
    d_ff = wg.shape[2]
    tm = _row_tile(n)
    return pl.pallas_call(
        functools.partial(_ffn_kernel, f_chunk=_f_chunk(d_ff)),
        grid=(n // tm,),
        in_specs=[pl.BlockSpec((tm, d), lambda i: (i, 0)),
                  _const_spec((1, d), layer),
                  _const_spec((d, d_ff), layer), _const_spec((d, d_ff), layer), _const_spec((d_ff, d), layer)],
        out_specs=pl.BlockSpec((tm, d), lambda i: (i, 0)),
        out_shape=jax.ShapeDtypeStruct((n, d), F32),
        compiler_params=_cparams("parallel"),
        name="ffn",
    )(x, g, wg, wu, wd)


def _prep_kernel(x_ref, g_ref, w_ref, gains_ref, c_ref, a_ref, b_ref,
                 xpool_ref, qc_ref, qr_ref, nsa_ref, nsab_ref, win_ref, winb_ref,
                 mq_ref, moba_ref, mobab_ref, gate_ref):
    xn = _rms_rows(x_ref[...], g_ref[...]).astype(BF)
    z = _dot(xn, w_ref[...])
    ones_bd = _ones_bd()
    c, a, b = c_ref[...], a_ref[...], b_ref[...]
    gains = gains_ref[...]
    slab = lambda i: z[:, i * LANES:(i + 1) * LANES]
    norm_rope = lambda s, gi: _rope(_head_norm(s, gains[gi:gi + 1], ones_bd), c, a, b)

    xpool_ref[...] = z[:, 0:POOL_CH]
    for gi in range(3):
        qn = _head_norm(slab(2 + gi), gains[0:1], ones_bd)
        qc_ref[:, gi * LANES:(gi + 1) * LANES] = qn
        qr_ref[:, gi * LANES:(gi + 1) * LANES] = (_rope(qn, c, a, b) * Q_PRESCALE).astype(BF)
    nsa = [slab(5), slab(6), norm_rope(slab(7), 1), slab(8)]
    for i, s in enumerate(nsa):
        nsa_ref[:, i * LANES:(i + 1) * LANES] = s
        nsab_ref[:, i * LANES:(i + 1) * LANES] = s.astype(BF)
    win = [norm_rope(slab(9), 2), slab(10)]
    for i, s in enumerate(win):
        win_ref[:, i * LANES:(i + 1) * LANES] = s
        winb_ref[:, i * LANES:(i + 1) * LANES] = s.astype(BF)
    for i in range(3):
        mq_ref[:, i * LANES:(i + 1) * LANES] = (norm_rope(slab(11 + i), 3) * Q_PRESCALE).astype(BF)
    for i in range(6):
        s = norm_rope(slab(14 + i), 4) if i < 3 else slab(14 + i)
        moba_ref[:, i * LANES:(i + 1) * LANES] = s
        mobab_ref[:, i * LANES:(i + 1) * LANES] = s.astype(BF)
    gate_ref[...] = _sigmoid(slab(20))


def _prep(x, g, w_in, gains, rope, layer):
    n, d = x.shape
    tm = _row_tile(min(n, rope[0].shape[0]))
    nb = rope[0].shape[0] // tm
    row = lambda w: pl.BlockSpec((tm, w), lambda i: (i, 0))
    tab = pl.BlockSpec((tm, LANES), lambda i: (i % nb, 0))
    widths_dtypes = [(POOL_CH, F32), (384, F32), (384, BF), (512, F32), (512, BF), (256, F32), (256, BF),
                     (384, BF), (768, F32), (768, BF), (LANES, F32)]
    return pl.pallas_call(
        _prep_kernel,
        grid=(n // tm,),
        in_specs=[row(d), _const_spec((1, d), layer), _const_spec((d, D_IN_PAD), layer),
                  _const_spec((8, LANES), layer), tab, tab, tab],
        out_specs=[row(w) for w, _ in widths_dtypes],
        out_shape=[jax.ShapeDtypeStruct((n, w), dt) for w, dt in widths_dtypes],
        compiler_params=_cparams("parallel"),
        name="prep",
    )(x, g, w_in, gains, *rope)


def _compress_tokens(load_rows, n_groups, wab_ref, bab_ref, w2_ref):
    x = jnp.concatenate([load_rows(l) for l in range(CMP_STRIDE)], axis=1).astype(BF)
    pab = _dot(x, wab_ref[...]) + bab_ref[...]
    pre = pab[:, 0:LANES] + pltpu.roll(pab[:, LANES:2 * LANES], n_groups - 1, axis=0)
    return _dot(_silu(pre).astype(BF), w2_ref[...])


def _compress_prompt_kernel(kc_ref, vc_ref, wabk_ref, babk_ref, w2k_ref, wabv_ref, babv_ref, w2v_ref, gkc_ref,
                            kcmp_ref, vcmp_ref, *, n_groups):
    kraw = _compress_tokens(lambda l: kc_ref[pl.ds(l, n_groups, stride=CMP_STRIDE), :],
                            n_groups, wabk_ref, babk_ref, w2k_ref)
    kcmp_ref[...] = _head_norm(kraw, gkc_ref[...], _ones_bd())
    vcmp_ref[...] = _compress_tokens(lambda l: vc_ref[pl.ds(l, n_groups, stride=CMP_STRIDE), :],
                                     n_groups, wabv_ref, babv_ref, w2v_ref)


def _cmp_weight_shapes():
    return [(CMP_STRIDE * LANES, 2 * LANES), (1, 2 * LANES), (LANES, LANES)]


def _compress_prompt(nsa_rows, cw, layer, batch, t):
    n_groups = t // CMP_STRIDE
    wspecs = [_const_spec(s, layer) for s in _cmp_weight_shapes()]
    return pl.pallas_call(
        functools.partial(_compress_prompt_kernel, n_groups=n_groups),
        grid=(batch,),
        in_specs=[pl.BlockSpec((t, LANES), lambda b: (b, 0)), pl.BlockSpec((t, LANES), lambda b: (b, 1))]
                 + wspecs + wspecs + [_const_spec((1, LANES), layer)],
        out_specs=[pl.BlockSpec((n_groups, LANES), lambda b: (b, 0))] * 2,
        out_shape=[jax.ShapeDtypeStruct((batch * n_groups, LANES), F32)] * 2,
        compiler_params=_cparams("parallel"),
        name="compress_prompt",
    )(nsa_rows, nsa_rows, cw["wabk"], cw["babk"], cw["w2k"], cw["wabv"], cw["babv"], cw["w2v"], cw["gkc"])


def _rank_select(imp, j, own, n_blocks, top_n):
    forced = (j == 0) | (j == own) | (j == own - 1)
    imp = jnp.where(forced, imp + SEL_FORCE, imp)
    imp = jnp.where(j <= own, imp, -jnp.inf)
    cnt = jnp.zeros(imp.shape, F32)
    for d in range(1, SEL_BLOCK):
        if d > n_blocks - 1 and d < SEL_BLOCK - (n_blocks - 1):
            continue
        nowrap = j >= d
        r = jnp.where(nowrap, pltpu.roll(imp, d, axis=1), pltpu.roll(imp, (d - SEL_BLOCK) % LANES, axis=1))
        cnt = cnt + jnp.where(nowrap, jnp.where(r >= imp, 1.0, 0.0), jnp.where(r > imp, 1.0, 0.0))
    return jnp.where(j <= own, jnp.where(cnt < top_n, 1.0, 0.0), 0.0)


def _online_update(s, v, m_ref, l_ref, acc_ref, idx):
    m_prev = m_ref[idx]
    m_new = jnp.maximum(m_prev, jnp.max(s, axis=1, keepdims=True))
    alpha = jnp.exp2(m_prev - m_new)
    ps = [jnp.exp2(s[:, i * LANES:(i + 1) * LANES] - m_new) for i in range(s.shape[1] // LANES)]
    l_ref[idx] = alpha * l_ref[idx] + _sum_list(ps)
    acc_ref[idx] = alpha * acc_ref[idx] + _dot(jnp.concatenate(ps, axis=1).astype(BF), v)
    m_ref[idx] = m_new


def _online_init(m_ref, l_ref, acc_ref):
    m_ref[...] = jnp.full(m_ref.shape, M_INIT, F32)
    l_ref[...] = jnp.zeros(l_ref.shape, F32)
    acc_ref[...] = jnp.zeros(acc_ref.shape, F32)


def _online_out(l_ref, acc_ref, idx):
    l = jnp.sum(l_ref[idx], axis=1, keepdims=True)
    return jnp.where(l > 0.0, acc_ref[idx] / l, 0.0)


def _nsa_prompt_kernel(qc_ref, qr_ref, gate_ref, kcmp_ref, vcmp_ref, nsab_ref, winb_ref, ovlt_ref, gexp_ref, et_ref,
                       o_ref, impt_ref, m_ref, l_ref, acc_ref, *, tq, tk):
    qi = pl.program_id(1)
    q0 = qi * tq
    lane_h = lax.broadcasted_iota(jnp.int32, (1, LANES), 1) >> 6
    qpos = q0 + lax.broadcasted_iota(jnp.int32, (tq, 1), 0)
    qpos6 = jnp.concatenate([qpos] * NSA_HEADS, axis=0)

    def stack6(tile, zero):
        return jnp.concatenate([jnp.where(lane_h == h, tile(g), zero)
                                for h in range(NSA_KV_HEADS) for g in range(NSA_GROUP)], axis=0)

    kc = kcmp_ref[...]
    n_cmp = kc.shape[0]
    c_end = lax.broadcasted_iota(jnp.int32, (1, n_cmp), 1) * CMP_STRIDE + CMP_LEN
    q_cmp = stack6(lambda g: qc_ref[:, g * LANES:(g + 1) * LANES], 0.0)
    p_c = _softmax_rows(_dot3_nt(q_cmp, kc) * SCALE, c_end <= qpos6 + 1)
    o_cmp = _dot(p_c.astype(BF), vcmp_ref[...].astype(BF))
    psum = [_sum_list([p_c[(h * NSA_GROUP + g) * tq:(h * NSA_GROUP + g + 1) * tq] for g in range(NSA_GROUP)])
            for h in range(NSA_KV_HEADS)]
    hi, mid, lo = _split3(jnp.concatenate(psum, axis=1))
    ovlt = ovlt_ref[...]
    impt = _dot_nt(ovlt, hi) + _dot_nt(ovlt, mid) + _dot_nt(ovlt, lo)
    brow = lax.broadcasted_iota(jnp.int32, (LANES, 1), 0)
    jrow = brow & (SEL_BLOCK - 1)
    own_l = (q0 + lax.broadcasted_iota(jnp.int32, (1, tq), 1)) >> 6
    forced = (jrow == 0) | (jrow == own_l) | (jrow == own_l - 1)
    impt = jnp.where(forced, impt + SEL_FORCE, impt)
    impt = jnp.where(jrow <= own_l, impt, -jnp.inf)
    impt_ref[...] = impt

    def rank_step(jp, cnt):
        comp = jnp.where(brow < SEL_BLOCK, impt_ref[pl.ds(jp, 1), :], impt_ref[pl.ds(SEL_BLOCK + jp, 1), :])
        return cnt + jnp.where(jrow > jp, jnp.where(comp >= impt, 1.0, 0.0), jnp.where(comp > impt, 1.0, 0.0))

    n_live = (q0 + tq - 1) // SEL_BLOCK + 1
    cnt = lax.fori_loop(0, n_live, rank_step, jnp.zeros((LANES, tq), F32))
    sel_t = jnp.where(jrow <= own_l, jnp.where(cnt < SEL_TOPN, 1.0, 0.0), 0.0)
    sel_neg = jnp.where(sel_t.T > 0.5, 0.0, NEG)

    zero = jnp.zeros((), BF)
    q_rot = stack6(lambda g: qr_ref[:, g * LANES:(g + 1) * LANES], zero)
    q_aug = jnp.concatenate([q_rot, stack6(lambda g: sel_neg, 0.0).astype(BF)], axis=1)
    _online_init(m_ref, l_ref, acc_ref)

    def sel_chunk(c, causal):
        start = pl.multiple_of(c * tk, tk)
        k_aug = jnp.concatenate([nsab_ref[pl.ds(start, tk), 2 * LANES:3 * LANES], et_ref[pl.ds(start, tk), :]], axis=1)
        v = nsab_ref[pl.ds(start, tk), 3 * LANES:4 * LANES]
        s = _dot_nt(q_aug, k_aug)
        if causal:
            kpos = start + lax.broadcasted_iota(jnp.int32, (1, tk), 1)
            s = jnp.where(kpos <= qpos6, s, NEG)
        _online_update(s, v, m_ref, l_ref, acc_ref, 0)

    c_last = (q0 + tq - 1) // tk
    lax.fori_loop(0, c_last, lambda c, carry: (sel_chunk(c, False), carry)[1], 0)
    sel_chunk(c_last, True)
    o_sel = _online_out(l_ref, acc_ref, 0)

    n_slots = WIN // tq + 1
    slot_c = [qi - (n_slots - 1) + slot for slot in range(n_slots)]
    starts = [pl.multiple_of(jnp.maximum(c, 0) * tq, tq) for c in slot_c]
    k_band = jnp.concatenate([winb_ref[pl.ds(st, tq), 0:LANES] for st in starts], axis=0)
    v_band = jnp.concatenate([winb_ref[pl.ds(st, tq), LANES:2 * LANES] for st in starts], axis=0)
    s_band = _dot_nt(q_rot, k_band)
    s_w = []
    for slot in range(n_slots):
        s = s_band[:, slot * tq:(slot + 1) * tq]
        kpos = starts[slot] + lax.broadcasted_iota(jnp.int32, (1, tq), 1)
        if slot == 0:
            s = jnp.where(kpos > qpos6 - WIN, s, NEG)
        if slot == n_slots - 1:
            s = jnp.where(kpos <= qpos6, s, NEG)
        else:
            s = jnp.where(slot_c[slot] >= 0, s, NEG)
        s_w.append(s)
    m = jnp.max(functools.reduce(jnp.maximum, s_w), axis=1, keepdims=True)
    ps = [jnp.exp2(s - m) for s in s_w]
    l = jnp.sum(_sum_list(ps), axis=1, keepdims=True)
    o_win = _dot(jnp.concatenate(ps, axis=1).astype(BF), v_band) / l

    gate = gate_ref[...]
    gh = gate.astype(BF)
    gl = (gate - gh.astype(F32)).astype(BF)
    gexp = _dot(gh, gexp_ref[...]) + _dot(gl, gexp_ref[...])
    for g in range(NSA_GROUP):
        r0, r1 = g * tq, (NSA_GROUP + g) * tq
        out = jnp.zeros((tq, LANES), F32)
        for br, o in enumerate((o_cmp, o_sel, o_win)):
            both = jnp.where(lane_h == 0, o[r0:r0 + tq], o[r1:r1 + tq])
            out = out + gexp[:, (g * 3 + br) * LANES:(g * 3 + br + 1) * LANES] * both
        o_ref[:, g * LANES:(g + 1) * LANES] = out.astype(BF)


def _nsa_prompt(qc, qr, gate, kcmp, vcmp, nsab, winb, ovlt, gexp, et, batch, t, tq):
    nq = t // tq
    n_groups = t // CMP_STRIDE
    assert WIN % tq == 0 and t // SEL_BLOCK <= SEL_BLOCK
    tile = lambda w: pl.BlockSpec((tq, w), lambda b, i: (b * nq + i, 0))
    per_b = lambda r, w: pl.BlockSpec((r, w), lambda b, i: (b, 0))
    const = lambda a: pl.BlockSpec(a.shape, lambda b, i: (0, 0), pipeline_mode=pl.Buffered(1))
    return pl.pallas_call(
        functools.partial(_nsa_prompt_kernel, tq=tq, tk=min(512, t)),
        grid=(batch, nq),
        in_specs=[tile(384), tile(384), tile(LANES), per_b(n_groups, LANES), per_b(n_groups, LANES),
                  per_b(t, 512), per_b(t, 256), const(ovlt), const(gexp), const(et)],
        out_specs=tile(384),
        out_shape=jax.ShapeDtypeStruct((batch * t, 384), BF),
        scratch_shapes=[pltpu.VMEM((LANES, tq), F32)] + [pltpu.VMEM((1, NSA_HEADS * tq, LANES), F32)] * 3,
        compiler_params=_cparams("parallel", "parallel"),
        name="nsa_prompt",
    )(qc, qr, gate, kcmp, vcmp, nsab, winb, ovlt, gexp, et)


def _moba_prompt_kernel(mq_ref, mobab_ref, et_ref, o_ref, kmbd_ref, gate_ref, m_ref, l_ref, acc_ref, *, tq, t, tk):
    qi = pl.program_id(1)
    q0 = qi * tq
    n_pairs = MOBA_HEADS // 2
    lane = lax.broadcasted_iota(jnp.int32, (1, LANES), 1)
    lane_h = lane >> 6
    hh = lane >> 4
    qpos = q0 + lax.broadcasted_iota(jnp.int32, (tq, 1), 0)
    qpos2 = jnp.concatenate([qpos, qpos], axis=0)
    brow = lax.broadcasted_iota(jnp.int32, (LANES, 1), 0)
    jrow = brow & 15

    @pl.when(qi == 0)
    def _():
        tt = lax.broadcasted_iota(jnp.int32, (1, t), 1) >> 8
        ind = jnp.where(jrow == tt, 1.0, 0.0).astype(BF)
        km = _dot(ind, mobab_ref[:, 0:384]) * (1.0 / MOBA_BLOCK)
        col_h = lax.broadcasted_iota(jnp.int32, (1, 384), 1) >> 6
        kmbd_ref[...] = jnp.where(col_h == (brow >> 4), km, 0.0).astype(BF)

    gate = _dot_nt(kmbd_ref[...], mq_ref[...])
    gate_ref[...] = gate
    own_l = (q0 + lax.broadcasted_iota(jnp.int32, (1, tq), 1)) >> 8

    def rank_step(jp, cnt):
        comp = jnp.concatenate([jnp.broadcast_to(gate_ref[pl.ds(hd * 16 + jp, 1), :], (16, tq))
                                for hd in range(LANES // 16)], axis=0)
        beats = jnp.where(jrow > jp, jnp.where(comp >= gate, 1.0, 0.0), jnp.where(comp > gate, 1.0, 0.0))
        return cnt + jnp.where(jp < own_l, beats, 0.0)

    n_past = (q0 + tq - 1) // MOBA_BLOCK
    cnt = lax.fori_loop(0, n_past, rank_step, jnp.zeros((LANES, tq), F32))
    picked_t = jnp.where(jrow == own_l, 1.0, jnp.where(jrow < own_l, jnp.where(cnt < MOBA_TOPK, 1.0, 0.0), 0.0))
    pick_neg = jnp.where(picked_t.T > 0.5, 0.0, NEG)

    zero = jnp.zeros((), BF)
    q_aug = []
    for p in range(n_pairs):
        qp = mq_ref[:, p * LANES:(p + 1) * LANES]
        q2 = jnp.concatenate([jnp.where(lane_h == 0, qp, zero), jnp.where(lane_h == 1, qp, zero)], axis=0)
        pk = jnp.concatenate([jnp.where(hh == 2 * p, pick_neg, 0.0), jnp.where(hh == 2 * p + 1, pick_neg, 0.0)], axis=0)
        q_aug.append(jnp.concatenate([q2, pk.astype(BF)], axis=1))
    _online_init(m_ref, l_ref, acc_ref)

    def chunk(c, causal):
        start = pl.multiple_of(c * tk, tk)
        et = et_ref[pl.ds(start, tk), :]
        for p in range(n_pairs):
            k_aug = jnp.concatenate([mobab_ref[pl.ds(start, tk), p * LANES:(p + 1) * LANES], et], axis=1)
            v = mobab_ref[pl.ds(start, tk), 384 + p * LANES:384 + (p + 1) * LANES]
            s = _dot_nt(q_aug[p], k_aug)
            if causal:
                kpos = start + lax.broadcasted_iota(jnp.int32, (1, tk), 1)
                s = jnp.where(kpos <= qpos2, s, NEG)
            _online_update(s, v, m_ref, l_ref, acc_ref, p)

    c_last = (q0 + tq - 1) // tk
    lax.fori_loop(0, c_last, lambda c, carry: (chunk(c, False), carry)[1], 0)
    chunk(c_last, True)
    for p in range(n_pairs):
        out = _online_out(l_ref, acc_ref, p)
        o_ref[:, p * LANES:(p + 1) * LANES] = jnp.where(lane_h == 0, out[0:tq], out[tq:2 * tq]).astype(BF)


def _moba_prompt(mq, mobab, et, batch, t, tq):
    nq = t // tq
    assert t // MOBA_BLOCK <= 16
    tile = lambda w: pl.BlockSpec((tq, w), lambda b, i: (b * nq + i, 0))
    return pl.pallas_call(
        functools.partial(_moba_prompt_kernel, tq=tq, t=t, tk=min(512, t)),
        grid=(batch, nq),
        in_specs=[tile(384), pl.BlockSpec((t, 768), lambda b, i: (b, 0)),
                  pl.BlockSpec(et.shape, lambda b, i: (0, 0), pipeline_mode=pl.Buffered(1))],
        out_specs=tile(384),
        out_shape=jax.ShapeDtypeStruct((batch * t, 384), BF),
        scratch_shapes=[pltpu.VMEM((LANES, 384), BF), pltpu.VMEM((LANES, tq), F32)]
                       + [pltpu.VMEM((MOBA_HEADS // 2, 2 * tq, LANES), F32)] * 3,
        compiler_params=_cparams("parallel", "arbitrary"),
        name="moba_prompt",
    )(mq, mobab, et)


def _pool_mix(win_sums, x, count_pos, pw_ref, sc_ref):
    lane_g = lax.broadcasted_iota(jnp.int32, (1, POOL_CH), 1) >> 6
    s = win_sums[POOL_WINDOWS[-1]]
    w = jnp.full((1, POOL_CH), POOL_WINDOWS[-1], jnp.int32)
    for gi in range(len(POOL_WINDOWS) - 2, -1, -1):
        s = jnp.where(lane_g == gi, win_sums[POOL_WINDOWS[gi]], s)
        w = jnp.where(lane_g == gi, POOL_WINDOWS[gi], w)
    count = jnp.minimum(count_pos, w).astype(F32)
    d = s / count - x
    return _dot(d.astype(BF), pw_ref[...]) * sc_ref[...]


def _pool_prompt_kernel(x_ref, halo_ref, pw_ref, sc_ref, o_ref, buf_ref, *, tp):
    i = pl.program_id(1)
    x = x_ref[...]
    buf_ref[0:16, :] = jnp.where(i > 0, halo_ref[...], 0.0)
    buf_ref[16:16 + tp, :] = x
    acc = x
    sums = {}
    for k in range(1, POOL_WINDOWS[-1]):
        acc = acc + buf_ref[16 - k:16 - k + tp, :]
        if k + 1 in POOL_WINDOWS:
            sums[k + 1] = acc
    pos1 = i * tp + lax.broadcasted_iota(jnp.int32, (tp, 1), 0) + 1
    o_ref[...] = _pool_mix(sums, x, pos1, pw_ref, sc_ref).astype(BF)


def _pool_prompt(xpool, pw, sc, layer, batch, t, tp):
    nt = t // tp
    return pl.pallas_call(
        functools.partial(_pool_prompt_kernel, tp=tp),
        grid=(batch, nt),
        in_specs=[pl.BlockSpec((tp, POOL_CH), lambda b, i: (b * nt + i, 0)),
                  pl.BlockSpec((16, POOL_CH), lambda b, i: (jnp.maximum((b * nt + i) * (tp // 16) - 1, 0), 0)),
                  _const_spec((POOL_CH, POOL_CH), layer), _const_spec((1, POOL_CH), layer)],
        out_specs=pl.BlockSpec((tp, POOL_CH), lambda b, i: (b * nt + i, 0)),
        out_shape=jax.ShapeDtypeStruct((batch * t, POOL_CH), BF),
        scratch_shapes=[pltpu.VMEM((tp + 16, POOL_CH), F32)],
        compiler_params=_cparams("parallel", "parallel"),
        name="pool_prompt",
    )(xpool, xpool, pw, sc)


def _sample_kernel(pt_ref, qc_ref, qr_ref, gate_ref, mq_ref, nsa_new_ref, win_new_ref, moba_new_ref, xpool_ref,
                   winst_ref, poolst_ref,
                   wabk_ref, babk_ref, w2k_ref, wabv_ref, babv_ref, w2v_ref, gkc_ref,
                   ovl_ref, gexp_ref, esel_ref, emoba_ref, pw_ref, sc_ref, *rest, n_pages, page, past):
    kc_pages = rest[:n_pages]
    vc_pages = rest[n_pages:2 * n_pages]
    sel_pages = rest[2 * n_pages:3 * n_pages]
    moba_pages = rest[3 * n_pages:4 * n_pages]
    onsa_ref, omoba_ref, ypool_ref = rest[4 * n_pages:]
    del pt_ref
    row = lax.broadcasted_iota(jnp.int32, (8, 1), 0)
    lane = lax.broadcasted_iota(jnp.int32, (1, LANES), 1)
    lane_h = lane >> 6
    row_h = jnp.where(row >= NSA_GROUP, 1, 0) + jnp.where(row >= 2 * NSA_GROUP, 1, 0)
    row_g = row - row_h * NSA_GROUP
    live6 = row < NSA_HEADS
    n_groups = past // CMP_STRIDE
    gpp = page // CMP_STRIDE

    def stack_q(ref):
        s = [ref[:, g * LANES:(g + 1) * LANES].astype(F32) for g in range(NSA_GROUP)]
        q = jnp.where(row_g == 0, s[0], jnp.where(row_g == 1, s[1], s[2]))
        return jnp.where(live6 & (row_h == lane_h), q, 0.0)

    def by_group(o):
        return [jnp.sum(jnp.where(live6 & (row == lane_h * NSA_GROUP + g), o, 0.0), axis=0, keepdims=True)
                for g in range(NSA_GROUP)]

    def strided(pages, l):
        return jnp.concatenate([pg[pl.ds(l, gpp, stride=CMP_STRIDE), :] for pg in pages], axis=0)

    kraw = _compress_tokens(lambda l: strided(kc_pages, l), n_groups, wabk_ref, babk_ref, w2k_ref)
    kcmp = _head_norm(kraw, gkc_ref[...], _ones_bd())
    vcmp = _compress_tokens(lambda l: strided(vc_pages, l), n_groups, wabv_ref, babv_ref, w2v_ref)

    qc8 = stack_q(qc_ref)
    c_end = lax.broadcasted_iota(jnp.int32, (1, n_groups), 1) * CMP_STRIDE + CMP_LEN
    p_c = _softmax_rows(_dot3_nt(qc8, kcmp) * SCALE, c_end <= past + 1)
    o_cmp = _dot(p_c.astype(BF), vcmp.astype(BF))
    imp8 = _dot_sel(p_c, ovl_ref[...])
    imp = jnp.sum(jnp.where(live6 & (row_h == lane_h), imp8, 0.0), axis=0, keepdims=True)
    jblk = lane & (SEL_BLOCK - 1)
    own = past // SEL_BLOCK
    sel = _rank_select(jnp.broadcast_to(imp, (8, LANES)), jblk, own, own + 1, SEL_TOPN)[0:1]
    sel8 = jnp.where(live6 & (row_h == lane_h), sel, 0.0)

    qr8 = stack_q(qr_ref)
    qr8b = qr8.astype(BF)
    nsa_new = nsa_new_ref[...]
    picked = _dot(sel8.astype(BF), esel_ref[...]) > 0.5
    k_sel = jnp.concatenate([pg[:, 0:LANES] for pg in sel_pages], axis=0).astype(BF)
    v_sel = jnp.concatenate([pg[:, LANES:2 * LANES] for pg in sel_pages], axis=0).astype(BF)
    s = jnp.where(picked, _dot_nt(qr8b, k_sel), NEG)
    own_picked = jnp.sum(jnp.where(jblk == own, sel8, 0.0), axis=1, keepdims=True) > 0.5
    s_new = jnp.where(own_picked, jnp.sum(qr8 * nsa_new[:, 2 * LANES:3 * LANES], axis=1, keepdims=True), NEG)
    m = jnp.maximum(jnp.maximum(jnp.max(s, axis=1, keepdims=True), s_new), M_INIT)
    e = jnp.exp2(s - m)
    e_new = jnp.exp2(s_new - m)
    den = jnp.sum(e, axis=1, keepdims=True) + e_new
    acc = e_new * nsa_new[:, 3 * LANES:4 * LANES] + _dot(e.astype(BF), v_sel)
    o_sel = jnp.where(den > 0.0, acc / den, 0.0)

    win_new = win_new_ref[...]
    n_win = winst_ref.shape[0]
    kpos_w = past - n_win + lax.broadcasted_iota(jnp.int32, (1, n_win), 1)
    ok_w = (past - kpos_w) < WIN
    s = jnp.where(ok_w, _dot_nt(qr8b, winst_ref[:, 0:LANES].astype(BF)), NEG)
    s_new = jnp.sum(qr8 * win_new[:, 0:LANES], axis=1, keepdims=True)
    m = jnp.maximum(jnp.max(s, axis=1, keepdims=True), s_new)
    e = jnp.exp2(s - m)
    e_new = jnp.exp2(s_new - m)
    den = jnp.sum(e, axis=1, keepdims=True) + e_new
    o_win = (_dot(e.astype(BF), winst_ref[:, LANES:2 * LANES].astype(BF)) + e_new * win_new[:, LANES:2 * LANES]) / den

    gate8 = jnp.broadcast_to(gate_ref[...], (8, LANES))
    gh = gate8.astype(BF)
    gl = (gate8 - gh.astype(F32)).astype(BF)
    gexp = (_dot(gh, gexp_ref[...]) + _dot(gl, gexp_ref[...]))[0:1]
    branches = [by_group(o_cmp), by_group(o_sel), by_group(o_win)]
    for g in range(NSA_GROUP):
        out = jnp.zeros((1, LANES), F32)
        for br in range(3):
            out = out + gexp[:, (g * 3 + br) * LANES:(g * 3 + br + 1) * LANES] * branches[br][g]
        onsa_ref[:, g * LANES:(g + 1) * LANES] = out.astype(BF)

    mq = mq_ref[...].astype(F32)
    lane3 = lax.broadcasted_iota(jnp.int32, (1, 384), 1)
    mq8 = jnp.where(row == (lane3 >> 6), mq, 0.0)
    mq8b = mq8.astype(BF)
    n_blk = past // MOBA_BLOCK
    n_blk_pad = -(-n_blk // 16) * 16
    k_m = jnp.concatenate([pg[:, 0:384] for pg in moba_pages], axis=0).astype(BF)
    v_m = jnp.concatenate([pg[:, 384:768] for pg in moba_pages], axis=0).astype(BF)
    km = _dot(emoba_ref[0:n_blk_pad, :], k_m) * (1.0 / MOBA_BLOCK)
    km = jnp.concatenate([km.astype(BF), jnp.zeros((LANES - n_blk_pad, 384), BF)], axis=0)
    gate_m = _dot_nt(mq8b, km)
    cnt = jnp.zeros((8, LANES), F32)
    for d in range(1, n_blk):
        lower = pltpu.roll(gate_m, d, axis=1)
        upper = pltpu.roll(gate_m, LANES - d, axis=1)
        cnt = cnt + jnp.where((lane >= d) & (lower >= gate_m), 1.0, 0.0)
        cnt = cnt + jnp.where((lane + d < n_blk) & (upper > gate_m), 1.0, 0.0)
    pick_m = jnp.where((lane < n_blk) & (cnt < MOBA_TOPK), 1.0, 0.0)
    ok_m = _dot(pick_m.astype(BF), emoba_ref[...]) > 0.5
    moba_new = moba_new_ref[...]
    s = jnp.where(ok_m, _dot_nt(mq8b, k_m), NEG)
    s_new = jnp.sum(mq8 * moba_new[:, 0:384], axis=1, keepdims=True)
    m = jnp.maximum(jnp.max(s, axis=1, keepdims=True), s_new)
    e = jnp.exp2(s - m)
    e_new = jnp.exp2(s_new - m)
    den = jnp.sum(e, axis=1, keepdims=True) + e_new
    o_m = (e_new * moba_new[:, 384:768] + _dot(e.astype(BF), v_m)) / den
    omoba_ref[...] = jnp.sum(jnp.where(row == (lane3 >> 6), o_m, 0.0), axis=0, keepdims=True).astype(BF)

    x = xpool_ref[...]
    hist = poolst_ref[...]
    hrow = lax.broadcasted_iota(jnp.int32, (POOL_HIST, 1), 0)
    sums = {w: x + jnp.sum(jnp.where(hrow >= POOL_HIST - (w - 1), hist, 0.0), axis=0, keepdims=True)
            for w in POOL_WINDOWS}
    y = _pool_mix(sums, x, jnp.full((1, 1), past + 1, jnp.int32), pw_ref, sc_ref)
    ypool_ref[...] = y.astype(BF)


def _sample(page_table, rows, winst, poolst, cache_nsa, cache_moba, cw, consts, pw, sc, layer, past):
    n_seq, n_pages = page_table.shape
    page = cache_nsa.shape[2]
    per_seq = lambda w: pl.BlockSpec((None, 1, w), lambda i, pt: (i, 0, 0))
    lconst = lambda shape: pl.BlockSpec((None,) + tuple(shape), lambda i, pt: (layer,) + (0,) * len(shape),
                                        pipeline_mode=pl.Buffered(1))
    const = lambda a: pl.BlockSpec(a.shape, lambda i, pt: (0,) * a.ndim, pipeline_mode=pl.Buffered(1))
    cmp_specs = [lconst(s) for s in _cmp_weight_shapes()]

    def page_specs(width, lane_block):
        def one(pj):
            return pl.BlockSpec((None, None, page, width), lambda i, pt: (layer, pt[i, pj], 0, lane_block))
        return [one(pj) for pj in range(n_pages)]

    in_specs = ([per_seq(384), per_seq(384), per_seq(LANES), per_seq(384), per_seq(512), per_seq(256), per_seq(768),
                 per_seq(POOL_CH),
                 pl.BlockSpec((None, None, winst.shape[2], 256), lambda i, pt: (layer, i, 0, 0)),
                 pl.BlockSpec((None, None, POOL_HIST, POOL_CH), lambda i, pt: (layer, i, 0, 0))]
                + cmp_specs + cmp_specs + [lconst((1, LANES))]
                + [const(consts["ovl_s"]), const(consts["gexp"]), const(consts["esel"]), const(consts["emoba"]),
                   lconst((POOL_CH, POOL_CH)), lconst((1, POOL_CH))]
                + page_specs(LANES, 0) + page_specs(LANES, 1) + page_specs(2 * LANES, 1) + page_specs(768, 0))
    grid_spec = pltpu.PrefetchScalarGridSpec(
        num_scalar_prefetch=1, grid=(n_seq,), in_specs=in_specs,
        out_specs=[per_seq(384), per_seq(384), per_seq(POOL_CH)])
    r3 = lambda a: a.reshape(n_seq, 1, a.shape[1])
    outs = pl.pallas_call(
        functools.partial(_sample_kernel, n_pages=n_pages, page=page, past=past),
        grid_spec=grid_spec,
        out_shape=[jax.ShapeDtypeStruct((n_seq, 1, 384), BF), jax.ShapeDtypeStruct((n_seq, 1, 384), BF),
                   jax.ShapeDtypeStruct((n_seq, 1, POOL_CH), BF)],
        compiler_params=_cparams("parallel"),
        name="sample_step",
    )(page_table, r3(rows["qc"]), r3(rows["qr"]), r3(rows["gate"]), r3(rows["mq"]), r3(rows["nsa"]), r3(rows["win"]),
      r3(rows["moba"]), r3(rows["xpool"]), winst, poolst,
      cw["wabk"], cw["babk"], cw["w2k"], cw["wabv"], cw["babv"], cw["w2v"], cw["gkc"],
      consts["ovl_s"], consts["gexp"], consts["esel"], consts["emoba"], pw, sc,
      *([cache_nsa] * (3 * n_pages)), *([cache_moba] * n_pages))
    return [o.reshape(n_seq, o.shape[2]) for o in outs]


def _combine_ffn_kernel(x_ref, yp_ref, on_ref, om_ref, w_ref, g_ref, wg_ref, wu_ref, wd_ref, o_ref, *, f_chunk):
    y = _dot(yp_ref[...], w_ref[0:POOL_CH, :]) + _dot(on_ref[...], w_ref[POOL_CH:POOL_CH + 384, :])
    y = y + _dot(om_ref[...], w_ref[POOL_CH + 384:POOL_CH + 768, :])
    o_ref[...] = _ffn_body(x_ref[...] + y, g_ref[...], wg_ref, wu_ref, wd_ref, f_chunk)


def _combine_ffn(x, ypool, onsa, omoba, w_out, g, wg, wu, wd, layer):
    n, d = x.shape
    d_ff = wg.shape[2]
    tm = _row_tile(n)
    row = lambda w: pl.BlockSpec((tm, w), lambda i: (i, 0))
    return pl.pallas_call(
        functools.partial(_combine_ffn_kernel, f_chunk=_f_chunk(d_ff)),
        grid=(n // tm,),
        in_specs=[row(d), row(POOL_CH), row(384), row(384), _const_spec((POOL_CH + 768, d), layer),
                  _const_spec((1, d), layer),
                  _const_spec((d, d_ff), layer), _const_spec((d, d_ff), layer), _const_spec((d_ff, d), layer)],
        out_specs=row(d),
        out_shape=jax.ShapeDtypeStruct((n, d), F32),
        compiler_params=_cparams("parallel"),
        name="combine_ffn",
    )(x, ypool, onsa, omoba, w_out, g, wg, wu, wd)


def _in_col_perm():
    cols = list(range(0, POOL_CH))
    for g in range(NSA_GROUP):
        for h in range(NSA_KV_HEADS):
            base = POOL_CH + (h * NSA_GROUP + g) * HEAD_DIM
            cols += range(base, base + HEAD_DIM)
    nsa_end = POOL_CH + NSA_HEADS * HEAD_DIM
    kv_end = nsa_end + 6 * NSA_KV_HEADS * HEAD_DIM
    cols += range(nsa_end, kv_end)
    cols += range(kv_end + GATE_COLS, kv_end + GATE_COLS + 3 * MOBA_HEADS * HEAD_DIM)
    gate_cols = list(range(kv_end, kv_end + GATE_COLS))
    return np.array(cols, np.int32), np.array(gate_cols, np.int32)


def _out_row_perm():
    rows = list(range(0, POOL_CH))
    for g in range(NSA_GROUP):
        for h in range(NSA_KV_HEADS):
            base = POOL_CH + (h * NSA_GROUP + g) * HEAD_DIM
            rows += range(base, base + HEAD_DIM)
    rows += range(POOL_CH + NSA_HEADS * HEAD_DIM, POOL_CH + NSA_HEADS * HEAD_DIM + MOBA_HEADS * HEAD_DIM)
    return np.array(rows, np.int32)


def _overlap(n_cmp_rows, n_cmp_valid, n_blocks):
    n = np.arange(n_cmp_rows)
    c_start = n * CMP_STRIDE
    c_end = c_start + CMP_LEN
    jb = np.arange(SEL_BLOCK)
    ov = (c_start[:, None] < (jb[None, :] + 1) * SEL_BLOCK) & (c_end[:, None] > jb[None, :] * SEL_BLOCK)
    ov &= (n[:, None] < n_cmp_valid) & (jb[None, :] < n_blocks)
    return ov.astype(np.float32)


def _constants(t, past):
    g_p = t // CMP_STRIDE
    ov = _overlap(g_p, g_p - 1, t // SEL_BLOCK)
    ovl_p = np.zeros((2 * g_p, LANES), np.float32)
    ovl_p[:g_p, :SEL_BLOCK] = ov
    ovl_p[g_p:, SEL_BLOCK:] = ov
    g_s = past // CMP_STRIDE
    ov_s = _overlap(g_s, g_s - 1, past // SEL_BLOCK + 1)
    ovl_s = np.concatenate([ov_s, ov_s], axis=1)
    lane = np.arange(LANES)
    gexp = np.zeros((LANES, 9 * LANES), np.float32)
    for g in range(NSA_GROUP):
        for br in range(3):
            src = (lane // HEAD_DIM) * 9 + g * 3 + br
            gexp[src, (g * 3 + br) * LANES + lane] = 1.0
    kp = np.arange(past)
    esel = ((lane[:, None] % SEL_BLOCK) == (kp[None, :] // SEL_BLOCK)).astype(np.float32)
    emoba = (lane[:, None] == (kp[None, :] // MOBA_BLOCK)).astype(np.float32)
    kt = np.arange(t)
    et_sel = ((kt[:, None] // SEL_BLOCK) == (lane[None, :] % SEL_BLOCK)).astype(np.float32)
    et_moba = ((kt[:, None] // MOBA_BLOCK) == (lane[None, :] % 16)).astype(np.float32)
    as_bf = lambda a: jnp.asarray(a, BF)
    return {"ovlt_p": as_bf(ovl_p.T), "ovl_s": as_bf(ovl_s), "gexp": as_bf(gexp), "esel": as_bf(esel),
            "emoba": as_bf(emoba), "et_sel": as_bf(et_sel), "et_moba": as_bf(et_moba)}


def _rope_tables(pos):
    half = ROPE_DIM // 2
    inv_freq = ROPE_THETA ** (-jnp.arange(half, dtype=F32) / half)
    ang = pos.astype(F32)[:, None] * inv_freq[None, :]
    cos, sin, zero = jnp.cos(ang), jnp.sin(ang), jnp.zeros_like(ang)
    rest = HEAD_DIM - ROPE_DIM
    c = jnp.concatenate([cos, cos, jnp.ones((ang.shape[0], rest), F32)], axis=1)
    a = jnp.concatenate([-sin, zero, jnp.zeros((ang.shape[0], rest), F32)], axis=1)
    b = jnp.concatenate([zero, sin, jnp.zeros((ang.shape[0], rest), F32)], axis=1)
    tile2 = lambda m: jnp.concatenate([m, m], axis=1)
    return tile2(c), tile2(a), tile2(b)


def _cmp_weights(cmp_pos, cmp_w1, cmp_w2, g_nsa_kc):
    bd = lambda w: jnp.concatenate([jnp.concatenate([w, jnp.zeros_like(w)], axis=-1),
                                    jnp.concatenate([jnp.zeros_like(w), w], axis=-1)], axis=-2)
    out = {}
    depth = cmp_pos.shape[0]
    for i, name in enumerate(("k", "v")):
        pos2 = jnp.concatenate([cmp_pos[:, i], cmp_pos[:, i]], axis=-1)
        w1 = bd(cmp_w1[:, i])
        out["w2" + name] = bd(cmp_w2[:, i]).astype(BF)
        halves = [w1[:, s:s + CMP_STRIDE].reshape(depth, CMP_STRIDE * LANES, LANES) for s in (0, CMP_STRIDE)]
        out["wab" + name] = jnp.concatenate(halves, axis=-1).astype(BF)
        bias = [jnp.einsum("xlc,xlce->xe", pos2[:, s:s + CMP_STRIDE], w1[:, s:s + CMP_STRIDE],
                           precision=lax.Precision.HIGHEST) for s in (0, CMP_STRIDE)]
        out["bab" + name] = jnp.concatenate(bias, axis=-1)[:, None, :]
    out["gkc"] = jnp.concatenate([g_nsa_kc, g_nsa_kc], axis=-1)[:, None, :]
    return out


def kernel(x_prompt, x_sample, cache_nsa_kv, cache_moba_kv, state_nsa_win, state_pool, page_table, g_ffa, w_ffa_gate, w_ffa_up, w_ffa_down, g_mix, w_in, w_out, pool_w, pool_scale, g_nsa_q, g_nsa_kc, g_nsa_ks, g_nsa_kw, cmp_pos, cmp_w1, cmp_w2, g_moba_q, g_moba_k, g_ffb, w_ffb_gate, w_ffb_up, w_ffb_down):
    batch, t, d = x_prompt.shape
    n_seq = x_sample.shape[0]
    depth, n_pool, page = cache_nsa_kv.shape[:3]
    n_pages = page_table.shape[1]
    past = n_pages * page
    n_p = batch * t
    n_win = state_nsa_win.shape[2]
    assert x_sample.shape[1] == 1 and t % MOBA_BLOCK == 0 and past % MOBA_BLOCK == 0 and n_win == WIN

    tp = _row_tile(t)
    tq_nsa = 256
    tq_moba = min(2 * MOBA_BLOCK, t)

    col_perm, gate_cols = _in_col_perm()
    w_in_p = jnp.concatenate([w_in[:, :, col_perm], w_in[:, :, gate_cols],
                              jnp.zeros((depth, d, LANES - GATE_COLS), w_in.dtype)], axis=-1).astype(BF)
    w_out_p = w_out[:, _out_row_perm(), :].astype(BF)
    bf = lambda w: w.astype(BF)
    wa = (bf(w_ffa_gate), bf(w_ffa_up), bf(w_ffa_down))
    wb = (bf(w_ffb_gate), bf(w_ffb_up), bf(w_ffb_down))
    tile2 = lambda g: jnp.concatenate([g, g], axis=-1)
    gains = jnp.stack([tile2(g_nsa_q), tile2(g_nsa_ks), tile2(g_nsa_kw), tile2(g_moba_q), tile2(g_moba_k)]
                      + [jnp.ones((depth, LANES), F32)] * 3, axis=1)
    cw = _cmp_weights(cmp_pos, cmp_w1, cmp_w2, g_nsa_kc)
    eye = jnp.eye(len(POOL_WINDOWS), dtype=pool_w.dtype)
    pw_bd = jnp.einsum("lgij,gh->lgihj", pool_w, eye).reshape(depth, POOL_CH, POOL_CH).astype(BF)
    sc = pool_scale[:, None, :]
    consts = _constants(t, past)
    rope_p = _rope_tables(jnp.arange(t, dtype=jnp.int32))
    rope_s = _rope_tables(jnp.full((n_seq,), past, jnp.int32))
    g_a, g_m, g_b = g_ffa[:, None, :], g_mix[:, None, :], g_ffb[:, None, :]

    cache_nsa = cache_nsa_kv.reshape(depth, n_pool, page, 512)
    cache_moba = cache_moba_kv.reshape(depth, n_pool, page, 768)
    winst = state_nsa_win.reshape(depth, n_seq, n_win, 256)

    xp = x_prompt.reshape(n_p, d)
    xs = x_sample.reshape(n_seq, d)
    names = ("xpool", "qc", "qr", "nsa", "nsab", "win", "winb", "mq", "moba", "mobab", "gate")
    outs = [[] for _ in range(8)]
    for l in range(depth):
        xp = _ffn(xp, g_a, *wa, l)
        xs = _ffn(xs, g_a, *wa, l)
        pp = dict(zip(names, _prep(xp, g_m, w_in_p, gains, rope_p, l)))
        ps = dict(zip(names, _prep(xs, g_m, w_in_p, gains, rope_s, l)))

        kcmp, vcmp = _compress_prompt(pp["nsa"], cw, l, batch, t)
        onsa_p = _nsa_prompt(pp["qc"], pp["qr"], pp["gate"], kcmp, vcmp, pp["nsab"], pp["winb"], consts["ovlt_p"],
                             consts["gexp"], consts["et_sel"], batch, t, tq_nsa)
        omoba_p = _moba_prompt(pp["mq"], pp["mobab"], consts["et_moba"], batch, t, tq_moba)
        ypool_p = _pool_prompt(pp["xpool"], pw_bd, sc, l, batch, t, tp)
        onsa_s, omoba_s, ypool_s = _sample(page_table, ps, winst, state_pool, cache_nsa, cache_moba, cw, consts,
                                           pw_bd, sc, l, past)

        xp = _combine_ffn(xp, ypool_p, onsa_p, omoba_p, w_out_p, g_b, *wb, l)
        xs = _combine_ffn(xs, ypool_s, onsa_s, omoba_s, w_out_p, g_b, *wb, l)

        outs[0].append(pp["nsa"].reshape(batch, t, 4, NSA_KV_HEADS, HEAD_DIM))
        outs[1].append(ps["nsa"].reshape(n_seq, 1, 4, NSA_KV_HEADS, HEAD_DIM))
        outs[2].append(pp["moba"].reshape(batch, t, 2, MOBA_HEADS, HEAD_DIM))
        outs[3].append(ps["moba"].reshape(n_seq, 1, 2, MOBA_HEADS, HEAD_DIM))
        outs[4].append(pp["win"].reshape(batch, t, 2, NSA_KV_HEADS, HEAD_DIM)[:, t - min(WIN, t):])
        outs[5].append(ps["win"].reshape(n_seq, 1, 2, NSA_KV_HEADS, HEAD_DIM))
        outs[6].append(pp["xpool"].reshape(batch, t, POOL_CH)[:, t - POOL_HIST:])
        outs[7].append(ps["xpool"].reshape(n_seq, 1, POOL_CH))

    stacked = [jnp.stack(o, axis=0) for o in outs]
    stacked[5] = jnp.concatenate([state_nsa_win[:, :, 1:], stacked[5]], axis=2)
    stacked[7] = jnp.concatenate([state_pool[:, :, 1:], stacked[7]], axis=2)
    return (xp.reshape(batch, t, d), xs.reshape(n_seq, 1, d), *stacked)
```

```python
import functools

import numpy as np
import jax
import jax.numpy as jnp
from jax import lax
from jax.experimental import pallas as pl
from jax.experimental.pallas import tpu as pltpu

F32 = jnp.float32
BF = jnp.bfloat16

HEAD_DIM = 64
LANES = 128
POOL_CH = 256
POOL_WINDOWS = (2, 4, 8, 16)
POOL_HIST = 15
NSA_KV_HEADS = 2
NSA_GROUP = 3
NSA_HEADS = 6
MOBA_HEADS = 6
CMP_LEN = 32
CMP_STRIDE = 16
SEL_BLOCK = 64
SEL_TOPN = 16
WIN = 512
MOBA_BLOCK = 256
MOBA_TOPK = 3
ROPE_THETA = 500000.0
ROPE_DIM = 16
EPS = 1e-6
NEG = -1e30
M_INIT = -1e29
SEL_FORCE = 1e4
SCALE = HEAD_DIM ** -0.5
Q_PRESCALE = SCALE * 1.4426950408889634
GATE_COLS = 3 * NSA_HEADS
D_QKV = 2560
D_IN_PAD = D_QKV + LANES
VMEM_LIMIT = 56 * 1024 * 1024


def _cparams(*sem):
    return pltpu.CompilerParams(dimension_semantics=sem, vmem_limit_bytes=VMEM_LIMIT)


def _dot(a, b):
    return jnp.dot(a, b, preferred_element_type=F32)


def _dot_nt(a, b):
    return lax.dot_general(a, b, (((1,), (1,)), ((), ())), preferred_element_type=F32)


def _split3(x):
    hi = x.astype(BF)
    r = x - hi.astype(F32)
    mid = r.astype(BF)
    lo = (r - mid.astype(F32)).astype(BF)
    return hi, mid, lo


def _dot_sel(x, m):
    hi, mid, lo = _split3(x)
    return _dot(hi, m) + _dot(mid, m) + _dot(lo, m)


def _dot3_nt(a, b):
    ah = a.astype(BF)
    al = (a - ah.astype(F32)).astype(BF)
    bh = b.astype(BF)
    bl = (b - bh.astype(F32)).astype(BF)
    return _dot_nt(ah, bh) + _dot_nt(ah, bl) + _dot_nt(al, bh)


def _sigmoid(x):
    return 1.0 / (1.0 + jnp.exp(-x))


def _silu(x):
    return x * _sigmoid(x)


def _rms_rows(x, g):
    return x * lax.rsqrt(jnp.mean(x * x, axis=-1, keepdims=True) + EPS) * g


def _ones_bd():
    r = lax.broadcasted_iota(jnp.int32, (LANES, LANES), 0) >> 6
    c = lax.broadcasted_iota(jnp.int32, (LANES, LANES), 1) >> 6
    return jnp.where(r == c, 1.0, 0.0).astype(BF)


def _head_norm(x, gain, ones_bd):
    ms = _dot_sel(x * x, ones_bd) * (1.0 / HEAD_DIM)
    return x * lax.rsqrt(ms + EPS) * gain


def _rope(x, c, a, b):
    return x * c + pltpu.roll(x, LANES - ROPE_DIM // 2, axis=1) * a + pltpu.roll(x, ROPE_DIM // 2, axis=1) * b


def _softmax_rows(s, valid):
    s = jnp.where(valid, s, NEG)
    m = jnp.max(s, axis=1, keepdims=True)
    e = jnp.where(valid, jnp.exp(s - m), 0.0)
    den = jnp.sum(e, axis=1, keepdims=True)
    return jnp.where(den > 0.0, e / den, 0.0)


def _sum_list(xs):
    return functools.reduce(lambda a, b: a + b, xs)


def _ffn_body(x, g, wg_ref, wu_ref, wd_ref, f_chunk):
    xn = _rms_rows(x, g).astype(BF)
    d_ff = wg_ref.shape[1]
    acc = jnp.zeros(x.shape, F32)
    for c in range(d_ff // f_chunk):
        sl = slice(c * f_chunk, (c + 1) * f_chunk)
        gate = _dot(xn, wg_ref[:, sl])
        up = _dot(xn, wu_ref[:, sl])
        h = (_silu(gate) * up).astype(BF)
        acc = acc + _dot(h, wd_ref[sl, :])
    return x + 0.5 * acc


def _ffn_kernel(x_ref, g_ref, wg_ref, wu_ref, wd_ref, o_ref, *, f_chunk):
    o_ref[...] = _ffn_body(x_ref[...], g_ref[...], wg_ref, wu_ref, wd_ref, f_chunk)


def _f_chunk(d_ff):
    for c in (1408, 1024, 512, 256, 128):
        if d_ff % c == 0:
            return c
    return d_ff


def _row_tile(n):
    return next((c for c in (512, 384, 256, 128) if n % c == 0), n)


def _const_spec(shape, layer):
    nd = len(shape)
    return pl.BlockSpec((None,) + tuple(shape), lambda *_: (layer,) + (0,) * nd, pipeline_mode=pl.Buffered(1))


def _ffn(x, g, wg, wu, wd, layer):
    n, d = x.shape
    d_ff = wg.shape[2]
    tm = _row_tile(n)
    return pl.pallas_call(
        functools.partial(_ffn_kernel, f_chunk=_f_chunk(d_ff)),
        grid=(n // tm,),
        in_specs=[pl.BlockSpec((tm, d), lambda i: (i, 0)),
                  _const_spec((1, d), layer),
                  _const_spec((d, d_ff), layer), _const_spec((d, d_ff), layer), _const_spec((d_ff, d), layer)],
        out_specs=pl.BlockSpec((tm, d), lambda i: (i, 0)),
        out_shape=jax.ShapeDtypeStruct((n, d), F32),
        compiler_params=_cparams("parallel"),
        name="ffn",
    )(x, g, wg, wu, wd)


def _prep_kernel(x_ref, g_ref, w_ref, gains_ref, c_ref, a_ref, b_ref,
                 xpool_ref, qc_ref, qr_ref, nsa_ref, nsab_ref, win_ref, winb_ref,
                 mq_ref, moba_ref, mobab_ref, gate_ref):
    xn = _rms_rows(x_ref[...], g_ref[...]).astype(BF)
    z = _dot(xn, w_ref[...])
    c, a, b = c_ref[...], a_ref[...], b_ref[...]
    gains = gains_ref[...]
    slab = lambda i: z[:, i * LANES:(i + 1) * LANES]

    normed_gain = {2: 0, 3: 0, 4: 0, 7: 1, 9: 2, 11: 3, 12: 3, 13: 3, 14: 4, 15: 4, 16: 4}
    r2 = lax.broadcasted_iota(jnp.int32, (2 * LANES, 2 * LANES), 0) >> 6
    c2 = lax.broadcasted_iota(jnp.int32, (2 * LANES, 2 * LANES), 1) >> 6
    ones_bd2 = jnp.where(r2 == c2, 1.0, 0.0).astype(BF)
    idxs = list(normed_gain)
    normed = {}
    for p0 in range(0, len(idxs), 2):
        pair = idxs[p0:p0 + 2]
        sq = jnp.concatenate([slab(i) * slab(i) for i in pair], axis=1)
        hi = sq.astype(BF)
        lo = (sq - hi.astype(F32)).astype(BF)
        w = ones_bd2 if len(pair) == 2 else ones_bd2[0:LANES, 0:LANES]
        ms = (_dot(hi, w) + _dot(lo, w)) * (1.0 / HEAD_DIM)
        for k, i in enumerate(pair):
            gi = normed_gain[i]
            normed[i] = slab(i) * lax.rsqrt(ms[:, k * LANES:(k + 1) * LANES] + EPS) * gains[gi:gi + 1]
    norm_rope = lambda i: _rope(normed[i], c, a, b)

    xpool_ref[...] = z[:, 0:POOL_CH]
    for gi in range(3):
        qc_ref[:, gi * LANES:(gi + 1) * LANES] = normed[2 + gi]
        qr_ref[:, gi * LANES:(gi + 1) * LANES] = (norm_rope(2 + gi) * Q_PRESCALE).astype(BF)
    nsa = [slab(5), slab(6), norm_rope(7), slab(8)]
    for i, s in enumerate(nsa):
        nsa_ref[:, i * LANES:(i + 1) * LANES] = s
        nsab_ref[:, i * LANES:(i + 1) * LANES] = s.astype(BF)
    win = [norm_rope(9), slab(10)]
    for i, s in enumerate(win):
        win_ref[:, i * LANES:(i + 1) * LANES] = s
        winb_ref[:, i * LANES:(i + 1) * LANES] = s.astype(BF)
    for i in range(3):
        mq_ref[:, i * LANES:(i + 1) * LANES] = (norm_rope(11 + i) * Q_PRESCALE).astype(BF)
    for i in range(6):
        s = norm_rope(14 + i) if i < 3 else slab(14 + i)
        moba_ref[:, i * LANES:(i + 1) * LANES] = s
        mobab_ref[:, i * LANES:(i + 1) * LANES] = s.astype(BF)
    gate_ref[...] = _sigmoid(slab(20))


def _prep(x, g, w_in, gains, rope, layer):
    n, d = x.shape
    tm = _row_tile(min(n, rope[0].shape[0]))
    nb = rope[0].shape[0] // tm
    row = lambda w: pl.BlockSpec((tm, w), lambda i: (i, 0))
    tab = pl.BlockSpec((tm, LANES), lambda i: (i % nb, 0))
    widths_dtypes = [(POOL_CH, F32), (384, F32), (384, BF), (512, F32), (512, BF), (256, F32), (256, BF),
                     (384, BF), (768, F32), (768, BF), (LANES, F32)]
    return pl.pallas_call(
        _prep_kernel,
        grid=(n // tm,),
        in_specs=[row(d), _const_spec((1, d), layer), _const_spec((d, D_IN_PAD), layer),
                  _const_spec((8, LANES), layer), tab, tab, tab],
        out_specs=[row(w) for w, _ in widths_dtypes],
        out_shape=[jax.ShapeDtypeStruct((n, w), dt) for w, dt in widths_dtypes],
        compiler_params=_cparams("parallel"),
        name="prep",
    )(x, g, w_in, gains, *rope)


def _compress_tokens(load_rows, n_groups, wab_ref, bab_ref, w2_ref):
    x = jnp.concatenate([load_rows(l) for l in range(CMP_STRIDE)], axis=1).astype(BF)
    pab = _dot(x, wab_ref[...]) + bab_ref[...]
    pre = pab[:, 0:LANES] + pltpu.roll(pab[:, LANES:2 * LANES], n_groups - 1, axis=0)
    return _dot(_silu(pre).astype(BF), w2_ref[...])


def _compress_prompt_kernel(kc_ref, vc_ref, wabk_ref, babk_ref, w2k_ref, wabv_ref, babv_ref, w2v_ref, gkc_ref,
                            kcmp_ref, vcmp_ref, *, n_groups):
    kraw = _compress_tokens(lambda l: kc_ref[pl.ds(l, n_groups, stride=CMP_STRIDE), :],
                            n_groups, wabk_ref, babk_ref, w2k_ref)
    kcmp_ref[...] = _head_norm(kraw, gkc_ref[...], _ones_bd())
    vcmp_ref[...] = _compress_tokens(lambda l: vc_ref[pl.ds(l, n_groups, stride=CMP_STRIDE), :],
                                     n_groups, wabv_ref, babv_ref, w2v_ref)


def _cmp_weight_shapes():
    return [(CMP_STRIDE * LANES, 2 * LANES), (1, 2 * LANES), (LANES, LANES)]


def _compress_prompt(nsa_rows, cw, layer, batch, t):
    n_groups = t // CMP_STRIDE
    wspecs = [_const_spec(s, layer) for s in _cmp_weight_shapes()]
    return pl.pallas_call(
        functools.partial(_compress_prompt_kernel, n_groups=n_groups),
        grid=(batch,),
        in_specs=[pl.BlockSpec((t, LANES), lambda b: (b, 0)), pl.BlockSpec((t, LANES), lambda b: (b, 1))]
                 + wspecs + wspecs + [_const_spec((1, LANES), layer)],
        out_specs=[pl.BlockSpec((n_groups, LANES), lambda b: (b, 0))] * 2,
        out_shape=[jax.ShapeDtypeStruct((batch * n_groups, LANES), F32)] * 2,
        compiler_params=_cparams("parallel"),
        name="compress_prompt",
    )(nsa_rows, nsa_rows, cw["wabk"], cw["babk"], cw["w2k"], cw["wabv"], cw["babv"], cw["w2v"], cw["gkc"])


def _rank_select(imp, j, own, n_blocks, top_n):
    forced = (j == 0) | (j == own) | (j == own - 1)
    imp = jnp.where(forced, imp + SEL_FORCE, imp)
    imp = jnp.where(j <= own, imp, -jnp.inf)
    cnt = jnp.zeros(imp.shape, F32)
    for d in range(1, SEL_BLOCK):
        if d > n_blocks - 1 and d < SEL_BLOCK - (n_blocks - 1):
            continue
        nowrap = j >= d
        r = jnp.where(nowrap, pltpu.roll(imp, d, axis=1), pltpu.roll(imp, (d - SEL_BLOCK) % LANES, axis=1))
        cnt = cnt + jnp.where(nowrap, jnp.where(r >= imp, 1.0, 0.0), jnp.where(r > imp, 1.0, 0.0))
    return jnp.where(j <= own, jnp.where(cnt < top_n, 1.0, 0.0), 0.0)


def _online_update(s, v, m_ref, l_ref, acc_ref, idx):
    m_prev = m_ref[idx]
    m_new = jnp.maximum(m_prev, jnp.max(s, axis=1, keepdims=True))
    alpha = jnp.exp2(m_prev - m_new)
    ps = [jnp.exp2(s[:, i * LANES:(i + 1) * LANES] - m_new) for i in range(s.shape[1] // LANES)]
    l_ref[idx] = alpha * l_ref[idx] + _sum_list(ps)
    acc_ref[idx] = alpha * acc_ref[idx] + _dot(jnp.concatenate(ps, axis=1).astype(BF), v)
    m_ref[idx] = m_new


def _online_init(m_ref, l_ref, acc_ref):
    m_ref[...] = jnp.full(m_ref.shape, M_INIT, F32)
    l_ref[...] = jnp.zeros(l_ref.shape, F32)
    acc_ref[...] = jnp.zeros(acc_ref.shape, F32)


def _online_out(l_ref, acc_ref, idx):
    l = jnp.sum(l_ref[idx], axis=1, keepdims=True)
    return jnp.where(l > 0.0, acc_ref[idx] / l, 0.0)


def _nsa_prompt_kernel(qc_ref, qr_ref, gate_ref, kcmp_ref, vcmp_ref, nsab_ref, winb_ref, ovlt_ref, gexp_ref, et_ref,
                       o_ref, impt_ref, m_ref, l_ref, acc_ref, *, tq, tk):
    qi = pl.program_id(1)
    q0 = qi * tq
    lane_h = lax.broadcasted_iota(jnp.int32, (1, LANES), 1) >> 6
    qpos = q0 + lax.broadcasted_iota(jnp.int32, (tq, 1), 0)
    qpos6 = jnp.concatenate([qpos] * NSA_HEADS, axis=0)

    def stack6(tile, zero):
        return jnp.concatenate([jnp.where(lane_h == h, tile(g), zero)
                                for h in range(NSA_KV_HEADS) for g in range(NSA_GROUP)], axis=0)

    kc = kcmp_ref[...]
    n_cmp = kc.shape[0]
    c_end = lax.broadcasted_iota(jnp.int32, (1, n_cmp), 1) * CMP_STRIDE + CMP_LEN
    q_cmp = stack6(lambda g: qc_ref[:, g * LANES:(g + 1) * LANES], 0.0)
    p_c = _softmax_rows(_dot3_nt(q_cmp, kc) * SCALE, c_end <= qpos6 + 1)
    o_cmp = _dot(p_c.astype(BF), vcmp_ref[...].astype(BF))
    psum = [_sum_list([p_c[(h * NSA_GROUP + g) * tq:(h * NSA_GROUP + g + 1) * tq] for g in range(NSA_GROUP)])
            for h in range(NSA_KV_HEADS)]
    hi, mid, lo = _split3(jnp.concatenate(psum, axis=1))
    ovlt = ovlt_ref[...]
    impt = _dot_nt(ovlt, hi) + _dot_nt(ovlt, mid) + _dot_nt(ovlt, lo)
    brow = lax.broadcasted_iota(jnp.int32, (LANES, 1), 0)
    jrow = brow & (SEL_BLOCK - 1)
    own_l = (q0 + lax.broadcasted_iota(jnp.int32, (1, tq), 1)) >> 6
    forced = (jrow == 0) | (jrow == own_l) | (jrow == own_l - 1)
    impt = jnp.where(forced, impt + SEL_FORCE, impt)
    impt = jnp.where(jrow <= own_l, impt, -jnp.inf)
    impt_ref[...] = impt

    def beats(jp):
        comp = jnp.where(brow < SEL_BLOCK, impt_ref[pl.ds(jp, 1), :], impt_ref[pl.ds(SEL_BLOCK + jp, 1), :])
        return jnp.where(jrow > jp, jnp.where(comp >= impt, 1.0, 0.0), jnp.where(comp > impt, 1.0, 0.0))

    n_live = (q0 + tq - 1) // SEL_BLOCK + 1
    cnt = lax.fori_loop(0, (n_live + 1) // 2, lambda i, cnt: cnt + beats(2 * i) + beats(2 * i + 1),
                        jnp.zeros((LANES, tq), F32))
    sel_t = jnp.where(jrow <= own_l, jnp.where(cnt < SEL_TOPN, 1.0, 0.0), 0.0)
    sel_neg = jnp.where(sel_t.T > 0.5, 0.0, NEG)

    zero = jnp.zeros((), BF)
    q_rot = stack6(lambda g: qr_ref[:, g * LANES:(g + 1) * LANES], zero)
    q_aug = jnp.concatenate([q_rot, stack6(lambda g: sel_neg, 0.0).astype(BF)], axis=1)
    _online_init(m_ref, l_ref, acc_ref)

    def sel_chunk(c, causal):
        start = pl.multiple_of(c * tk, tk)
        k_aug = jnp.concatenate([nsab_ref[pl.ds(start, tk), 2 * LANES:3 * LANES], et_ref[pl.ds(start, tk), :]], axis=1)
        v = nsab_ref[pl.ds(start, tk), 3 * LANES:4 * LANES]
        s = _dot_nt(q_aug, k_aug)
        if causal:
            kpos = start + lax.broadcasted_iota(jnp.int32, (1, tk), 1)
            s = jnp.where(kpos <= qpos6, s, NEG)
        _online_update(s, v, m_ref, l_ref, acc_ref, 0)

    c_last = (q0 + tq - 1) // tk
    lax.fori_loop(0, c_last, lambda c, carry: (sel_chunk(c, False), carry)[1], 0)
    sel_chunk(c_last, True)
    o_sel = _online_out(l_ref, acc_ref, 0)

    n_slots = WIN // tq + 1
    slot_c = [qi - (n_slots - 1) + slot for slot in range(n_slots)]
    starts = [pl.multiple_of(jnp.maximum(c, 0) * tq, tq) for c in slot_c]
    k_band = jnp.concatenate([winb_ref[pl.ds(st, tq), 0:LANES] for st in starts], axis=0)
    v_band = jnp.concatenate([winb_ref[pl.ds(st, tq), LANES:2 * LANES] for st in starts], axis=0)
    s_band = _dot_nt(q_rot, k_band)
    s_w = []
    for slot in range(n_slots):
        s = s_band[:, slot * tq:(slot + 1) * tq]
        kpos = starts[slot] + lax.broadcasted_iota(jnp.int32, (1, tq), 1)
        if slot == 0:
            s = jnp.where(kpos > qpos6 - WIN, s, NEG)
        if slot == n_slots - 1:
            s = jnp.where(kpos <= qpos6, s, NEG)
        else:
            s = jnp.where(slot_c[slot] >= 0, s, NEG)
        s_w.append(s)
    m = jnp.max(functools.reduce(jnp.maximum, s_w), axis=1, keepdims=True)
    ps = [jnp.exp2(s - m) for s in s_w]
    l = jnp.sum(_sum_list(ps), axis=1, keepdims=True)
    o_win = _dot(jnp.concatenate(ps, axis=1).astype(BF), v_band) / l

    gate = gate_ref[...]
    gh = gate.astype(BF)
    gl = (gate - gh.astype(F32)).astype(BF)
    gexp = _dot(gh, gexp_ref[...]) + _dot(gl, gexp_ref[...])
    for g in range(NSA_GROUP):
        r0, r1 = g * tq, (NSA_GROUP + g) * tq
        out = jnp.zeros((tq, LANES), F32)
        for br, o in enumerate((o_cmp, o_sel, o_win)):
            both = jnp.where(lane_h == 0, o[r0:r0 + tq], o[r1:r1 + tq])
            out = out + gexp[:, (g * 3 + br) * LANES:(g * 3 + br + 1) * LANES] * both
        o_ref[:, g * LANES:(g + 1) * LANES] = out.astype(BF)


def _nsa_prompt(qc, qr, gate, kcmp, vcmp, nsab, winb, ovlt, gexp, et, batch, t, tq):
    nq = t // tq
    n_groups = t // CMP_STRIDE
    assert WIN % tq == 0 and t // SEL_BLOCK <= SEL_BLOCK
    tile = lambda w: pl.BlockSpec((tq, w), lambda b, i: (b * nq + i, 0))
    per_b = lambda r, w: pl.BlockSpec((r, w), lambda b, i: (b, 0))
    const = lambda a: pl.BlockSpec(a.shape, lambda b, i: (0, 0), pipeline_mode=pl.Buffered(1))
    return pl.pallas_call(
        functools.partial(_nsa_prompt_kernel, tq=tq, tk=min(512, t)),
        grid=(batch, nq),
        in_specs=[tile(384), tile(384), tile(LANES), per_b(n_groups, LANES), per_b(n_groups, LANES),
                  per_b(t, 512), per_b(t, 256), const(ovlt), const(gexp), const(et)],
        out_specs=tile(384),
        out_shape=jax.ShapeDtypeStruct((batch * t, 384), BF),
        scratch_shapes=[pltpu.VMEM((LANES, tq), F32)] + [pltpu.VMEM((1, NSA_HEADS * tq, LANES), F32)] * 3,
        compiler_params=_cparams("parallel", "parallel"),
        name="nsa_prompt",
    )(qc, qr, gate, kcmp, vcmp, nsab, winb, ovlt, gexp, et)


def _moba_prompt_kernel(mq_ref, mobab_ref, et_ref, o_ref, kmbd_ref, gate_ref, m_ref, l_ref, acc_ref, *, tq, t, tk):
    qi = pl.program_id(1)
    q0 = qi * tq
    n_pairs = MOBA_HEADS // 2
    lane = lax.broadcasted_iota(jnp.int32, (1, LANES), 1)
    lane_h = lane >> 6
    hh = lane >> 4
    qpos = q0 + lax.broadcasted_iota(jnp.int32, (tq, 1), 0)
    qpos2 = jnp.concatenate([qpos, qpos], axis=0)
    brow = lax.broadcasted_iota(jnp.int32, (LANES, 1), 0)
    jrow = brow & 15

    @pl.when(qi == 0)
    def _():
        tt = lax.broadcasted_iota(jnp.int32, (1, t), 1) >> 8
        ind = jnp.where(jrow == tt, 1.0, 0.0).astype(BF)
        km = _dot(ind, mobab_ref[:, 0:384]) * (1.0 / MOBA_BLOCK)
        col_h = lax.broadcasted_iota(jnp.int32, (1, 384), 1) >> 6
        kmbd_ref[...] = jnp.where(col_h == (brow >> 4), km, 0.0).astype(BF)

    gate = _dot_nt(kmbd_ref[...], mq_ref[...])
    gate_ref[...] = gate
    own_l = (q0 + lax.broadcasted_iota(jnp.int32, (1, tq), 1)) >> 8

    def rank_step(jp, cnt):
        comp = jnp.concatenate([jnp.broadcast_to(gate_ref[pl.ds(hd * 16 + jp, 1), :], (16, tq))
                                for hd in range(LANES // 16)], axis=0)
        beats = jnp.where(jrow > jp, jnp.where(comp >= gate, 1.0, 0.0), jnp.where(comp > gate, 1.0, 0.0))
        return cnt + jnp.where(jp < own_l, beats, 0.0)

    n_past = (q0 + tq - 1) // MOBA_BLOCK
    cnt = lax.fori_loop(0, n_past, rank_step, jnp.zeros((LANES, tq), F32))
    picked_t = jnp.where(jrow == own_l, 1.0, jnp.where(jrow < own_l, jnp.where(cnt < MOBA_TOPK, 1.0, 0.0), 0.0))
    pick_neg = jnp.where(picked_t.T > 0.5, 0.0, NEG)

    zero = jnp.zeros((), BF)
    q_aug = []
    for p in range(n_pairs):
        qp = mq_ref[:, p * LANES:(p + 1) * LANES]
        q2 = jnp.concatenate([jnp.where(lane_h == 0, qp, zero), jnp.where(lane_h == 1, qp, zero)], axis=0)
        pk = jnp.concatenate([jnp.where(hh == 2 * p, pick_neg, 0.0), jnp.where(hh == 2 * p + 1, pick_neg, 0.0)], axis=0)
        q_aug.append(jnp.concatenate([q2, pk.astype(BF)], axis=1))
    _online_init(m_ref, l_ref, acc_ref)

    def chunk(c, causal):
        start = pl.multiple_of(c * tk, tk)
        et = et_ref[pl.ds(start, tk), :]
        for p in range(n_pairs):
            k_aug = jnp.concatenate([mobab_ref[pl.ds(start, tk), p * LANES:(p + 1) * LANES], et], axis=1)
            v = mobab_ref[pl.ds(start, tk), 384 + p * LANES:384 + (p + 1) * LANES]
            s = _dot_nt(q_aug[p], k_aug)
            if causal:
                kpos = start + lax.broadcasted_iota(jnp.int32, (1, tk), 1)
                s = jnp.where(kpos <= qpos2, s, NEG)
            _online_update(s, v, m_ref, l_ref, acc_ref, p)

    c_last = (q0 + tq - 1) // tk
    lax.fori_loop(0, c_last, lambda c, carry: (chunk(c, False), carry)[1], 0)
    chunk(c_last, True)
    for p in range(n_pairs):
        out = _online_out(l_ref, acc_ref, p)
        o_ref[:, p * LANES:(p + 1) * LANES] = jnp.where(lane_h == 0, out[0:tq], out[tq:2 * tq]).astype(BF)


def _moba_prompt(mq, mobab, et, batch, t, tq):
    nq = t // tq
    assert t // MOBA_BLOCK <= 16
    tile = lambda w: pl.BlockSpec((tq, w), lambda b, i: (b * nq + i, 0))
    return pl.pallas_call(
        functools.partial(_moba_prompt_kernel, tq=tq, t=t, tk=min(512, t)),
        grid=(batch, nq),
        in_specs=[tile(384), pl.BlockSpec((t, 768), lambda b, i: (b, 0)),
                  pl.BlockSpec(et.shape, lambda b, i: (0, 0), pipeline_mode=pl.Buffered(1))],
        out_specs=tile(384),
        out_shape=jax.ShapeDtypeStruct((batch * t, 384), BF),
        scratch_shapes=[pltpu.VMEM((LANES, 384), BF), pltpu.VMEM((LANES, tq), F32)]
                       + [pltpu.VMEM((MOBA_HEADS // 2, 2 * tq, LANES), F32)] * 3,
        compiler_params=_cparams("parallel", "arbitrary"),
        name="moba_prompt",
    )(mq, mobab, et)


def _pool_mix(win_sums, x, count_pos, pw_ref, sc_ref):
    lane_g = lax.broadcasted_iota(jnp.int32, (1, POOL_CH), 1) >> 6
    s = win_sums[POOL_WINDOWS[-1]]
    w = jnp.full((1, POOL_CH), POOL_WINDOWS[-1], jnp.int32)
    for gi in range(len(POOL_WINDOWS) - 2, -1, -1):
        s = jnp.where(lane_g == gi, win_sums[POOL_WINDOWS[gi]], s)
        w = jnp.where(lane_g == gi, POOL_WINDOWS[gi], w)
    count = jnp.minimum(count_pos, w).astype(F32)
    d = s / count - x
    return _dot(d.astype(BF), pw_ref[...]) * sc_ref[...]


def _pool_prompt_kernel(x_ref, halo_ref, pw_ref, sc_ref, o_ref, buf_ref, *, tp):
    i = pl.program_id(1)
    x = x_ref[...]
    buf_ref[0:16, :] = jnp.where(i > 0, halo_ref[...], 0.0)
    buf_ref[16:16 + tp, :] = x
    acc = x
    sums = {}
    for k in range(1, POOL_WINDOWS[-1]):
        acc = acc + buf_ref[16 - k:16 - k + tp, :]
        if k + 1 in POOL_WINDOWS:
            sums[k + 1] = acc
    pos1 = i * tp + lax.broadcasted_iota(jnp.int32, (tp, 1), 0) + 1
    o_ref[...] = _pool_mix(sums, x, pos1, pw_ref, sc_ref).astype(BF)


def _pool_prompt(xpool, pw, sc, layer, batch, t, tp):
    nt = t // tp
    return pl.pallas_call(
        functools.partial(_pool_prompt_kernel, tp=tp),
        grid=(batch, nt),
        in_specs=[pl.BlockSpec((tp, POOL_CH), lambda b, i: (b * nt + i, 0)),
                  pl.BlockSpec((16, POOL_CH), lambda b, i: (jnp.maximum((b * nt + i) * (tp // 16) - 1, 0), 0)),
                  _const_spec((POOL_CH, POOL_CH), layer), _const_spec((1, POOL_CH), layer)],
        out_specs=pl.BlockSpec((tp, POOL_CH), lambda b, i: (b * nt + i, 0)),
        out_shape=jax.ShapeDtypeStruct((batch * t, POOL_CH), BF),
        scratch_shapes=[pltpu.VMEM((tp + 16, POOL_CH), F32)],
        compiler_params=_cparams("parallel", "parallel"),
        name="pool_prompt",
    )(xpool, xpool, pw, sc)


SAMPLE_ROW_FIELDS = (("qc", 384), ("qr", 384), ("gate", LANES), ("mq", 384), ("nsa", 512), ("win", 256),
                     ("moba", 768), ("xpool", POOL_CH))


def _sample_kernel(pt_ref, rows_ref, winst_ref, poolst_ref,
                   wabk_ref, babk_ref, w2k_ref, wabv_ref, babv_ref, w2v_ref, gkc_ref,
                   ovl_ref, gexp_ref, esel_ref, emoba_ref, pw_ref, sc_ref, *rest, n_pages, page, past):
    nsa_pages = rest[:n_pages]
    moba_pages = rest[n_pages:2 * n_pages]
    onsa_ref, omoba_ref, ypool_ref, cmp_ref = rest[2 * n_pages:]
    del pt_ref
    fields, lane0 = {}, 0
    for name, width in SAMPLE_ROW_FIELDS:
        fields[name] = rows_ref[:, lane0:lane0 + width]
        lane0 += width
    row = lax.broadcasted_iota(jnp.int32, (8, 1), 0)
    lane = lax.broadcasted_iota(jnp.int32, (1, LANES), 1)
    lane_h = lane >> 6
    row_h = jnp.where(row >= NSA_GROUP, 1, 0) + jnp.where(row >= 2 * NSA_GROUP, 1, 0)
    row_g = row - row_h * NSA_GROUP
    live6 = row < NSA_HEADS
    n_groups = past // CMP_STRIDE

    def stack_q(q384):
        s = [q384[:, g * LANES:(g + 1) * LANES] for g in range(NSA_GROUP)]
        q = jnp.where(row_g == 0, s[0], jnp.where(row_g == 1, s[1], s[2]))
        return jnp.where(live6 & (row_h == lane_h), q, 0.0)

    def by_group(o):
        return [jnp.sum(jnp.where(live6 & (row == lane_h * NSA_GROUP + g), o, 0.0), axis=0, keepdims=True)
                for g in range(NSA_GROUP)]

    mq = fields["mq"]
    lane3 = lax.broadcasted_iota(jnp.int32, (1, 384), 1)
    mq8 = jnp.where(row == (lane3 >> 6), mq, 0.0)
    mq8b = mq8.astype(BF)
    n_blk = past // MOBA_BLOCK
    n_blk_pad = -(-n_blk // 16) * 16
    k_m = jnp.concatenate([pg[:, 0:384] for pg in moba_pages], axis=0).astype(BF)
    v_m = jnp.concatenate([pg[:, 384:768] for pg in moba_pages], axis=0).astype(BF)
    km = _dot(emoba_ref[0:n_blk_pad, :], k_m) * (1.0 / MOBA_BLOCK)
    km = jnp.concatenate([km.astype(BF), jnp.zeros((LANES - n_blk_pad, 384), BF)], axis=0)
    gate_m = _dot_nt(mq8b, km)
    s_m = _dot_nt(mq8b, k_m)
    cnt = jnp.zeros((8, LANES), F32)
    for d in range(1, n_blk):
        lower = pltpu.roll(gate_m, d, axis=1)
        upper = pltpu.roll(gate_m, LANES - d, axis=1)
        cnt = cnt + jnp.where((lane >= d) & (lower >= gate_m), 1.0, 0.0)
        cnt = cnt + jnp.where((lane + d < n_blk) & (upper > gate_m), 1.0, 0.0)
    pick_m = jnp.where((lane < n_blk) & (cnt < MOBA_TOPK), 1.0, 0.0)
    ok_m = _dot(pick_m.astype(BF), emoba_ref[...]) > 0.5
    moba_new = fields["moba"]
    s = jnp.where(ok_m, s_m, NEG)
    s_new = jnp.sum(mq8 * moba_new[:, 0:384], axis=1, keepdims=True)
    m = jnp.maximum(jnp.max(s, axis=1, keepdims=True), s_new)
    e = jnp.exp2(s - m)
    e_new = jnp.exp2(s_new - m)
    den = jnp.sum(e, axis=1, keepdims=True) + e_new
    o_m = (e_new * moba_new[:, 384:768] + _dot(e.astype(BF), v_m)) / den
    omoba_ref[...] = jnp.sum(jnp.where(row == (lane3 >> 6), o_m, 0.0), axis=0, keepdims=True).astype(BF)

    for pi, pg in enumerate(nsa_pages):
        cmp_ref[0, pi * page:(pi + 1) * page, :] = pg[:, 0:LANES]
        cmp_ref[1, pi * page:(pi + 1) * page, :] = pg[:, LANES:2 * LANES]
    kraw = _compress_tokens(lambda l: cmp_ref[0, pl.ds(l, n_groups, stride=CMP_STRIDE), :],
                            n_groups, wabk_ref, babk_ref, w2k_ref)
    kcmp = _head_norm(kraw, gkc_ref[...], _ones_bd())
    vcmp = _compress_tokens(lambda l: cmp_ref[1, pl.ds(l, n_groups, stride=CMP_STRIDE), :],
                            n_groups, wabv_ref, babv_ref, w2v_ref)

    qc8 = stack_q(fields["qc"])
    c_end = lax.broadcasted_iota(jnp.int32, (1, n_groups), 1) * CMP_STRIDE + CMP_LEN
    p_c = _softmax_rows(_dot3_nt(qc8, kcmp) * SCALE, c_end <= past + 1)
    o_cmp = _dot(p_c.astype(BF), vcmp.astype(BF))
    imp8 = _dot_sel(p_c, ovl_ref[...])
    imp = jnp.sum(jnp.where(live6 & (row_h == lane_h), imp8, 0.0), axis=0, keepdims=True)
    jblk = lane & (SEL_BLOCK - 1)
    own = past // SEL_BLOCK
    sel = _rank_select(jnp.broadcast_to(imp, (8, LANES)), jblk, own, own + 1, SEL_TOPN)[0:1]
    sel8 = jnp.where(live6 & (row_h == lane_h), sel, 0.0)

    qr8 = stack_q(fields["qr"])
    qr8b = qr8.astype(BF)
    nsa_new = fields["nsa"]
    picked = _dot(sel8.astype(BF), esel_ref[...]) > 0.5
    k_sel = jnp.concatenate([pg[:, 2 * LANES:3 * LANES] for pg in nsa_pages], axis=0).astype(BF)
    v_sel = jnp.concatenate([pg[:, 3 * LANES:4 * LANES] for pg in nsa_pages], axis=0).astype(BF)
    s = jnp.where(picked, _dot_nt(qr8b, k_sel), NEG)
    own_picked = jnp.sum(jnp.where(jblk == own, sel8, 0.0), axis=1, keepdims=True) > 0.5
    s_new = jnp.where(own_picked, jnp.sum(qr8 * nsa_new[:, 2 * LANES:3 * LANES], axis=1, keepdims=True), NEG)
    m = jnp.maximum(jnp.maximum(jnp.max(s, axis=1, keepdims=True), s_new), M_INIT)
    e = jnp.exp2(s - m)
    e_new = jnp.exp2(s_new - m)
    den = jnp.sum(e, axis=1, keepdims=True) + e_new
    acc = e_new * nsa_new[:, 3 * LANES:4 * LANES] + _dot(e.astype(BF), v_sel)
    o_sel = jnp.where(den > 0.0, acc / den, 0.0)

    win_new = fields["win"]
    n_win = winst_ref.shape[0]
    kpos_w = past - n_win + lax.broadcasted_iota(jnp.int32, (1, n_win), 1)
    ok_w = (past - kpos_w) < WIN
    s = jnp.where(ok_w, _dot_nt(qr8b, winst_ref[:, 0:LANES].astype(BF)), NEG)
    s_new = jnp.sum(qr8 * win_new[:, 0:LANES], axis=1, keepdims=True)
    m = jnp.maximum(jnp.max(s, axis=1, keepdims=True), s_new)
    e = jnp.exp2(s - m)
    e_new = jnp.exp2(s_new - m)
    den = jnp.sum(e, axis=1, keepdims=True) + e_new
    o_win = (_dot(e.astype(BF), winst_ref[:, LANES:2 * LANES].astype(BF)) + e_new * win_new[:, LANES:2 * LANES]) / den

    gate8 = jnp.broadcast_to(fields["gate"], (8, LANES))
    gh = gate8.astype(BF)
    gl = (gate8 - gh.astype(F32)).astype(BF)
    gexp = (_dot(gh, gexp_ref[...]) + _dot(gl, gexp_ref[...]))[0:1]
    branches = [by_group(o_cmp), by_group(o_sel), by_group(o_win)]
    for g in range(NSA_GROUP):
        out = jnp.zeros((1, LANES), F32)
        for br in range(3):
            out = out + gexp[:, (g * 3 + br) * LANES:(g * 3 + br + 1) * LANES] * branches[br][g]
        onsa_ref[:, g * LANES:(g + 1) * LANES] = out.astype(BF)

    x = fields["xpool"]
    hist = poolst_ref[...]
    hrow = lax.broadcasted_iota(jnp.int32, (POOL_HIST, 1), 0)
    sums = {w: x + jnp.sum(jnp.where(hrow >= POOL_HIST - (w - 1), hist, 0.0), axis=0, keepdims=True)
            for w in POOL_WINDOWS}
    y = _pool_mix(sums, x, jnp.full((1, 1), past + 1, jnp.int32), pw_ref, sc_ref)
    ypool_ref[...] = y.astype(BF)


def _sample(page_table, rows, winst, poolst, cache_nsa, cache_moba, cw, consts, pw, sc, layer, past):
    n_seq, n_pages = page_table.shape
    page = cache_nsa.shape[2]
    per_seq = lambda w: pl.BlockSpec((None, 1, w), lambda i, pt: (i, 0, 0))
    lconst = lambda shape: pl.BlockSpec((None,) + tuple(shape), lambda i, pt: (layer,) + (0,) * len(shape),
                                        pipeline_mode=pl.Buffered(1))
    const = lambda a: pl.BlockSpec(a.shape, lambda i, pt: (0,) * a.ndim, pipeline_mode=pl.Buffered(1))
    cmp_specs = [lconst(s) for s in _cmp_weight_shapes()]

    def page_specs(width):
        def one(pj):
            return pl.BlockSpec((None, None, page, width), lambda i, pt: (layer, pt[i, pj], 0, 0))
        return [one(pj) for pj in range(n_pages)]

    row_all = jnp.concatenate([rows[name].astype(F32) for name, _ in SAMPLE_ROW_FIELDS], axis=1)
    row_w = row_all.shape[1]
    in_specs = ([per_seq(row_w),
                 pl.BlockSpec((None, None, winst.shape[2], 256), lambda i, pt: (layer, i, 0, 0)),
                 pl.BlockSpec((None, None, POOL_HIST, POOL_CH), lambda i, pt: (layer, i, 0, 0))]
                + cmp_specs + cmp_specs + [lconst((1, LANES))]
                + [const(consts["ovl_s"]), const(consts["gexp"]), const(consts["esel"]), const(consts["emoba"]),
                   lconst((POOL_CH, POOL_CH)), lconst((1, POOL_CH))]
                + page_specs(512) + page_specs(768))
    grid_spec = pltpu.PrefetchScalarGridSpec(
        num_scalar_prefetch=1, grid=(n_seq,), in_specs=in_specs,
        out_specs=[per_seq(384), per_seq(384), per_seq(POOL_CH)],
        scratch_shapes=[pltpu.VMEM((2, past, LANES), F32)])
    outs = pl.pallas_call(
        functools.partial(_sample_kernel, n_pages=n_pages, page=page, past=past),
        grid_spec=grid_spec,
        out_shape=[jax.ShapeDtypeStruct((n_seq, 1, 384), BF), jax.ShapeDtypeStruct((n_seq, 1, 384), BF),
                   jax.ShapeDtypeStruct((n_seq, 1, POOL_CH), BF)],
        compiler_params=_cparams("parallel"),
        name="sample_step",
    )(page_table, row_all.reshape(n_seq, 1, row_w), winst, poolst,
      cw["wabk"], cw["babk"], cw["w2k"], cw["wabv"], cw["babv"], cw["w2v"], cw["gkc"],
      consts["ovl_s"], consts["gexp"], consts["esel"], consts["emoba"], pw, sc,
      *([cache_nsa] * n_pages), *([cache_moba] * n_pages))
    return [o.reshape(n_seq, o.shape[2]) for o in outs]


def _combine_ffn_kernel(x_ref, yp_ref, on_ref, om_ref, w_ref, g_ref, wg_ref, wu_ref, wd_ref, o_ref, *, f_chunk):
    mix = jnp.concatenate([yp_ref[...], on_ref[...], om_ref[...]], axis=1)
    o_ref[...] = _ffn_body(x_ref[...] + _dot(mix, w_ref[...]), g_ref[...], wg_ref, wu_ref, wd_ref, f_chunk)


def _combine_ffn(x, ypool, onsa, omoba, w_out, g, wg, wu, wd, layer):
    n, d = x.shape
    d_ff = wg.shape[2]
    tm = _row_tile(n)
    row = lambda w: pl.BlockSpec((tm, w), lambda i: (i, 0))
    return pl.pallas_call(
        functools.partial(_combine_ffn_kernel, f_chunk=_f_chunk(d_ff)),
        grid=(n // tm,),
        in_specs=[row(d), row(POOL_CH), row(384), row(384), _const_spec((POOL_CH + 768, d), layer),
                  _const_spec((1, d), layer),
                  _const_spec((d, d_ff), layer), _const_spec((d, d_ff), layer), _const_spec((d_ff, d), layer)],
        out_specs=row(d),
        out_shape=jax.ShapeDtypeStruct((n, d), F32),
        compiler_params=_cparams("parallel"),
        name="combine_ffn",
    )(x, ypool, onsa, omoba, w_out, g, wg, wu, wd)


def _in_col_perm():
    cols = list(range(0, POOL_CH))
    for g in range(NSA_GROUP):
        for h in range(NSA_KV_HEADS):
            base = POOL_CH + (h * NSA_GROUP + g) * HEAD_DIM
            cols += range(base, base + HEAD_DIM)
    nsa_end = POOL_CH + NSA_HEADS * HEAD_DIM
    kv_end = nsa_end + 6 * NSA_KV_HEADS * HEAD_DIM
    cols += range(nsa_end, kv_end)
    cols += range(kv_end + GATE_COLS, kv_end + GATE_COLS + 3 * MOBA_HEADS * HEAD_DIM)
    gate_cols = list(range(kv_end, kv_end + GATE_COLS))
    return np.array(cols, np.int32), np.array(gate_cols, np.int32)


def _out_row_perm():
    rows = list(range(0, POOL_CH))
    for g in range(NSA_GROUP):
        for h in range(NSA_KV_HEADS):
            base = POOL_CH + (h * NSA_GROUP + g) * HEAD_DIM
            rows += range(base, base + HEAD_DIM)
    rows += range(POOL_CH + NSA_HEADS * HEAD_DIM, POOL_CH + NSA_HEADS * HEAD_DIM + MOBA_HEADS * HEAD_DIM)
    return np.array(rows, np.int32)


def _overlap(n_cmp_rows, n_cmp_valid, n_blocks):
    n = np.arange(n_cmp_rows)
    c_start = n * CMP_STRIDE
    c_end = c_start + CMP_LEN
    jb = np.arange(SEL_BLOCK)
    ov = (c_start[:, None] < (jb[None, :] + 1) * SEL_BLOCK) & (c_end[:, None] > jb[None, :] * SEL_BLOCK)
    ov &= (n[:, None] < n_cmp_valid) & (jb[None, :] < n_blocks)
    return ov.astype(np.float32)


def _constants(t, past):
    g_p = t // CMP_STRIDE
    ov = _overlap(g_p, g_p - 1, t // SEL_BLOCK)
    ovl_p = np.zeros((2 * g_p, LANES), np.float32)
    ovl_p[:g_p, :SEL_BLOCK] = ov
    ovl_p[g_p:, SEL_BLOCK:] = ov
    g_s = past // CMP_STRIDE
    ov_s = _overlap(g_s, g_s - 1, past // SEL_BLOCK + 1)
    ovl_s = np.concatenate([ov_s, ov_s], axis=1)
    lane = np.arange(LANES)
    gexp = np.zeros((LANES, 9 * LANES), np.float32)
    for g in range(NSA_GROUP):
        for br in range(3):
            src = (lane // HEAD_DIM) * 9 + g * 3 + br
            gexp[src, (g * 3 + br) * LANES + lane] = 1.0
    kp = np.arange(past)
    esel = ((lane[:, None] % SEL_BLOCK) == (kp[None, :] // SEL_BLOCK)).astype(np.float32)
    emoba = (lane[:, None] == (kp[None, :] // MOBA_BLOCK)).astype(np.float32)
    kt = np.arange(t)
    et_sel = ((kt[:, None] // SEL_BLOCK) == (lane[None, :] % SEL_BLOCK)).astype(np.float32)
    et_moba = ((kt[:, None] // MOBA_BLOCK) == (lane[None, :] % 16)).astype(np.float32)
    as_bf = lambda a: jnp.asarray(a, BF)
    return {"ovlt_p": as_bf(ovl_p.T), "ovl_s": as_bf(ovl_s), "gexp": as_bf(gexp), "esel": as_bf(esel),
            "emoba": as_bf(emoba), "et_sel": as_bf(et_sel), "et_moba": as_bf(et_moba)}


def _rope_tables(pos):
    half = ROPE_DIM // 2
    inv_freq = ROPE_THETA ** (-jnp.arange(half, dtype=F32) / half)
    ang = pos.astype(F32)[:, None] * inv_freq[None, :]
    cos, sin, zero = jnp.cos(ang), jnp.sin(ang), jnp.zeros_like(ang)
    rest = HEAD_DIM - ROPE_DIM
    c = jnp.concatenate([cos, cos, jnp.ones((ang.shape[0], rest), F32)], axis=1)
    a = jnp.concatenate([-sin, zero, jnp.zeros((ang.shape[0], rest), F32)], axis=1)
    b = jnp.concatenate([zero, sin, jnp.zeros((ang.shape[0], rest), F32)], axis=1)
    tile2 = lambda m: jnp.concatenate([m, m], axis=1)
    return tile2(c), tile2(a), tile2(b)


def _cmp_weights(cmp_pos, cmp_w1, cmp_w2, g_nsa_kc):
    bd = lambda w: jnp.concatenate([jnp.concatenate([w, jnp.zeros_like(w)], axis=-1),
                                    jnp.concatenate([jnp.zeros_like(w), w], axis=-1)], axis=-2)
    out = {}
    depth = cmp_pos.shape[0]
    for i, name in enumerate(("k", "v")):
        pos2 = jnp.concatenate([cmp_pos[:, i], cmp_pos[:, i]], axis=-1)
        w1 = bd(cmp_w1[:, i])
        out["w2" + name] = bd(cmp_w2[:, i]).astype(BF)
        halves = [w1[:, s:s + CMP_STRIDE].reshape(depth, CMP_STRIDE * LANES, LANES) for s in (0, CMP_STRIDE)]
        out["wab" + name] = jnp.concatenate(halves, axis=-1).astype(BF)
        bias = [jnp.einsum("xlc,xlce->xe", pos2[:, s:s + CMP_STRIDE], w1[:, s:s + CMP_STRIDE],
                           precision=lax.Precision.HIGHEST) for s in (0, CMP_STRIDE)]
        out["bab" + name] = jnp.concatenate(bias, axis=-1)[:, None, :]
    out["gkc"] = jnp.concatenate([g_nsa_kc, g_nsa_kc], axis=-1)[:, None, :]
    return out


def kernel(x_prompt, x_sample, cache_nsa_kv, cache_moba_kv, state_nsa_win, state_pool, page_table, g_ffa, w_ffa_gate, w_ffa_up, w_ffa_down, g_mix, w_in, w_out, pool_w, pool_scale, g_nsa_q, g_nsa_kc, g_nsa_ks, g_nsa_kw, cmp_pos, cmp_w1, cmp_w2, g_moba_q, g_moba_k, g_ffb, w_ffb_gate, w_ffb_up, w_ffb_down):
    batch, t, d = x_prompt.shape
    n_seq = x_sample.shape[0]
    depth, n_pool, page = cache_nsa_kv.shape[:3]
    n_pages = page_table.shape[1]
    past = n_pages * page
    n_p = batch * t
    n_win = state_nsa_win.shape[2]
    assert x_sample.shape[1] == 1 and t % MOBA_BLOCK == 0 and past % MOBA_BLOCK == 0 and n_win == WIN

    tp = _row_tile(t)
    tq_nsa = 256
    tq_moba = min(2 * MOBA_BLOCK, t)

    col_perm, gate_cols = _in_col_perm()
    w_in_p = jnp.concatenate([w_in[:, :, col_perm], w_in[:, :, gate_cols],
                              jnp.zeros((depth, d, LANES - GATE_COLS), w_in.dtype)], axis=-1).astype(BF)
    w_out_p = w_out[:, _out_row_perm(), :].astype(BF)
    bf = lambda w: w.astype(BF)
    wa = (bf(w_ffa_gate), bf(w_ffa_up), bf(w_ffa_down))
    wb = (bf(w_ffb_gate), bf(w_ffb_up), bf(w_ffb_down))
    tile2 = lambda g: jnp.concatenate([g, g], axis=-1)
    gains = jnp.stack([tile2(g_nsa_q), tile2(g_nsa_ks), tile2(g_nsa_kw), tile2(g_moba_q), tile2(g_moba_k)]
                      + [jnp.ones((depth, LANES), F32)] * 3, axis=1)
    cw = _cmp_weights(cmp_pos, cmp_w1, cmp_w2, g_nsa_kc)
    eye = jnp.eye(len(POOL_WINDOWS), dtype=pool_w.dtype)
    pw_bd = jnp.einsum("lgij,gh->lgihj", pool_w, eye).reshape(depth, POOL_CH, POOL_CH).astype(BF)
    sc = pool_scale[:, None, :]
    consts = _constants(t, past)
    rope_p = _rope_tables(jnp.arange(t, dtype=jnp.int32))
    rope_s = _rope_tables(jnp.full((n_seq,), past, jnp.int32))
    g_a, g_m, g_b = g_ffa[:, None, :], g_mix[:, None, :], g_ffb[:, None, :]

    cache_nsa = cache_nsa_kv.reshape(depth, n_pool, page, 512)
    cache_moba = cache_moba_kv.reshape(depth, n_pool, page, 768)
    winst = state_nsa_win.reshape(depth, n_seq, n_win, 256)

    xp = x_prompt.reshape(n_p, d)
    xs = x_sample.reshape(n_seq, d)
    names = ("xpool", "qc", "qr", "nsa", "nsab", "win", "winb", "mq", "moba", "mobab", "gate")
    outs = [[] for _ in range(8)]
    for l in range(depth):
        xp = _ffn(xp, g_a, *wa, l)
        xs = _ffn(xs, g_a, *wa, l)
        pp = dict(zip(names, _prep(xp, g_m, w_in_p, gains, rope_p, l)))
        ps = dict(zip(names, _prep(xs, g_m, w_in_p, gains, rope_s, l)))

        kcmp, vcmp = _compress_prompt(pp["nsa"], cw, l, batch, t)
        onsa_p = _nsa_prompt(pp["qc"], pp["qr"], pp["gate"], kcmp, vcmp, pp["nsab"], pp["winb"], consts["ovlt_p"],
                             consts["gexp"], consts["et_sel"], batch, t, tq_nsa)
        omoba_p = _moba_prompt(pp["mq"], pp["mobab"], consts["et_moba"], batch, t, tq_moba)
        ypool_p = _pool_prompt(pp["xpool"], pw_bd, sc, l, batch, t, tp)
        onsa_s, omoba_s, ypool_s = _sample(page_table, ps, winst, state_pool, cache_nsa, cache_moba, cw, consts,
                                           pw_bd, sc, l, past)

        xp = _combine_ffn(xp, ypool_p, onsa_p, omoba_p, w_out_p, g_b, *wb, l)
        xs = _combine_ffn(xs, ypool_s, onsa_s, omoba_s, w_out_p, g_b, *wb, l)

        outs[0].append(pp["nsa"].reshape(batch, t, 4, NSA_KV_HEADS, HEAD_DIM))
        outs[1].append(ps["nsa"].reshape(n_seq, 1, 4, NSA_KV_HEADS, HEAD_DIM))
        outs[2].append(pp["moba"].reshape(batch, t, 2, MOBA_HEADS, HEAD_DIM))
        outs[3].append(ps["moba"].reshape(n_seq, 1, 2, MOBA_HEADS, HEAD_DIM))
        outs[4].append(pp["win"].reshape(batch, t, 2, NSA_KV_HEADS, HEAD_DIM)[:, t - min(WIN, t):])
        outs[5].append(ps["win"].reshape(n_seq, 1, 2, NSA_KV_HEADS, HEAD_DIM))
        outs[6].append(pp["xpool"].reshape(batch, t, POOL_CH)[:, t - POOL_HIST:])
        outs[7].append(ps["xpool"].reshape(n_seq, 1, POOL_CH))

    stacked = [jnp.stack(o, axis=0) for o in outs]
    stacked[5] = jnp.concatenate([state_nsa_win[:, :, 1:], stacked[5]], axis=2)
    stacked[7] = jnp.concatenate([state_pool[:, :, 1:], stacked[7]], axis=2)
    return (xp.reshape(batch, t, d), xs.reshape(n_seq, 1, d), *stacked)
```

```python
import functools

import numpy as np
import jax
import jax.numpy as jnp
from jax import lax
from jax.experimental import pallas as pl
from jax.experimental.pallas import tpu as pltpu

F32 = jnp.float32
BF = jnp.bfloat16

HEAD_DIM = 64
LANES = 128
POOL_CH = 256
POOL_WINDOWS = (2, 4, 8, 16)
POOL_HIST = 15
NSA_KV_HEADS = 2
NSA_GROUP = 3
NSA_HEADS = 6
MOBA_HEADS = 6
CMP_LEN = 32
CMP_STRIDE = 16
SEL_BLOCK = 64
SEL_TOPN = 16
WIN = 512
MOBA_BLOCK = 256
MOBA_TOPK = 3
ROPE_THETA = 500000.0
ROPE_DIM = 16
EPS = 1e-6
NEG = -1e30
M_INIT = -1e29
SEL_FORCE = 1e4
SCALE = HEAD_DIM ** -0.5
Q_PRESCALE = SCALE * 1.4426950408889634
GATE_COLS = 3 * NSA_HEADS
D_QKV = 2560
D_IN_PAD = D_QKV + LANES
VMEM_LIMIT = 56 * 1024 * 1024


def _cparams(*sem):
    return pltpu.CompilerParams(dimension_semantics=sem, vmem_limit_bytes=VMEM_LIMIT)


def _dot(a, b):
    return jnp.dot(a, b, preferred_element_type=F32)


def _dot_nt(a, b):
    return lax.dot_general(a, b, (((1,), (1,)), ((), ())), preferred_element_type=F32)


def _split3(x):
    hi = x.astype(BF)
    r = x - hi.astype(F32)
    mid = r.astype(BF)
    lo = (r - mid.astype(F32)).astype(BF)
    return hi, mid, lo


def _dot_sel(x, m):
    hi, mid, lo = _split3(x)
    return _dot(hi, m) + _dot(mid, m) + _dot(lo, m)


def _dot3_nt(a, b):
    ah = a.astype(BF)
    al = (a - ah.astype(F32)).astype(BF)
    bh = b.astype(BF)
    bl = (b - bh.astype(F32)).astype(BF)
    return _dot_nt(ah, bh) + _dot_nt(ah, bl) + _dot_nt(al, bh)


def _sigmoid(x):
    return 1.0 / (1.0 + jnp.exp(-x))


def _silu(x):
    return x * _sigmoid(x)


def _rms_rows(x, g):
    return x * lax.rsqrt(jnp.mean(x * x, axis=-1, keepdims=True) + EPS) * g


def _ones_bd():
    r = lax.broadcasted_iota(jnp.int32, (LANES, LANES), 0) >> 6
    c = lax.broadcasted_iota(jnp.int32, (LANES, LANES), 1) >> 6
    return jnp.where(r == c, 1.0, 0.0).astype(BF)


def _head_norm(x, gain, ones_bd):
    ms = _dot_sel(x * x, ones_bd) * (1.0 / HEAD_DIM)
    return x * lax.rsqrt(ms + EPS) * gain


def _rope(x, c, a, b):
    return x * c + pltpu.roll(x, LANES - ROPE_DIM // 2, axis=1) * a + pltpu.roll(x, ROPE_DIM // 2, axis=1) * b


def _softmax_rows(s, valid):
    s = jnp.where(valid, s, NEG)
    m = jnp.max(s, axis=1, keepdims=True)
    e = jnp.where(valid, jnp.exp(s - m), 0.0)
    den = jnp.sum(e, axis=1, keepdims=True)
    return jnp.where(den > 0.0, e / den, 0.0)


def _sum_list(xs):
    return functools.reduce(lambda a, b: a + b, xs)


def _ffn_body(x, g, wg_ref, wu_ref, wd_ref, f_chunk):
    xn = _rms_rows(x, g).astype(BF)
    d_ff = wg_ref.shape[1]
    acc = jnp.zeros(x.shape, F32)
    for c in range(d_ff // f_chunk):
        sl = slice(c * f_chunk, (c + 1) * f_chunk)
        gate = _dot(xn, wg_ref[:, sl])
        up = _dot(xn, wu_ref[:, sl])
        h = (_silu(gate) * up).astype(BF)
        acc = acc + _dot(h, wd_ref[sl, :])
    return x + 0.5 * acc


def _ffn_kernel(x_ref, g_ref, wg_ref, wu_ref, wd_ref, o_ref, *, f_chunk):
    o_ref[...] = _ffn_body(x_ref[...], g_ref[...], wg_ref, wu_ref, wd_ref, f_chunk)


def _f_chunk(d_ff):
    for c in (1408, 1024, 512, 256, 128):
        if d_ff % c == 0:
            return c
    return d_ff


def _row_tile(n):
    return next((c for c in (512, 384, 256, 128) if n % c == 0), n)


def _const_spec(shape, layer):
    nd = len(shape)
    return pl.BlockSpec((None,) + tuple(shape), lambda *_: (layer,) + (0,) * nd, pipeline_mode=pl.Buffered(1))


def _ffn(x, g, wg, wu, wd, layer):
    n, d = x.shape
    d_ff = wg.shape[2]
    tm = _row_tile(n)
    return pl.pallas_call(
        functools.partial(_ffn_kernel, f_chunk=_f_chunk(d_ff)),
        grid=(n // tm,),
        in_specs=[pl.BlockSpec((tm, d), lambda i: (i, 0)),
                  _const_spec((1, d), layer),
                  _const_spec((d, d_ff), layer), _const_spec((d, d_ff), layer), _const_spec((d_ff, d), layer)],
        out_specs=pl.BlockSpec((tm, d), lambda i: (i, 0)),
        out_shape=jax.ShapeDtypeStruct((n, d), F32),
        compiler_params=_cparams("parallel"),
        name="ffn",
    )(x, g, wg, wu, wd)


def _prep_kernel(x_ref, g_ref, w_ref, gains_ref, c_ref, a_ref, b_ref,
                 xpool_ref, qc_ref, qr_ref, nsa_ref, nsab_ref, win_ref, winb_ref,
                 mq_ref, moba_ref, mobab_ref, gate_ref):
    xn = _rms_rows(x_ref[...], g_ref[...]).astype(BF)
    z = _dot(xn, w_ref[...])
    c, a, b = c_ref[...], a_ref[...], b_ref[...]
    gains = gains_ref[...]
    slab = lambda i: z[:, i * LANES:(i + 1) * LANES]

    normed_gain = {2: 0, 3: 0, 4: 0, 7: 1, 9: 2, 11: 3, 12: 3, 13: 3, 14: 4, 15: 4, 16: 4}
    r2 = lax.broadcasted_iota(jnp.int32, (2 * LANES, 2 * LANES), 0) >> 6
    c2 = lax.broadcasted_iota(jnp.int32, (2 * LANES, 2 * LANES), 1) >> 6
    ones_bd2 = jnp.where(r2 == c2, 1.0, 0.0).astype(BF)
    idxs = list(normed_gain)
    normed = {}
    for p0 in range(0, len(idxs), 2):
        pair = idxs[p0:p0 + 2]
        sq = jnp.concatenate([slab(i) * slab(i) for i in pair], axis=1)
        hi = sq.astype(BF)
        lo = (sq - hi.astype(F32)).astype(BF)
        w = ones_bd2 if len(pair) == 2 else ones_bd2[0:LANES, 0:LANES]
        ms = (_dot(hi, w) + _dot(lo, w)) * (1.0 / HEAD_DIM)
        for k, i in enumerate(pair):
            gi = normed_gain[i]
            normed[i] = slab(i) * lax.rsqrt(ms[:, k * LANES:(k + 1) * LANES] + EPS) * gains[gi:gi + 1]
    norm_rope = lambda i: _rope(normed[i], c, a, b)

    xpool_ref[...] = z[:, 0:POOL_CH]
    for gi in range(3):
        qc_ref[:, gi * LANES:(gi + 1) * LANES] = normed[2 + gi]
        qr_ref[:, gi * LANES:(gi + 1) * LANES] = (norm_rope(2 + gi) * Q_PRESCALE).astype(BF)
    nsa = [slab(5), slab(6), norm_rope(7), slab(8)]
    for i, s in enumerate(nsa):
        nsa_ref[:, i * LANES:(i + 1) * LANES] = s
        nsab_ref[:, i * LANES:(i + 1) * LANES] = s.astype(BF)
    win = [norm_rope(9), slab(10)]
    for i, s in enumerate(win):
        win_ref[:, i * LANES:(i + 1) * LANES] = s
        winb_ref[:, i * LANES:(i + 1) * LANES] = s.astype(BF)
    for i in range(3):
        mq_ref[:, i * LANES:(i + 1) * LANES] = (norm_rope(11 + i) * Q_PRESCALE).astype(BF)
    for i in range(6):
        s = norm_rope(14 + i) if i < 3 else slab(14 + i)
        moba_ref[:, i * LANES:(i + 1) * LANES] = s
        mobab_ref[:, i * LANES:(i + 1) * LANES] = s.astype(BF)
    gate_ref[...] = _sigmoid(slab(20))


def _prep(x, g, w_in, gains, rope, layer):
    n, d = x.shape
    tm = _row_tile(min(n, rope[0].shape[0]))
    nb = rope[0].shape[0] // tm
    row = lambda w: pl.BlockSpec((tm, w), lambda i: (i, 0))
    tab = pl.BlockSpec((tm, LANES), lambda i: (i % nb, 0))
    widths_dtypes = [(POOL_CH, F32), (384, F32), (384, BF), (512, F32), (512, BF), (256, F32), (256, BF),
                     (384, BF), (768, F32), (768, BF), (LANES, F32)]
    return pl.pallas_call(
        _prep_kernel,
        grid=(n // tm,),
        in_specs=[row(d), _const_spec((1, d), layer), _const_spec((d, D_IN_PAD), layer),
                  _const_spec((8, LANES), layer), tab, tab, tab],
        out_specs=[row(w) for w, _ in widths_dtypes],
        out_shape=[jax.ShapeDtypeStruct((n, w), dt) for w, dt in widths_dtypes],
        compiler_params=_cparams("parallel"),
        name="prep",
    )(x, g, w_in, gains, *rope)


def _compress_tokens(load_rows, n_groups, wab_ref, bab_ref, w2_ref):
    x = jnp.concatenate([load_rows(l) for l in range(CMP_STRIDE)], axis=1).astype(BF)
    pab = _dot(x, wab_ref[...]) + bab_ref[...]
    pre = pab[:, 0:LANES] + pltpu.roll(pab[:, LANES:2 * LANES], n_groups - 1, axis=0)
    return _dot(_silu(pre).astype(BF), w2_ref[...])


def _compress_prompt_kernel(kc_ref, vc_ref, wabk_ref, babk_ref, w2k_ref, wabv_ref, babv_ref, w2v_ref, gkc_ref,
                            kcmp_ref, vcmp_ref, *, n_groups):
    kraw = _compress_tokens(lambda l: kc_ref[pl.ds(l, n_groups, stride=CMP_STRIDE), :],
                            n_groups, wabk_ref, babk_ref, w2k_ref)
    kcmp_ref[...] = _head_norm(kraw, gkc_ref[...], _ones_bd())
    vcmp_ref[...] = _compress_tokens(lambda l: vc_ref[pl.ds(l, n_groups, stride=CMP_STRIDE), :],
                                     n_groups, wabv_ref, babv_ref, w2v_ref)


def _cmp_weight_shapes():
    return [(CMP_STRIDE * LANES, 2 * LANES), (1, 2 * LANES), (LANES, LANES)]


def _compress_prompt(nsa_rows, cw, layer, batch, t):
    n_groups = t // CMP_STRIDE
    wspecs = [_const_spec(s, layer) for s in _cmp_weight_shapes()]
    return pl.pallas_call(
        functools.partial(_compress_prompt_kernel, n_groups=n_groups),
        grid=(batch,),
        in_specs=[pl.BlockSpec((t, LANES), lambda b: (b, 0)), pl.BlockSpec((t, LANES), lambda b: (b, 1))]
                 + wspecs + wspecs + [_const_spec((1, LANES), layer)],
        out_specs=[pl.BlockSpec((n_groups, LANES), lambda b: (b, 0))] * 2,
        out_shape=[jax.ShapeDtypeStruct((batch * n_groups, LANES), F32)] * 2,
        compiler_params=_cparams("parallel"),
        name="compress_prompt",
    )(nsa_rows, nsa_rows, cw["wabk"], cw["babk"], cw["w2k"], cw["wabv"], cw["babv"], cw["w2v"], cw["gkc"])


def _rank_select(imp, j, own, n_blocks, top_n):
    forced = (j == 0) | (j == own) | (j == own - 1)
    imp = jnp.where(forced, imp + SEL_FORCE, imp)
    imp = jnp.where(j <= own, imp, -jnp.inf)
    cnt = jnp.zeros(imp.shape, F32)
    for d in range(1, n_blocks):
        lower = pltpu.roll(imp, d, axis=1)
        upper = pltpu.roll(imp, LANES - d, axis=1)
        cnt = cnt + jnp.where((j >= d) & (lower >= imp), 1.0, 0.0)
        cnt = cnt + jnp.where((j + d < n_blocks) & (upper > imp), 1.0, 0.0)
    return jnp.where(j <= own, jnp.where(cnt < top_n, 1.0, 0.0), 0.0)


def _online_update(s, v, m_ref, l_ref, acc_ref, idx):
    m_prev = m_ref[idx]
    m_new = jnp.maximum(m_prev, jnp.max(s, axis=1, keepdims=True))
    alpha = jnp.exp2(m_prev - m_new)
    ps = [jnp.exp2(s[:, i * LANES:(i + 1) * LANES] - m_new) for i in range(s.shape[1] // LANES)]
    l_ref[idx] = alpha * l_ref[idx] + _sum_list(ps)
    acc_ref[idx] = alpha * acc_ref[idx] + _dot(jnp.concatenate(ps, axis=1).astype(BF), v)
    m_ref[idx] = m_new


def _online_init(m_ref, l_ref, acc_ref):
    m_ref[...] = jnp.full(m_ref.shape, M_INIT, F32)
    l_ref[...] = jnp.zeros(l_ref.shape, F32)
    acc_ref[...] = jnp.zeros(acc_ref.shape, F32)


def _online_out(l_ref, acc_ref, idx):
    l = jnp.sum(l_ref[idx], axis=1, keepdims=True)
    return jnp.where(l > 0.0, acc_ref[idx] / l, 0.0)


def _nsa_prompt_kernel(qc_ref, qr_ref, gate_ref, kcmp_ref, vcmp_ref, nsab_ref, winb_ref, ovlt_ref, gexp_ref, et_ref,
                       o_ref, impt_ref, m_ref, l_ref, acc_ref, *, tq, tk):
    qi = pl.program_id(1)
    q0 = qi * tq
    lane_h = lax.broadcasted_iota(jnp.int32, (1, LANES), 1) >> 6
    qpos = q0 + lax.broadcasted_iota(jnp.int32, (tq, 1), 0)
    qpos6 = jnp.concatenate([qpos] * NSA_HEADS, axis=0)

    def stack6(tile, zero):
        return jnp.concatenate([jnp.where(lane_h == h, tile(g), zero)
                                for h in range(NSA_KV_HEADS) for g in range(NSA_GROUP)], axis=0)

    kc = kcmp_ref[...]
    n_cmp = kc.shape[0]
    c_end = lax.broadcasted_iota(jnp.int32, (1, n_cmp), 1) * CMP_STRIDE + CMP_LEN
    q_cmp = stack6(lambda g: qc_ref[:, g * LANES:(g + 1) * LANES], 0.0)
    p_c = _softmax_rows(_dot3_nt(q_cmp, kc) * SCALE, c_end <= qpos6 + 1)
    o_cmp = _dot(p_c.astype(BF), vcmp_ref[...].astype(BF))
    psum = [_sum_list([p_c[(h * NSA_GROUP + g) * tq:(h * NSA_GROUP + g + 1) * tq] for g in range(NSA_GROUP)])
            for h in range(NSA_KV_HEADS)]
    hi, mid, lo = _split3(jnp.concatenate(psum, axis=1))
    ovlt = ovlt_ref[...]
    impt = _dot_nt(ovlt, hi) + _dot_nt(ovlt, mid) + _dot_nt(ovlt, lo)
    brow = lax.broadcasted_iota(jnp.int32, (LANES, 1), 0)
    jrow = brow & (SEL_BLOCK - 1)
    own_l = (q0 + lax.broadcasted_iota(jnp.int32, (1, tq), 1)) >> 6
    forced = (jrow == 0) | (jrow == own_l) | (jrow == own_l - 1)
    impt = jnp.where(forced, impt + SEL_FORCE, impt)
    impt = jnp.where(jrow <= own_l, impt, -jnp.inf)
    impt_ref[...] = impt

    def beats(jp):
        comp = jnp.where(brow < SEL_BLOCK, impt_ref[pl.ds(jp, 1), :], impt_ref[pl.ds(SEL_BLOCK + jp, 1), :])
        return jnp.where(jrow > jp, jnp.where(comp >= impt, 1.0, 0.0), jnp.where(comp > impt, 1.0, 0.0))

    n_live = (q0 + tq - 1) // SEL_BLOCK + 1
    cnt = lax.fori_loop(0, (n_live + 1) // 2, lambda i, cnt: cnt + beats(2 * i) + beats(2 * i + 1),
                        jnp.zeros((LANES, tq), F32))
    sel_t = jnp.where(jrow <= own_l, jnp.where(cnt < SEL_TOPN, 1.0, 0.0), 0.0)
    sel_neg = jnp.where(sel_t.T > 0.5, 0.0, NEG)

    zero = jnp.zeros((), BF)
    q_rot = stack6(lambda g: qr_ref[:, g * LANES:(g + 1) * LANES], zero)
    q_aug = jnp.concatenate([q_rot, stack6(lambda g: sel_neg, 0.0).astype(BF)], axis=1)
    _online_init(m_ref, l_ref, acc_ref)

    def sel_chunk(start, width, causal):
        k_aug = jnp.concatenate([nsab_ref[pl.ds(start, width), 2 * LANES:3 * LANES],
                                 et_ref[pl.ds(start, width), :]], axis=1)
        v = nsab_ref[pl.ds(start, width), 3 * LANES:4 * LANES]
        s = _dot_nt(q_aug, k_aug)
        if causal:
            kpos = start + lax.broadcasted_iota(jnp.int32, (1, width), 1)
            s = jnp.where(kpos <= qpos6, s, NEG)
        _online_update(s, v, m_ref, l_ref, acc_ref, 0)

    c_last = (q0 + tq - 1) // tk
    lax.fori_loop(0, c_last, lambda c, carry: (sel_chunk(pl.multiple_of(c * tk, tk), tk, False), carry)[1], 0)
    d_start = pl.multiple_of(c_last * tk, tk)
    d_tiles = (q0 + tq - c_last * tk) // tq
    for r in range(1, tk // tq + 1):
        @pl.when(d_tiles == r)
        def _():
            sel_chunk(d_start, r * tq, True)
    o_sel = _online_out(l_ref, acc_ref, 0)

    n_slots = WIN // tq + 1
    slot_c = [qi - (n_slots - 1) + slot for slot in range(n_slots)]
    starts = [pl.multiple_of(jnp.maximum(c, 0) * tq, tq) for c in slot_c]
    k_band = jnp.concatenate([winb_ref[pl.ds(st, tq), 0:LANES] for st in starts], axis=0)
    v_band = jnp.concatenate([winb_ref[pl.ds(st, tq), LANES:2 * LANES] for st in starts], axis=0)
    s_band = _dot_nt(q_rot, k_band)
    s_w = []
    for slot in range(n_slots):
        s = s_band[:, slot * tq:(slot + 1) * tq]
        kpos = starts[slot] + lax.broadcasted_iota(jnp.int32, (1, tq), 1)
        if slot == 0:
            s = jnp.where(kpos > qpos6 - WIN, s, NEG)
        if slot == n_slots - 1:
            s = jnp.where(kpos <= qpos6, s, NEG)
        else:
            s = jnp.where(slot_c[slot] >= 0, s, NEG)
        s_w.append(s)
    m = jnp.max(functools.reduce(jnp.maximum, s_w), axis=1, keepdims=True)
    ps = [jnp.exp2(s - m) for s in s_w]
    l = jnp.sum(_sum_list(ps), axis=1, keepdims=True)
    o_win = _dot(jnp.concatenate(ps, axis=1).astype(BF), v_band) / l

    gate = gate_ref[...]
    gh = gate.astype(BF)
    gl = (gate - gh.astype(F32)).astype(BF)
    gexp = _dot(gh, gexp_ref[...]) + _dot(gl, gexp_ref[...])
    for g in range(NSA_GROUP):
        r0, r1 = g * tq, (NSA_GROUP + g) * tq
        out = jnp.zeros((tq, LANES), F32)
        for br, o in enumerate((o_cmp, o_sel, o_win)):
            both = jnp.where(lane_h == 0, o[r0:r0 + tq], o[r1:r1 + tq])
            out = out + gexp[:, (g * 3 + br) * LANES:(g * 3 + br + 1) * LANES] * both
        o_ref[:, g * LANES:(g + 1) * LANES] = out.astype(BF)


def _nsa_prompt(qc, qr, gate, kcmp, vcmp, nsab, winb, ovlt, gexp, et, batch, t, tq):
    nq = t // tq
    n_groups = t // CMP_STRIDE
    assert WIN % tq == 0 and t // SEL_BLOCK <= SEL_BLOCK
    tile = lambda w: pl.BlockSpec((tq, w), lambda b, i: (b * nq + i, 0))
    per_b = lambda r, w: pl.BlockSpec((r, w), lambda b, i: (b, 0))
    const = lambda a: pl.BlockSpec(a.shape, lambda b, i: (0, 0), pipeline_mode=pl.Buffered(1))
    return pl.pallas_call(
        functools.partial(_nsa_prompt_kernel, tq=tq, tk=min(512, t)),
        grid=(batch, nq),
        in_specs=[tile(384), tile(384), tile(LANES), per_b(n_groups, LANES), per_b(n_groups, LANES),
                  per_b(t, 512), per_b(t, 256), const(ovlt), const(gexp), const(et)],
        out_specs=tile(384),
        out_shape=jax.ShapeDtypeStruct((batch * t, 384), BF),
        scratch_shapes=[pltpu.VMEM((LANES, tq), F32)] + [pltpu.VMEM((1, NSA_HEADS * tq, LANES), F32)] * 3,
        compiler_params=_cparams("parallel", "parallel"),
        name="nsa_prompt",
    )(qc, qr, gate, kcmp, vcmp, nsab, winb, ovlt, gexp, et)


def _moba_prompt_kernel(mq_ref, mobab_ref, et_ref, o_ref, kmbd_ref, gate_ref, m_ref, l_ref, acc_ref, *, tq, t, tk):
    qi = pl.program_id(1)
    q0 = qi * tq
    n_pairs = MOBA_HEADS // 2
    lane = lax.broadcasted_iota(jnp.int32, (1, LANES), 1)
    lane_h = lane >> 6
    hh = lane >> 4
    qpos = q0 + lax.broadcasted_iota(jnp.int32, (tq, 1), 0)
    qpos2 = jnp.concatenate([qpos, qpos], axis=0)
    brow = lax.broadcasted_iota(jnp.int32, (LANES, 1), 0)
    jrow = brow & 15

    @pl.when(qi == 0)
    def _():
        tt = lax.broadcasted_iota(jnp.int32, (1, t), 1) >> 8
        ind = jnp.where(jrow == tt, 1.0, 0.0).astype(BF)
        km = _dot(ind, mobab_ref[:, 0:384]) * (1.0 / MOBA_BLOCK)
        col_h = lax.broadcasted_iota(jnp.int32, (1, 384), 1) >> 6
        kmbd_ref[...] = jnp.where(col_h == (brow >> 4), km, 0.0).astype(BF)

    gate = _dot_nt(kmbd_ref[...], mq_ref[...])
    gate_ref[...] = gate
    own_l = (q0 + lax.broadcasted_iota(jnp.int32, (1, tq), 1)) >> 8

    def rank_step(jp, cnt):
        comp = jnp.concatenate([jnp.broadcast_to(gate_ref[pl.ds(hd * 16 + jp, 1), :], (16, tq))
                                for hd in range(LANES // 16)], axis=0)
        beats = jnp.where(jrow > jp, jnp.where(comp >= gate, 1.0, 0.0), jnp.where(comp > gate, 1.0, 0.0))
        return cnt + jnp.where(jp < own_l, beats, 0.0)

    n_past = (q0 + tq - 1) // MOBA_BLOCK
    cnt = lax.fori_loop(0, n_past, rank_step, jnp.zeros((LANES, tq), F32))
    picked_t = jnp.where(jrow == own_l, 1.0, jnp.where(jrow < own_l, jnp.where(cnt < MOBA_TOPK, 1.0, 0.0), 0.0))
    pick_neg = jnp.where(picked_t.T > 0.5, 0.0, NEG)

    zero = jnp.zeros((), BF)
    q_aug = []
    for p in range(n_pairs):
        qp = mq_ref[:, p * LANES:(p + 1) * LANES]
        q2 = jnp.concatenate([jnp.where(lane_h == 0, qp, zero), jnp.where(lane_h == 1, qp, zero)], axis=0)
        pk = jnp.concatenate([jnp.where(hh == 2 * p, pick_neg, 0.0), jnp.where(hh == 2 * p + 1, pick_neg, 0.0)], axis=0)
        q_aug.append(jnp.concatenate([q2, pk.astype(BF)], axis=1))
    _online_init(m_ref, l_ref, acc_ref)

    def chunk(c, causal):
        start = pl.multiple_of(c * tk, tk)
        et = et_ref[pl.ds(start, tk), :]
        for p in range(n_pairs):
            k_aug = jnp.concatenate([mobab_ref[pl.ds(start, tk), p * LANES:(p + 1) * LANES], et], axis=1)
            v = mobab_ref[pl.ds(start, tk), 384 + p * LANES:384 + (p + 1) * LANES]
            s = _dot_nt(q_aug[p], k_aug)
            if causal:
                kpos = start + lax.broadcasted_iota(jnp.int32, (1, tk), 1)
                s = jnp.where(kpos <= qpos2, s, NEG)
            _online_update(s, v, m_ref, l_ref, acc_ref, p)

    c_last = (q0 + tq - 1) // tk
    lax.fori_loop(0, c_last, lambda c, carry: (chunk(c, False), carry)[1], 0)
    chunk(c_last, True)
    for p in range(n_pairs):
        out = _online_out(l_ref, acc_ref, p)
        o_ref[:, p * LANES:(p + 1) * LANES] = jnp.where(lane_h == 0, out[0:tq], out[tq:2 * tq]).astype(BF)


def _moba_prompt(mq, mobab, et, batch, t, tq):
    nq = t // tq
    assert t // MOBA_BLOCK <= 16
    tile = lambda w: pl.BlockSpec((tq, w), lambda b, i: (b * nq + i, 0))
    return pl.pallas_call(
        functools.partial(_moba_prompt_kernel, tq=tq, t=t, tk=min(512, t)),
        grid=(batch, nq),
        in_specs=[tile(384), pl.BlockSpec((t, 768), lambda b, i: (b, 0)),
                  pl.BlockSpec(et.shape, lambda b, i: (0, 0), pipeline_mode=pl.Buffered(1))],
        out_specs=tile(384),
        out_shape=jax.ShapeDtypeStruct((batch * t, 384), BF),
        scratch_shapes=[pltpu.VMEM((LANES, 384), BF), pltpu.VMEM((LANES, tq), F32)]
                       + [pltpu.VMEM((MOBA_HEADS // 2, 2 * tq, LANES), F32)] * 3,
        compiler_params=_cparams("parallel", "arbitrary"),
        name="moba_prompt",
    )(mq, mobab, et)


def _pool_mix(win_sums, x, count_pos, pw_ref, sc_ref):
    lane_g = lax.broadcasted_iota(jnp.int32, (1, POOL_CH), 1) >> 6
    s = win_sums[POOL_WINDOWS[-1]]
    w = jnp.full((1, POOL_CH), POOL_WINDOWS[-1], jnp.int32)
    for gi in range(len(POOL_WINDOWS) - 2, -1, -1):
        s = jnp.where(lane_g == gi, win_sums[POOL_WINDOWS[gi]], s)
        w = jnp.where(lane_g == gi, POOL_WINDOWS[gi], w)
    count = jnp.minimum(count_pos, w).astype(F32)
    d = s / count - x
    return _dot(d.astype(BF), pw_ref[...]) * sc_ref[...]


def _pool_prompt_kernel(x_ref, halo_ref, pw_ref, sc_ref, o_ref, buf_ref, *, tp):
    i = pl.program_id(1)
    x = x_ref[...]
    buf_ref[0:16, :] = jnp.where(i > 0, halo_ref[...], 0.0)
    buf_ref[16:16 + tp, :] = x
    acc = x
    sums = {}
    for k in range(1, POOL_WINDOWS[-1]):
        acc = acc + buf_ref[16 - k:16 - k + tp, :]
        if k + 1 in POOL_WINDOWS:
            sums[k + 1] = acc
    pos1 = i * tp + lax.broadcasted_iota(jnp.int32, (tp, 1), 0) + 1
    o_ref[...] = _pool_mix(sums, x, pos1, pw_ref, sc_ref).astype(BF)


def _pool_prompt(xpool, pw, sc, layer, batch, t, tp):
    nt = t // tp
    return pl.pallas_call(
        functools.partial(_pool_prompt_kernel, tp=tp),
        grid=(batch, nt),
        in_specs=[pl.BlockSpec((tp, POOL_CH), lambda b, i: (b * nt + i, 0)),
                  pl.BlockSpec((16, POOL_CH), lambda b, i: (jnp.maximum((b * nt + i) * (tp // 16) - 1, 0), 0)),
                  _const_spec((POOL_CH, POOL_CH), layer), _const_spec((1, POOL_CH), layer)],
        out_specs=pl.BlockSpec((tp, POOL_CH), lambda b, i: (b * nt + i, 0)),
        out_shape=jax.ShapeDtypeStruct((batch * t, POOL_CH), BF),
        scratch_shapes=[pltpu.VMEM((tp + 16, POOL_CH), F32)],
        compiler_params=_cparams("parallel", "parallel"),
        name="pool_prompt",
    )(xpool, xpool, pw, sc)


SAMPLE_ROW_FIELDS = (("qc", 384), ("qr", 384), ("gate", LANES), ("mq", 384), ("nsa", 512), ("win", 256),
                     ("moba", 768), ("xpool", POOL_CH))


def _sample_kernel(pt_ref, rows_ref, winst_ref, poolst_ref,
                   wabk_ref, babk_ref, w2k_ref, wabv_ref, babv_ref, w2v_ref, gkc_ref,
                   ovl_ref, gexp_ref, esel_ref, emoba_ref, pw_ref, sc_ref, *rest, n_pages, page, past):
    nsa_pages = rest[:n_pages]
    moba_pages = rest[n_pages:2 * n_pages]
    onsa_ref, omoba_ref, ypool_ref, cmp_ref = rest[2 * n_pages:]
    del pt_ref
    fields, lane0 = {}, 0
    for name, width in SAMPLE_ROW_FIELDS:
        fields[name] = rows_ref[:, lane0:lane0 + width]
        lane0 += width
    row = lax.broadcasted_iota(jnp.int32, (8, 1), 0)
    lane = lax.broadcasted_iota(jnp.int32, (1, LANES), 1)
    lane_h = lane >> 6
    row_h = jnp.where(row >= NSA_GROUP, 1, 0) + jnp.where(row >= 2 * NSA_GROUP, 1, 0)
    row_g = row - row_h * NSA_GROUP
    live6 = row < NSA_HEADS
    n_groups = past // CMP_STRIDE

    def stack_q(q384):
        s = [q384[:, g * LANES:(g + 1) * LANES] for g in range(NSA_GROUP)]
        q = jnp.where(row_g == 0, s[0], jnp.where(row_g == 1, s[1], s[2]))
        return jnp.where(live6 & (row_h == lane_h), q, 0.0)

    def by_group(o):
        return [jnp.sum(jnp.where(live6 & (row == lane_h * NSA_GROUP + g), o, 0.0), axis=0, keepdims=True)
                for g in range(NSA_GROUP)]

    mq = fields["mq"]
    lane3 = lax.broadcasted_iota(jnp.int32, (1, 384), 1)
    mq8 = jnp.where(row == (lane3 >> 6), mq, 0.0)
    mq8b = mq8.astype(BF)
    n_blk = past // MOBA_BLOCK
    n_blk_pad = -(-n_blk // 16) * 16
    k_m = jnp.concatenate([pg[:, 0:384] for pg in moba_pages], axis=0).astype(BF)
    v_m = jnp.concatenate([pg[:, 384:768] for pg in moba_pages], axis=0).astype(BF)
    km = _dot(emoba_ref[0:n_blk_pad, :], k_m) * (1.0 / MOBA_BLOCK)
    km = jnp.concatenate([km.astype(BF), jnp.zeros((LANES - n_blk_pad, 384), BF)], axis=0)
    gate_m = _dot_nt(mq8b, km)
    s_m = _dot_nt(mq8b, k_m)
    cnt = jnp.zeros((8, LANES), F32)
    for d in range(1, n_blk):
        lower = pltpu.roll(gate_m, d, axis=1)
        upper = pltpu.roll(gate_m, LANES - d, axis=1)
        cnt = cnt + jnp.where((lane >= d) & (lower >= gate_m), 1.0, 0.0)
        cnt = cnt + jnp.where((lane + d < n_blk) & (upper > gate_m), 1.0, 0.0)
    pick_m = jnp.where((lane < n_blk) & (cnt < MOBA_TOPK), 1.0, 0.0)
    ok_m = _dot(pick_m.astype(BF), emoba_ref[...]) > 0.5
    moba_new = fields["moba"]
    s = jnp.where(ok_m, s_m, NEG)
    s_new = jnp.sum(mq8 * moba_new[:, 0:384], axis=1, keepdims=True)
    m = jnp.maximum(jnp.max(s, axis=1, keepdims=True), s_new)
    e = jnp.exp2(s - m)
    e_new = jnp.exp2(s_new - m)
    den = jnp.sum(e, axis=1, keepdims=True) + e_new
    o_m = (e_new * moba_new[:, 384:768] + _dot(e.astype(BF), v_m)) / den
    omoba_ref[...] = jnp.sum(jnp.where(row == (lane3 >> 6), o_m, 0.0), axis=0, keepdims=True).astype(BF)

    for pi, pg in enumerate(nsa_pages):
        cmp_ref[0, pi * page:(pi + 1) * page, :] = pg[:, 0:LANES]
        cmp_ref[1, pi * page:(pi + 1) * page, :] = pg[:, LANES:2 * LANES]
    kraw = _compress_tokens(lambda l: cmp_ref[0, pl.ds(l, n_groups, stride=CMP_STRIDE), :],
                            n_groups, wabk_ref, babk_ref, w2k_ref)
    kcmp = _head_norm(kraw, gkc_ref[...], _ones_bd())
    vcmp = _compress_tokens(lambda l: cmp_ref[1, pl.ds(l, n_groups, stride=CMP_STRIDE), :],
                            n_groups, wabv_ref, babv_ref, w2v_ref)

    qc8 = stack_q(fields["qc"])
    c_end = lax.broadcasted_iota(jnp.int32, (1, n_groups), 1) * CMP_STRIDE + CMP_LEN
    p_c = _softmax_rows(_dot3_nt(qc8, kcmp) * SCALE, c_end <= past + 1)
    o_cmp = _dot(p_c.astype(BF), vcmp.astype(BF))
    imp8 = _dot_sel(p_c, ovl_ref[...])
    imp = jnp.sum(jnp.where(live6 & (row_h == lane_h), imp8, 0.0), axis=0, keepdims=True)
    jblk = lane & (SEL_BLOCK - 1)
    own = past // SEL_BLOCK
    sel = _rank_select(jnp.broadcast_to(imp, (8, LANES)), jblk, own, own + 1, SEL_TOPN)[0:1]
    sel8 = jnp.where(live6 & (row_h == lane_h), sel, 0.0)

    qr8 = stack_q(fields["qr"])
    qr8b = qr8.astype(BF)
    nsa_new = fields["nsa"]
    picked = _dot(sel8.astype(BF), esel_ref[...]) > 0.5
    k_sel = jnp.concatenate([pg[:, 2 * LANES:3 * LANES] for pg in nsa_pages], axis=0).astype(BF)
    v_sel = jnp.concatenate([pg[:, 3 * LANES:4 * LANES] for pg in nsa_pages], axis=0).astype(BF)
    s = jnp.where(picked, _dot_nt(qr8b, k_sel), NEG)
    own_picked = jnp.sum(jnp.where(jblk == own, sel8, 0.0), axis=1, keepdims=True) > 0.5
    s_new = jnp.where(own_picked, jnp.sum(qr8 * nsa_new[:, 2 * LANES:3 * LANES], axis=1, keepdims=True), NEG)
    m = jnp.maximum(jnp.maximum(jnp.max(s, axis=1, keepdims=True), s_new), M_INIT)
    e = jnp.exp2(s - m)
    e_new = jnp.exp2(s_new - m)
    den = jnp.sum(e, axis=1, keepdims=True) + e_new
    acc = e_new * nsa_new[:, 3 * LANES:4 * LANES] + _dot(e.astype(BF), v_sel)
    o_sel = jnp.where(den > 0.0, acc / den, 0.0)

    win_new = fields["win"]
    n_win = winst_ref.shape[0]
    kpos_w = past - n_win + lax.broadcasted_iota(jnp.int32, (1, n_win), 1)
    ok_w = (past - kpos_w) < WIN
    s = jnp.where(ok_w, _dot_nt(qr8b, winst_ref[:, 0:LANES].astype(BF)), NEG)
    s_new = jnp.sum(qr8 * win_new[:, 0:LANES], axis=1, keepdims=True)
    m = jnp.maximum(jnp.max(s, axis=1, keepdims=True), s_new)
    e = jnp.exp2(s - m)
    e_new = jnp.exp2(s_new - m)
    den = jnp.sum(e, axis=1, keepdims=True) + e_new
    o_win = (_dot(e.astype(BF), winst_ref[:, LANES:2 * LANES].astype(BF)) + e_new * win_new[:, LANES:2 * LANES]) / den

    gate8 = jnp.broadcast_to(fields["gate"], (8, LANES))
    gh = gate8.astype(BF)
    gl = (gate8 - gh.astype(F32)).astype(BF)
    gexp = (_dot(gh, gexp_ref[...]) + _dot(gl, gexp_ref[...]))[0:1]
    branches = [by_group(o_cmp), by_group(o_sel), by_group(o_win)]
    for g in range(NSA_GROUP):
        out = jnp.zeros((1, LANES), F32)
        for br in range(3):
            out = out + gexp[:, (g * 3 + br) * LANES:(g * 3 + br + 1) * LANES] * branches[br][g]
        onsa_ref[:, g * LANES:(g + 1) * LANES] = out.astype(BF)

    x = fields["xpool"]
    hist = poolst_ref[...]
    hrow = lax.broadcasted_iota(jnp.int32, (POOL_HIST, 1), 0)
    sums = {w: x + jnp.sum(jnp.where(hrow >= POOL_HIST - (w - 1), hist, 0.0), axis=0, keepdims=True)
            for w in POOL_WINDOWS}
    y = _pool_mix(sums, x, jnp.full((1, 1), past + 1, jnp.int32), pw_ref, sc_ref)
    ypool_ref[...] = y.astype(BF)


def _sample(page_table, rows, winst, poolst, cache_nsa, cache_moba, cw, consts, pw, sc, layer, past):
    n_seq, n_pages = page_table.shape
    page = cache_nsa.shape[2]
    per_seq = lambda w: pl.BlockSpec((None, 1, w), lambda i, pt: (i, 0, 0))
    lconst = lambda shape: pl.BlockSpec((None,) + tuple(shape), lambda i, pt: (layer,) + (0,) * len(shape),
                                        pipeline_mode=pl.Buffered(1))
    const = lambda a: pl.BlockSpec(a.shape, lambda i, pt: (0,) * a.ndim, pipeline_mode=pl.Buffered(1))
    cmp_specs = [lconst(s) for s in _cmp_weight_shapes()]

    def page_specs(width):
        def one(pj):
            return pl.BlockSpec((None, None, page, width), lambda i, pt: (layer, pt[i, pj], 0, 0))
        return [one(pj) for pj in range(n_pages)]

    row_all = jnp.concatenate([rows[name].astype(F32) for name, _ in SAMPLE_ROW_FIELDS], axis=1)
    row_w = row_all.shape[1]
    in_specs = ([per_seq(row_w),
                 pl.BlockSpec((None, None, winst.shape[2], 256), lambda i, pt: (layer, i, 0, 0)),
                 pl.BlockSpec((None, None, POOL_HIST, POOL_CH), lambda i, pt: (layer, i, 0, 0))]
                + cmp_specs + cmp_specs + [lconst((1, LANES))]
                + [const(consts["ovl_s"]), const(consts["gexp"]), const(consts["esel"]), const(consts["emoba"]),
                   lconst((POOL_CH, POOL_CH)), lconst((1, POOL_CH))]
                + page_specs(512) + page_specs(768))
    grid_spec = pltpu.PrefetchScalarGridSpec(
        num_scalar_prefetch=1, grid=(n_seq,), in_specs=in_specs,
        out_specs=[per_seq(384), per_seq(384), per_seq(POOL_CH)],
        scratch_shapes=[pltpu.VMEM((2, past, LANES), F32)])
    outs = pl.pallas_call(
        functools.partial(_sample_kernel, n_pages=n_pages, page=page, past=past),
        grid_spec=grid_spec,
        out_shape=[jax.ShapeDtypeStruct((n_seq, 1, 384), BF), jax.ShapeDtypeStruct((n_seq, 1, 384), BF),
                   jax.ShapeDtypeStruct((n_seq, 1, POOL_CH), BF)],
        compiler_params=_cparams("parallel"),
        name="sample_step",
    )(page_table, row_all.reshape(n_seq, 1, row_w), winst, poolst,
      cw["wabk"], cw["babk"], cw["w2k"], cw["wabv"], cw["babv"], cw["w2v"], cw["gkc"],
      consts["ovl_s"], consts["gexp"], consts["esel"], consts["emoba"], pw, sc,
      *([cache_nsa] * n_pages), *([cache_moba] * n_pages))
    return [o.reshape(n_seq, o.shape[2]) for o in outs]


def _combine_ffn_kernel(x_ref, yp_ref, on_ref, om_ref, w_ref, g_ref, wg_ref, wu_ref, wd_ref, o_ref, *, f_chunk):
    mix = jnp.concatenate([yp_ref[...], on_ref[...], om_ref[...]], axis=1)
    o_ref[...] = _ffn_body(x_ref[...] + _dot(mix, w_ref[...]), g_ref[...], wg_ref, wu_ref, wd_ref, f_chunk)


def _combine_ffn(x, ypool, onsa, omoba, w_out, g, wg, wu, wd, layer):
    n, d = x.shape
    d_ff = wg.shape[2]
    tm = _row_tile(n)
    row = lambda w: pl.BlockSpec((tm, w), lambda i: (i, 0))
    return pl.pallas_call(
        functools.partial(_combine_ffn_kernel, f_chunk=_f_chunk(d_ff)),
        grid=(n // tm,),
        in_specs=[row(d), row(POOL_CH), row(384), row(384), _const_spec((POOL_CH + 768, d), layer),
                  _const_spec((1, d), layer),
                  _const_spec((d, d_ff), layer), _const_spec((d, d_ff), layer), _const_spec((d_ff, d), layer)],
        out_specs=row(d),
        out_shape=jax.ShapeDtypeStruct((n, d), F32),
        compiler_params=_cparams("parallel"),
        name="combine_ffn",
    )(x, ypool, onsa, omoba, w_out, g, wg, wu, wd)


def _in_col_perm():
    cols = list(range(0, POOL_CH))
    for g in range(NSA_GROUP):
        for h in range(NSA_KV_HEADS):
            base = POOL_CH + (h * NSA_GROUP + g) * HEAD_DIM
            cols += range(base, base + HEAD_DIM)
    nsa_end = POOL_CH + NSA_HEADS * HEAD_DIM
    kv_end = nsa_end + 6 * NSA_KV_HEADS * HEAD_DIM
    cols += range(nsa_end, kv_end)
    cols += range(kv_end + GATE_COLS, kv_end + GATE_COLS + 3 * MOBA_HEADS * HEAD_DIM)
    gate_cols = list(range(kv_end, kv_end + GATE_COLS))
    return np.array(cols, np.int32), np.array(gate_cols, np.int32)


def _out_row_perm():
    rows = list(range(0, POOL_CH))
    for g in range(NSA_GROUP):
        for h in range(NSA_KV_HEADS):
            base = POOL_CH + (h * NSA_GROUP + g) * HEAD_DIM
            rows += range(base, base + HEAD_DIM)
    rows += range(POOL_CH + NSA_HEADS * HEAD_DIM, POOL_CH + NSA_HEADS * HEAD_DIM + MOBA_HEADS * HEAD_DIM)
    return np.array(rows, np.int32)


def _overlap(n_cmp_rows, n_cmp_valid, n_blocks):
    n = np.arange(n_cmp_rows)
    c_start = n * CMP_STRIDE
    c_end = c_start + CMP_LEN
    jb = np.arange(SEL_BLOCK)
    ov = (c_start[:, None] < (jb[None, :] + 1) * SEL_BLOCK) & (c_end[:, None] > jb[None, :] * SEL_BLOCK)
    ov &= (n[:, None] < n_cmp_valid) & (jb[None, :] < n_blocks)
    return ov.astype(np.float32)


def _constants(t, past):
    g_p = t // CMP_STRIDE
    ov = _overlap(g_p, g_p - 1, t // SEL_BLOCK)
    ovl_p = np.zeros((2 * g_p, LANES), np.float32)
    ovl_p[:g_p, :SEL_BLOCK] = ov
    ovl_p[g_p:, SEL_BLOCK:] = ov
    g_s = past // CMP_STRIDE
    ov_s = _overlap(g_s, g_s - 1, past // SEL_BLOCK + 1)
    ovl_s = np.concatenate([ov_s, ov_s], axis=1)
    lane = np.arange(LANES)
    gexp = np.zeros((LANES, 9 * LANES), np.float32)
    for g in range(NSA_GROUP):
        for br in range(3):
            src = (lane // HEAD_DIM) * 9 + g * 3 + br
            gexp[src, (g * 3 + br) * LANES + lane] = 1.0
    kp = np.arange(past)
    esel = ((lane[:, None] % SEL_BLOCK) == (kp[None, :] // SEL_BLOCK)).astype(np.float32)
    emoba = (lane[:, None] == (kp[None, :] // MOBA_BLOCK)).astype(np.float32)
    kt = np.arange(t)
    et_sel = ((kt[:, None] // SEL_BLOCK) == (lane[None, :] % SEL_BLOCK)).astype(np.float32)
    et_moba = ((kt[:, None] // MOBA_BLOCK) == (lane[None, :] % 16)).astype(np.float32)
    as_bf = lambda a: jnp.asarray(a, BF)
    return {"ovlt_p": as_bf(ovl_p.T), "ovl_s": as_bf(ovl_s), "gexp": as_bf(gexp), "esel": as_bf(esel),
            "emoba": as_bf(emoba), "et_sel": as_bf(et_sel), "et_moba": as_bf(et_moba)}


def _rope_tables(pos):
    half = ROPE_DIM // 2
    inv_freq = ROPE_THETA ** (-jnp.arange(half, dtype=F32) / half)
    ang = pos.astype(F32)[:, None] * inv_freq[None, :]
    cos, sin, zero = jnp.cos(ang), jnp.sin(ang), jnp.zeros_like(ang)
    rest = HEAD_DIM - ROPE_DIM
    c = jnp.concatenate([cos, cos, jnp.ones((ang.shape[0], rest), F32)], axis=1)
    a = jnp.concatenate([-sin, zero, jnp.zeros((ang.shape[0], rest), F32)], axis=1)
    b = jnp.concatenate([zero, sin, jnp.zeros((ang.shape[0], rest), F32)], axis=1)
    tile2 = lambda m: jnp.concatenate([m, m], axis=1)
    return tile2(c), tile2(a), tile2(b)


def _cmp_weights(cmp_pos, cmp_w1, cmp_w2, g_nsa_kc):
    bd = lambda w: jnp.concatenate([jnp.concatenate([w, jnp.zeros_like(w)], axis=-1),
                                    jnp.concatenate([jnp.zeros_like(w), w], axis=-1)], axis=-2)
    out = {}
    depth = cmp_pos.shape[0]
    for i, name in enumerate(("k", "v")):
        pos2 = jnp.concatenate([cmp_pos[:, i], cmp_pos[:, i]], axis=-1)
        w1 = bd(cmp_w1[:, i])
        out["w2" + name] = bd(cmp_w2[:, i]).astype(BF)
        halves = [w1[:, s:s + CMP_STRIDE].reshape(depth, CMP_STRIDE * LANES, LANES) for s in (0, CMP_STRIDE)]
        out["wab" + name] = jnp.concatenate(halves, axis=-1).astype(BF)
        bias = [jnp.einsum("xlc,xlce->xe", pos2[:, s:s + CMP_STRIDE], w1[:, s:s + CMP_STRIDE],
                           precision=lax.Precision.HIGHEST) for s in (0, CMP_STRIDE)]
        out["bab" + name] = jnp.concatenate(bias, axis=-1)[:, None, :]
    out["gkc"] = jnp.concatenate([g_nsa_kc, g_nsa_kc], axis=-1)[:, None, :]
    return out


def kernel(x_prompt, x_sample, cache_nsa_kv, cache_moba_kv, state_nsa_win, state_pool, page_table, g_ffa, w_ffa_gate, w_ffa_up, w_ffa_down, g_mix, w_in, w_out, pool_w, pool_scale, g_nsa_q, g_nsa_kc, g_nsa_ks, g_nsa_kw, cmp_pos, cmp_w1, cmp_w2, g_moba_q, g_moba_k, g_ffb, w_ffb_gate, w_ffb_up, w_ffb_down):
    batch, t, d = x_prompt.shape
    n_seq = x_sample.shape[0]
    depth, n_pool, page = cache_nsa_kv.shape[:3]
    n_pages = page_table.shape[1]
    past = n_pages * page
    n_p = batch * t
    n_win = state_nsa_win.shape[2]
    assert x_sample.shape[1] == 1 and t % MOBA_BLOCK == 0 and past % MOBA_BLOCK == 0 and n_win == WIN

    tp = _row_tile(t)
    tq_nsa = 256
    tq_moba = min(2 * MOBA_BLOCK, t)

    col_perm, gate_cols = _in_col_perm()
    w_in_p = jnp.concatenate([w_in[:, :, col_perm], w_in[:, :, gate_cols],
                              jnp.zeros((depth, d, LANES - GATE_COLS), w_in.dtype)], axis=-1).astype(BF)
    w_out_p = w_out[:, _out_row_perm(), :].astype(BF)
    bf = lambda w: w.astype(BF)
    wa = (bf(w_ffa_gate), bf(w_ffa_up), bf(w_ffa_down))
    wb = (bf(w_ffb_gate), bf(w_ffb_up), bf(w_ffb_down))
    tile2 = lambda g: jnp.concatenate([g, g], axis=-1)
    gains = jnp.stack([tile2(g_nsa_q), tile2(g_nsa_ks), tile2(g_nsa_kw), tile2(g_moba_q), tile2(g_moba_k)]
                      + [jnp.ones((depth, LANES), F32)] * 3, axis=1)
    cw = _cmp_weights(cmp_pos, cmp_w1, cmp_w2, g_nsa_kc)
    eye = jnp.eye(len(POOL_WINDOWS), dtype=pool_w.dtype)
    pw_bd = jnp.einsum("lgij,gh->lgihj", pool_w, eye).reshape(depth, POOL_CH, POOL_CH).astype(BF)
    sc = pool_scale[:, None, :]
    consts = _constants(t, past)
    rope_p = _rope_tables(jnp.arange(t, dtype=jnp.int32))
    rope_s = _rope_tables(jnp.full((n_seq,), past, jnp.int32))
    g_a, g_m, g_b = g_ffa[:, None, :], g_mix[:, None, :], g_ffb[:, None, :]

    cache_nsa = cache_nsa_kv.reshape(depth, n_pool, page, 512)
    cache_moba = cache_moba_kv.reshape(depth, n_pool, page, 768)
    winst = state_nsa_win.reshape(depth, n_seq, n_win, 256)

    xp = x_prompt.reshape(n_p, d)
    xs = x_sample.reshape(n_seq, d)
    names = ("xpool", "qc", "qr", "nsa", "nsab", "win", "winb", "mq", "moba", "mobab", "gate")
    outs = [[] for _ in range(8)]
    for l in range(depth):
        xp = _ffn(xp, g_a, *wa, l)
        xs = _ffn(xs, g_a, *wa, l)
        pp = dict(zip(names, _prep(xp, g_m, w_in_p, gains, rope_p, l)))
        ps = dict(zip(names, _prep(xs, g_m, w_in_p, gains, rope_s, l)))

        kcmp, vcmp = _compress_prompt(pp["nsa"], cw, l, batch, t)
        onsa_p = _nsa_prompt(pp["qc"], pp["qr"], pp["gate"], kcmp, vcmp, pp["nsab"], pp["winb"], consts["ovlt_p"],
                             consts["gexp"], consts["et_sel"], batch, t, tq_nsa)
        omoba_p = _moba_prompt(pp["mq"], pp["mobab"], consts["et_moba"], batch, t, tq_moba)
        ypool_p = _pool_prompt(pp["xpool"], pw_bd, sc, l, batch, t, tp)
        onsa_s, omoba_s, ypool_s = _sample(page_table, ps, winst, state_pool, cache_nsa, cache_moba, cw, consts,
                                           pw_bd, sc, l, past)

        xp = _combine_ffn(xp, ypool_p, onsa_p, omoba_p, w_out_p, g_b, *wb, l)
        xs = _combine_ffn(xs, ypool_s, onsa_s, omoba_s, w_out_p, g_b, *wb, l)

        outs[0].append(pp["nsa"].reshape(batch, t, 4, NSA_KV_HEADS, HEAD_DIM))
        outs[1].append(ps["nsa"].reshape(n_seq, 1, 4, NSA_KV_HEADS, HEAD_DIM))
        outs[2].append(pp["moba"].reshape(batch, t, 2, MOBA_HEADS, HEAD_DIM))
        outs[3].append(ps["moba"].reshape(n_seq, 1, 2, MOBA_HEADS, HEAD_DIM))
        outs[4].append(pp["win"].reshape(batch, t, 2, NSA_KV_HEADS, HEAD_DIM)[:, t - min(WIN, t):])
        outs[5].append(ps["win"].reshape(n_seq, 1, 2, NSA_KV_HEADS, HEAD_DIM))
        outs[6].append(pp["xpool"].reshape(batch, t, POOL_CH)[:, t - POOL_HIST:])
        outs[7].append(ps["xpool"].reshape(n_seq, 1, POOL_CH))

    stacked = [jnp.stack(o, axis=0) for o in outs]
    stacked[5] = jnp.concatenate([state_nsa_win[:, :, 1:], stacked[5]], axis=2)
    stacked[7] = jnp.concatenate([state_pool[:, :, 1:], stacked[7]], axis=2)
    return (xp.reshape(batch, t, d), xs.reshape(n_seq, 1, d), *stacked)
```

```python
import functools

import numpy as np
import jax
import jax.numpy as jnp
from jax import lax
from jax.experimental import pallas as pl
from jax.experimental.pallas import tpu as pltpu

F32 = jnp.float32
BF = jnp.bfloat16

HEAD_DIM = 64
LANES = 128
POOL_CH = 256
POOL_WINDOWS = (2, 4, 8, 16)
POOL_HIST = 15
NSA_KV_HEADS = 2
NSA_GROUP = 3
NSA_HEADS = 6
MOBA_HEADS = 6
CMP_LEN = 32
CMP_STRIDE = 16
SEL_BLOCK = 64
SEL_TOPN = 16
WIN = 512
MOBA_BLOCK = 256
MOBA_TOPK = 3
ROPE_THETA = 500000.0
ROPE_DIM = 16
EPS = 1e-6
NEG = -1e30
M_INIT = -1e29
SEL_FORCE = 1e4
SCALE = HEAD_DIM ** -0.5
Q_PRESCALE = SCALE * 1.4426950408889634
GATE_COLS = 3 * NSA_HEADS
D_QKV = 2560
D_IN_PAD = D_QKV + LANES
VMEM_LIMIT = 56 * 1024 * 1024


def _cparams(*sem):
    return pltpu.CompilerParams(dimension_semantics=sem, vmem_limit_bytes=VMEM_LIMIT)


def _dot(a, b):
    return jnp.dot(a, b, preferred_element_type=F32)


def _dot_nt(a, b):
    return lax.dot_general(a, b, (((1,), (1,)), ((), ())), preferred_element_type=F32)


def _split3(x):
    hi = x.astype(BF)
    r = x - hi.astype(F32)
    mid = r.astype(BF)
    lo = (r - mid.astype(F32)).astype(BF)
    return hi, mid, lo


def _dot_sel(x, m):
    hi, mid, lo = _split3(x)
    return _dot(hi, m) + _dot(mid, m) + _dot(lo, m)


def _dot3_nt(a, b):
    ah = a.astype(BF)
    al = (a - ah.astype(F32)).astype(BF)
    bh = b.astype(BF)
    bl = (b - bh.astype(F32)).astype(BF)
    return _dot_nt(ah, bh) + _dot_nt(ah, bl) + _dot_nt(al, bh)


def _sigmoid(x):
    return 1.0 / (1.0 + jnp.exp(-x))


def _silu(x):
    return x * _sigmoid(x)


def _rms_rows(x, g):
    return x * lax.rsqrt(jnp.mean(x * x, axis=-1, keepdims=True) + EPS) * g


def _ones_bd():
    r = lax.broadcasted_iota(jnp.int32, (LANES, LANES), 0) >> 6
    c = lax.broadcasted_iota(jnp.int32, (LANES, LANES), 1) >> 6
    return jnp.where(r == c, 1.0, 0.0).astype(BF)


def _head_norm(x, gain, ones_bd):
    ms = _dot_sel(x * x, ones_bd) * (1.0 / HEAD_DIM)
    return x * lax.rsqrt(ms + EPS) * gain


def _rope(x, c, a, b):
    return x * c + pltpu.roll(x, LANES - ROPE_DIM // 2, axis=1) * a + pltpu.roll(x, ROPE_DIM // 2, axis=1) * b


def _softmax_rows(s, valid):
    s = jnp.where(valid, s, NEG)
    m = jnp.max(s, axis=1, keepdims=True)
    e = jnp.where(valid, jnp.exp(s - m), 0.0)
    den = jnp.sum(e, axis=1, keepdims=True)
    return jnp.where(den > 0.0, e / den, 0.0)


def _sum_list(xs):
    return functools.reduce(lambda a, b: a + b, xs)


def _ffn_body(x, g, wg_ref, wu_ref, wd_ref, f_chunk):
    xn = _rms_rows(x, g).astype(BF)
    d_ff = wg_ref.shape[1]
    acc = jnp.zeros(x.shape, F32)
    for c in range(d_ff // f_chunk):
        sl = slice(c * f_chunk, (c + 1) * f_chunk)
        gate = _dot(xn, wg_ref[:, sl])
        up = _dot(xn, wu_ref[:, sl])
        h = (_silu(gate) * up).astype(BF)
        acc = acc + _dot(h, wd_ref[sl, :])
    return x + 0.5 * acc


def _ffn_kernel(x_ref, g_ref, wg_ref, wu_ref, wd_ref, o_ref, *, f_chunk):
    o_ref[...] = _ffn_body(x_ref[...], g_ref[...], wg_ref, wu_ref, wd_ref, f_chunk)


def _f_chunk(d_ff):
    for c in (1408, 1024, 512, 256, 128):
        if d_ff % c == 0:
            return c
    return d_ff


def _row_tile(n):
    return next((c for c in (512, 384, 256, 128) if n % c == 0), n)


def _const_spec(shape, layer):
    nd = len(shape)
    return pl.BlockSpec((None,) + tuple(shape), lambda *_: (layer,) + (0,) * nd, pipeline_mode=pl.Buffered(1))


def _ffn(x, g, wg, wu, wd, layer):
    n, d = x.shape
    d_ff = wg.shape[2]
    tm = _row_tile(n)
    return pl.pallas_call(
        functools.partial(_ffn_kernel, f_chunk=_f_chunk(d_ff)),
        grid=(n // tm,),
        in_specs=[pl.BlockSpec((tm, d), lambda i: (i, 0)),
                  _const_spec((1, d), layer),
                  _const_spec((d, d_ff), layer), _const_spec((d, d_ff), layer), _const_spec((d_ff, d), layer)],
        out_specs=pl.BlockSpec((tm, d), lambda i: (i, 0)),
        out_shape=jax.ShapeDtypeStruct((n, d), F32),
        compiler_params=_cparams("parallel"),
        name="ffn",
    )(x, g, wg, wu, wd)


def _prep_kernel(x_ref, g_ref, w_ref, gains_ref, c_ref, a_ref, b_ref,
                 xpool_ref, qc_ref, qr_ref, nsa_ref, nsab_ref, win_ref, winb_ref,
                 mq_ref, moba_ref, mobab_ref, gate_ref):
    xn = _rms_rows(x_ref[...], g_ref[...]).astype(BF)
    z = _dot(xn, w_ref[...])
    c, a, b = c_ref[...], a_ref[...], b_ref[...]
    gains = gains_ref[...]
    slab = lambda i: z[:, i * LANES:(i + 1) * LANES]

    normed_gain = {2: 0, 3: 0, 4: 0, 7: 1, 9: 2, 11: 3, 12: 3, 13: 3, 14: 4, 15: 4, 16: 4}
    r2 = lax.broadcasted_iota(jnp.int32, (2 * LANES, 2 * LANES), 0) >> 6
    c2 = lax.broadcasted_iota(jnp.int32, (2 * LANES, 2 * LANES), 1) >> 6
    ones_bd2 = jnp.where(r2 == c2, 1.0, 0.0).astype(BF)
    idxs = list(normed_gain)
    normed = {}
    for p0 in range(0, len(idxs), 2):
        pair = idxs[p0:p0 + 2]
        sq = jnp.concatenate([slab(i) * slab(i) for i in pair], axis=1)
        hi = sq.astype(BF)
        lo = (sq - hi.astype(F32)).astype(BF)
        w = ones_bd2 if len(pair) == 2 else ones_bd2[0:LANES, 0:LANES]
        ms = (_dot(hi, w) + _dot(lo, w)) * (1.0 / HEAD_DIM)
        for k, i in enumerate(pair):
            gi = normed_gain[i]
            normed[i] = slab(i) * lax.rsqrt(ms[:, k * LANES:(k + 1) * LANES] + EPS) * gains[gi:gi + 1]
    norm_rope = lambda i: _rope(normed[i], c, a, b)

    xpool_ref[...] = z[:, 0:POOL_CH]
    for gi in range(3):
        qc_ref[:, gi * LANES:(gi + 1) * LANES] = normed[2 + gi]
        qr_ref[:, gi * LANES:(gi + 1) * LANES] = (norm_rope(2 + gi) * Q_PRESCALE).astype(BF)
    nsa = [slab(5), slab(6), norm_rope(7), slab(8)]
    for i, s in enumerate(nsa):
        nsa_ref[:, i * LANES:(i + 1) * LANES] = s
        nsab_ref[:, i * LANES:(i + 1) * LANES] = s.astype(BF)
    win = [norm_rope(9), slab(10)]
    for i, s in enumerate(win):
        win_ref[:, i * LANES:(i + 1) * LANES] = s
        winb_ref[:, i * LANES:(i + 1) * LANES] = s.astype(BF)
    for i in range(3):
        mq_ref[:, i * LANES:(i + 1) * LANES] = (norm_rope(11 + i) * Q_PRESCALE).astype(BF)
    for i in range(6):
        s = norm_rope(14 + i) if i < 3 else slab(14 + i)
        moba_ref[:, i * LANES:(i + 1) * LANES] = s
        mobab_ref[:, i * LANES:(i + 1) * LANES] = s.astype(BF)
    gate_ref[...] = _sigmoid(slab(20))


def _prep(x, g, w_in, gains, rope, layer):
    n, d = x.shape
    tm = _row_tile(min(n, rope[0].shape[0]))
    nb = rope[0].shape[0] // tm
    row = lambda w: pl.BlockSpec((tm, w), lambda i: (i, 0))
    tab = pl.BlockSpec((tm, LANES), lambda i: (i % nb, 0))
    widths_dtypes = [(POOL_CH, F32), (384, F32), (384, BF), (512, F32), (512, BF), (256, F32), (256, BF),
                     (384, BF), (768, F32), (768, BF), (LANES, F32)]
    return pl.pallas_call(
        _prep_kernel,
        grid=(n // tm,),
        in_specs=[row(d), _const_spec((1, d), layer), _const_spec((d, D_IN_PAD), layer),
                  _const_spec((8, LANES), layer), tab, tab, tab],
        out_specs=[row(w) for w, _ in widths_dtypes],
        out_shape=[jax.ShapeDtypeStruct((n, w), dt) for w, dt in widths_dtypes],
        compiler_params=_cparams("parallel"),
        name="prep",
    )(x, g, w_in, gains, *rope)


def _compress_tokens(load_rows, n_groups, wab_ref, bab_ref, w2_ref):
    x = jnp.concatenate([load_rows(l) for l in range(CMP_STRIDE)], axis=1).astype(BF)
    pab = _dot(x, wab_ref[...]) + bab_ref[...]
    pre = pab[:, 0:LANES] + pltpu.roll(pab[:, LANES:2 * LANES], n_groups - 1, axis=0)
    return _dot(_silu(pre).astype(BF), w2_ref[...])


def _compress_prompt_kernel(kc_ref, vc_ref, wabk_ref, babk_ref, w2k_ref, wabv_ref, babv_ref, w2v_ref, gkc_ref,
                            kcmp_ref, vcmp_ref, *, n_groups):
    kraw = _compress_tokens(lambda l: kc_ref[pl.ds(l, n_groups, stride=CMP_STRIDE), :],
                            n_groups, wabk_ref, babk_ref, w2k_ref)
    kcmp_ref[...] = _head_norm(kraw, gkc_ref[...], _ones_bd())
    vcmp_ref[...] = _compress_tokens(lambda l: vc_ref[pl.ds(l, n_groups, stride=CMP_STRIDE), :],
                                     n_groups, wabv_ref, babv_ref, w2v_ref)


def _cmp_weight_shapes():
    return [(CMP_STRIDE * LANES, 2 * LANES), (1, 2 * LANES), (LANES, LANES)]


def _compress_prompt(nsa_rows, cw, layer, batch, t):
    n_groups = t // CMP_STRIDE
    wspecs = [_const_spec(s, layer) for s in _cmp_weight_shapes()]
    return pl.pallas_call(
        functools.partial(_compress_prompt_kernel, n_groups=n_groups),
        grid=(batch,),
        in_specs=[pl.BlockSpec((t, LANES), lambda b: (b, 0)), pl.BlockSpec((t, LANES), lambda b: (b, 1))]
                 + wspecs + wspecs + [_const_spec((1, LANES), layer)],
        out_specs=[pl.BlockSpec((n_groups, LANES), lambda b: (b, 0))] * 2,
        out_shape=[jax.ShapeDtypeStruct((batch * n_groups, LANES), F32)] * 2,
        compiler_params=_cparams("parallel"),
        name="compress_prompt",
    )(nsa_rows, nsa_rows, cw["wabk"], cw["babk"], cw["w2k"], cw["wabv"], cw["babv"], cw["w2v"], cw["gkc"])


def _rank_select(imp, j, own, n_blocks, top_n):
    forced = (j == 0) | (j == own) | (j == own - 1)
    imp = jnp.where(forced, imp + SEL_FORCE, imp)
    imp = jnp.where(j <= own, imp, -jnp.inf)
    cnt = jnp.zeros(imp.shape, F32)
    for d in range(1, n_blocks):
        lower = pltpu.roll(imp, d, axis=1)
        upper = pltpu.roll(imp, LANES - d, axis=1)
        cnt = cnt + jnp.where((j >= d) & (lower >= imp), 1.0, 0.0)
        cnt = cnt + jnp.where((j + d < n_blocks) & (upper > imp), 1.0, 0.0)
    return jnp.where(j <= own, jnp.where(cnt < top_n, 1.0, 0.0), 0.0)


def _online_update(s, v, m_ref, l_ref, acc_ref, idx):
    m_prev = m_ref[idx]
    m_new = jnp.maximum(m_prev, jnp.max(s, axis=1, keepdims=True))
    alpha = jnp.exp2(m_prev - m_new)
    ps = [jnp.exp2((s[:, i * LANES:(i + 1) * LANES] - m_new).astype(BF)) for i in range(s.shape[1] // LANES)]
    l_ref[idx] = alpha * l_ref[idx] + _sum_list(ps).astype(F32)
    acc_ref[idx] = alpha * acc_ref[idx] + _dot(jnp.concatenate(ps, axis=1), v)
    m_ref[idx] = m_new


def _online_init(m_ref, l_ref, acc_ref):
    m_ref[...] = jnp.full(m_ref.shape, M_INIT, F32)
    l_ref[...] = jnp.zeros(l_ref.shape, F32)
    acc_ref[...] = jnp.zeros(acc_ref.shape, F32)


def _online_out(l_ref, acc_ref, idx):
    l = jnp.sum(l_ref[idx], axis=1, keepdims=True)
    return jnp.where(l > 0.0, acc_ref[idx] / l, 0.0)


def _nsa_prompt_kernel(qc_ref, qr_ref, gate_ref, kcmp_ref, vcmp_ref, nsab_ref, winb_ref, ovlt_ref, gexp_ref, et_ref,
                       o_ref, impt_ref, m_ref, l_ref, acc_ref, *, tq, tk):
    qi = pl.program_id(1)
    q0 = qi * tq
    lane_h = lax.broadcasted_iota(jnp.int32, (1, LANES), 1) >> 6
    qpos = q0 + lax.broadcasted_iota(jnp.int32, (tq, 1), 0)
    qpos6 = jnp.concatenate([qpos] * NSA_HEADS, axis=0)

    def stack6(tile, zero):
        return jnp.concatenate([jnp.where(lane_h == h, tile(g), zero)
                                for h in range(NSA_KV_HEADS) for g in range(NSA_GROUP)], axis=0)

    kc = kcmp_ref[...]
    n_cmp = kc.shape[0]
    c_end = lax.broadcasted_iota(jnp.int32, (1, n_cmp), 1) * CMP_STRIDE + CMP_LEN
    q_cmp = stack6(lambda g: qc_ref[:, g * LANES:(g + 1) * LANES], 0.0)
    p_c = _softmax_rows(_dot3_nt(q_cmp, kc) * SCALE, c_end <= qpos6 + 1)
    o_cmp = _dot(p_c.astype(BF), vcmp_ref[...].astype(BF))
    psum = [_sum_list([p_c[(h * NSA_GROUP + g) * tq:(h * NSA_GROUP + g + 1) * tq] for g in range(NSA_GROUP)])
            for h in range(NSA_KV_HEADS)]
    hi, mid, lo = _split3(jnp.concatenate(psum, axis=1))
    ovlt = ovlt_ref[...]
    impt = _dot_nt(ovlt, hi) + _dot_nt(ovlt, mid) + _dot_nt(ovlt, lo)
    brow = lax.broadcasted_iota(jnp.int32, (LANES, 1), 0)
    jrow = brow & (SEL_BLOCK - 1)
    own_l = (q0 + lax.broadcasted_iota(jnp.int32, (1, tq), 1)) >> 6
    forced = (jrow == 0) | (jrow == own_l) | (jrow == own_l - 1)
    impt = jnp.where(forced, impt + SEL_FORCE, impt)
    impt = jnp.where(jrow <= own_l, impt, -jnp.inf)
    impt_ref[...] = impt

    def beats(jp):
        comp = jnp.where(brow < SEL_BLOCK, impt_ref[pl.ds(jp, 1), :], impt_ref[pl.ds(SEL_BLOCK + jp, 1), :])
        return jnp.where(jrow > jp, jnp.where(comp >= impt, 1.0, 0.0), jnp.where(comp > impt, 1.0, 0.0))

    n_live = (q0 + tq - 1) // SEL_BLOCK + 1
    cnt = lax.fori_loop(0, (n_live + 1) // 2, lambda i, cnt: cnt + beats(2 * i) + beats(2 * i + 1),
                        jnp.zeros((LANES, tq), F32))
    sel_t = jnp.where(jrow <= own_l, jnp.where(cnt < SEL_TOPN, 1.0, 0.0), 0.0)
    sel_neg = jnp.where(sel_t.T > 0.5, 0.0, NEG)

    zero = jnp.zeros((), BF)
    q_rot = stack6(lambda g: qr_ref[:, g * LANES:(g + 1) * LANES], zero)
    q_aug = jnp.concatenate([q_rot, stack6(lambda g: sel_neg, 0.0).astype(BF)], axis=1)
    _online_init(m_ref, l_ref, acc_ref)

    def sel_chunk(start, width, causal):
        k_aug = jnp.concatenate([nsab_ref[pl.ds(start, width), 2 * LANES:3 * LANES],
                                 et_ref[pl.ds(start, width), :]], axis=1)
        v = nsab_ref[pl.ds(start, width), 3 * LANES:4 * LANES]
        s = _dot_nt(q_aug, k_aug)
        if causal:
            kpos = start + lax.broadcasted_iota(jnp.int32, (1, width), 1)
            s = jnp.where(kpos <= qpos6, s, NEG)
        _online_update(s, v, m_ref, l_ref, acc_ref, 0)

    c_last = (q0 + tq - 1) // tk
    lax.fori_loop(0, c_last, lambda c, carry: (sel_chunk(pl.multiple_of(c * tk, tk), tk, False), carry)[1], 0)
    d_start = pl.multiple_of(c_last * tk, tk)
    d_tiles = (q0 + tq - c_last * tk) // tq
    for r in range(1, tk // tq + 1):
        @pl.when(d_tiles == r)
        def _():
            sel_chunk(d_start, r * tq, True)
    o_sel = _online_out(l_ref, acc_ref, 0)

    n_slots = WIN // tq + 1
    slot_c = [qi - (n_slots - 1) + slot for slot in range(n_slots)]
    starts = [pl.multiple_of(jnp.maximum(c, 0) * tq, tq) for c in slot_c]
    k_band = jnp.concatenate([winb_ref[pl.ds(st, tq), 0:LANES] for st in starts], axis=0)
    v_band = jnp.concatenate([winb_ref[pl.ds(st, tq), LANES:2 * LANES] for st in starts], axis=0)
    s_band = _dot_nt(q_rot, k_band)
    s_w = []
    for slot in range(n_slots):
        s = s_band[:, slot * tq:(slot + 1) * tq]
        kpos = starts[slot] + lax.broadcasted_iota(jnp.int32, (1, tq), 1)
        if slot == 0:
            s = jnp.where(kpos > qpos6 - WIN, s, NEG)
        if slot == n_slots - 1:
            s = jnp.where(kpos <= qpos6, s, NEG)
        else:
            s = jnp.where(slot_c[slot] >= 0, s, NEG)
        s_w.append(s)
    m = jnp.max(functools.reduce(jnp.maximum, s_w), axis=1, keepdims=True)
    ps = [jnp.exp2(s - m) for s in s_w]
    l = jnp.sum(_sum_list(ps), axis=1, keepdims=True)
    o_win = _dot(jnp.concatenate(ps, axis=1).astype(BF), v_band) / l

    gate = gate_ref[...]
    gh = gate.astype(BF)
    gl = (gate - gh.astype(F32)).astype(BF)
    gexp = _dot(gh, gexp_ref[...]) + _dot(gl, gexp_ref[...])
    for g in range(NSA_GROUP):
        r0, r1 = g * tq, (NSA_GROUP + g) * tq
        out = jnp.zeros((tq, LANES), F32)
        for br, o in enumerate((o_cmp, o_sel, o_win)):
            both = jnp.where(lane_h == 0, o[r0:r0 + tq], o[r1:r1 + tq])
            out = out + gexp[:, (g * 3 + br) * LANES:(g * 3 + br + 1) * LANES] * both
        o_ref[:, g * LANES:(g + 1) * LANES] = out.astype(BF)


def _nsa_prompt(qc, qr, gate, kcmp, vcmp, nsab, winb, ovlt, gexp, et, batch, t, tq):
    nq = t // tq
    n_groups = t // CMP_STRIDE
    assert WIN % tq == 0 and t // SEL_BLOCK <= SEL_BLOCK
    tile = lambda w: pl.BlockSpec((tq, w), lambda b, i: (b * nq + i, 0))
    per_b = lambda r, w: pl.BlockSpec((r, w), lambda b, i: (b, 0))
    const = lambda a: pl.BlockSpec(a.shape, lambda b, i: (0, 0), pipeline_mode=pl.Buffered(1))
    return pl.pallas_call(
        functools.partial(_nsa_prompt_kernel, tq=tq, tk=min(512, t)),
        grid=(batch, nq),
        in_specs=[tile(384), tile(384), tile(LANES), per_b(n_groups, LANES), per_b(n_groups, LANES),
                  per_b(t, 512), per_b(t, 256), const(ovlt), const(gexp), const(et)],
        out_specs=tile(384),
        out_shape=jax.ShapeDtypeStruct((batch * t, 384), BF),
        scratch_shapes=[pltpu.VMEM((LANES, tq), F32)] + [pltpu.VMEM((1, NSA_HEADS * tq, LANES), F32)] * 3,
        compiler_params=_cparams("parallel", "parallel"),
        name="nsa_prompt",
    )(qc, qr, gate, kcmp, vcmp, nsab, winb, ovlt, gexp, et)


def _moba_prompt_kernel(mq_ref, mobab_ref, et_ref, o_ref, kmbd_ref, gate_ref, m_ref, l_ref, acc_ref, *, tq, t, tk):
    qi = pl.program_id(1)
    q0 = qi * tq
    n_pairs = MOBA_HEADS // 2
    lane = lax.broadcasted_iota(jnp.int32, (1, LANES), 1)
    lane_h = lane >> 6
    hh = lane >> 4
    qpos = q0 + lax.broadcasted_iota(jnp.int32, (tq, 1), 0)
    qpos2 = jnp.concatenate([qpos, qpos], axis=0)
    brow = lax.broadcasted_iota(jnp.int32, (LANES, 1), 0)
    jrow = brow & 15

    @pl.when(qi == 0)
    def _():
        tt = lax.broadcasted_iota(jnp.int32, (1, t), 1) >> 8
        ind = jnp.where(jrow == tt, 1.0, 0.0).astype(BF)
        km = _dot(ind, mobab_ref[:, 0:384]) * (1.0 / MOBA_BLOCK)
        col_h = lax.broadcasted_iota(jnp.int32, (1, 384), 1) >> 6
        kmbd_ref[...] = jnp.where(col_h == (brow >> 4), km, 0.0).astype(BF)

    gate = _dot_nt(kmbd_ref[...], mq_ref[...])
    gate_ref[...] = gate
    own_l = (q0 + lax.broadcasted_iota(jnp.int32, (1, tq), 1)) >> 8

    def rank_step(jp, cnt):
        comp = jnp.concatenate([jnp.broadcast_to(gate_ref[pl.ds(hd * 16 + jp, 1), :], (16, tq))
                                for hd in range(LANES // 16)], axis=0)
        beats = jnp.where(jrow > jp, jnp.where(comp >= gate, 1.0, 0.0), jnp.where(comp > gate, 1.0, 0.0))
        return cnt + jnp.where(jp < own_l, beats, 0.0)

    n_past = (q0 + tq - 1) // MOBA_BLOCK
    cnt = lax.fori_loop(0, n_past, rank_step, jnp.zeros((LANES, tq), F32))
    picked_t = jnp.where(jrow == own_l, 1.0, jnp.where(jrow < own_l, jnp.where(cnt < MOBA_TOPK, 1.0, 0.0), 0.0))
    pick_neg = jnp.where(picked_t.T > 0.5, 0.0, NEG)

    zero = jnp.zeros((), BF)
    q_aug = []
    for p in range(n_pairs):
        qp = mq_ref[:, p * LANES:(p + 1) * LANES]
        q2 = jnp.concatenate([jnp.where(lane_h == 0, qp, zero), jnp.where(lane_h == 1, qp, zero)], axis=0)
        pk = jnp.concatenate([jnp.where(hh == 2 * p, pick_neg, 0.0), jnp.where(hh == 2 * p + 1, pick_neg, 0.0)], axis=0)
        q_aug.append(jnp.concatenate([q2, pk.astype(BF)], axis=1))
    _online_init(m_ref, l_ref, acc_ref)

    def chunk(c, causal):
        start = pl.multiple_of(c * tk, tk)
        et = et_ref[pl.ds(start, tk), :]
        for p in range(n_pairs):
            k_aug = jnp.concatenate([mobab_ref[pl.ds(start, tk), p * LANES:(p + 1) * LANES], et], axis=1)
            v = mobab_ref[pl.ds(start, tk), 384 + p * LANES:384 + (p + 1) * LANES]
            s = _dot_nt(q_aug[p], k_aug)
            if causal:
                kpos = start + lax.broadcasted_iota(jnp.int32, (1, tk), 1)
                s = jnp.where(kpos <= qpos2, s, NEG)
            _online_update(s, v, m_ref, l_ref, acc_ref, p)

    c_last = (q0 + tq - 1) // tk
    lax.fori_loop(0, c_last, lambda c, carry: (chunk(c, False), carry)[1], 0)
    chunk(c_last, True)
    for p in range(n_pairs):
        out = _online_out(l_ref, acc_ref, p)
        o_ref[:, p * LANES:(p + 1) * LANES] = jnp.where(lane_h == 0, out[0:tq], out[tq:2 * tq]).astype(BF)


def _moba_prompt(mq, mobab, et, batch, t, tq):
    nq = t // tq
    assert t // MOBA_BLOCK <= 16
    tile = lambda w: pl.BlockSpec((tq, w), lambda b, i: (b * nq + i, 0))
    return pl.pallas_call(
        functools.partial(_moba_prompt_kernel, tq=tq, t=t, tk=min(512, t)),
        grid=(batch, nq),
        in_specs=[tile(384), pl.BlockSpec((t, 768), lambda b, i: (b, 0)),
                  pl.BlockSpec(et.shape, lambda b, i: (0, 0), pipeline_mode=pl.Buffered(1))],
        out_specs=tile(384),
        out_shape=jax.ShapeDtypeStruct((batch * t, 384), BF),
        scratch_shapes=[pltpu.VMEM((LANES, 384), BF), pltpu.VMEM((LANES, tq), F32)]
                       + [pltpu.VMEM((MOBA_HEADS // 2, 2 * tq, LANES), F32)] * 3,
        compiler_params=_cparams("parallel", "arbitrary"),
        name="moba_prompt",
    )(mq, mobab, et)


def _pool_mix(win_sums, x, count_pos, pw_ref, sc_ref):
    lane_g = lax.broadcasted_iota(jnp.int32, (1, POOL_CH), 1) >> 6
    s = win_sums[POOL_WINDOWS[-1]]
    w = jnp.full((1, POOL_CH), POOL_WINDOWS[-1], jnp.int32)
    for gi in range(len(POOL_WINDOWS) - 2, -1, -1):
        s = jnp.where(lane_g == gi, win_sums[POOL_WINDOWS[gi]], s)
        w = jnp.where(lane_g == gi, POOL_WINDOWS[gi], w)
    count = jnp.minimum(count_pos, w).astype(F32)
    d = s / count - x
    return _dot(d.astype(BF), pw_ref[...]) * sc_ref[...]


def _pool_prompt_kernel(x_ref, halo_ref, pw_ref, sc_ref, o_ref, buf_ref, *, tp):
    i = pl.program_id(1)
    x = x_ref[...]
    buf_ref[0:16, :] = jnp.where(i > 0, halo_ref[...], 0.0)
    buf_ref[16:16 + tp, :] = x
    acc = x
    sums = {}
    for k in range(1, POOL_WINDOWS[-1]):
        acc = acc + buf_ref[16 - k:16 - k + tp, :]
        if k + 1 in POOL_WINDOWS:
            sums[k + 1] = acc
    pos1 = i * tp + lax.broadcasted_iota(jnp.int32, (tp, 1), 0) + 1
    o_ref[...] = _pool_mix(sums, x, pos1, pw_ref, sc_ref).astype(BF)


def _pool_prompt(xpool, pw, sc, layer, batch, t, tp):
    nt = t // tp
    return pl.pallas_call(
        functools.partial(_pool_prompt_kernel, tp=tp),
        grid=(batch, nt),
        in_specs=[pl.BlockSpec((tp, POOL_CH), lambda b, i: (b * nt + i, 0)),
                  pl.BlockSpec((16, POOL_CH), lambda b, i: (jnp.maximum((b * nt + i) * (tp // 16) - 1, 0), 0)),
                  _const_spec((POOL_CH, POOL_CH), layer), _const_spec((1, POOL_CH), layer)],
        out_specs=pl.BlockSpec((tp, POOL_CH), lambda b, i: (b * nt + i, 0)),
        out_shape=jax.ShapeDtypeStruct((batch * t, POOL_CH), BF),
        scratch_shapes=[pltpu.VMEM((tp + 16, POOL_CH), F32)],
        compiler_params=_cparams("parallel", "parallel"),
        name="pool_prompt",
    )(xpool, xpool, pw, sc)


SAMPLE_ROW_FIELDS = (("qc", 384), ("qr", 384), ("gate", LANES), ("mq", 384), ("nsa", 512), ("win", 256),
                     ("moba", 768), ("xpool", POOL_CH))


def _sample_kernel(pt_ref, rows_ref, winst_ref, poolst_ref,
                   wabk_ref, babk_ref, w2k_ref, wabv_ref, babv_ref, w2v_ref, gkc_ref,
                   ovl_ref, gexp_ref, esel_ref, emoba_ref, pw_ref, sc_ref, *rest, n_pages, page, past):
    nsa_pages = rest[:n_pages]
    moba_pages = rest[n_pages:2 * n_pages]
    onsa_ref, omoba_ref, ypool_ref, cmp_ref = rest[2 * n_pages:]
    del pt_ref
    fields, lane0 = {}, 0
    for name, width in SAMPLE_ROW_FIELDS:
        fields[name] = rows_ref[:, lane0:lane0 + width]
        lane0 += width
    row = lax.broadcasted_iota(jnp.int32, (8, 1), 0)
    lane = lax.broadcasted_iota(jnp.int32, (1, LANES), 1)
    lane_h = lane >> 6
    row_h = jnp.where(row >= NSA_GROUP, 1, 0) + jnp.where(row >= 2 * NSA_GROUP, 1, 0)
    row_g = row - row_h * NSA_GROUP
    live6 = row < NSA_HEADS
    n_groups = past // CMP_STRIDE

    def stack_q(q384):
        s = [q384[:, g * LANES:(g + 1) * LANES] for g in range(NSA_GROUP)]
        q = jnp.where(row_g == 0, s[0], jnp.where(row_g == 1, s[1], s[2]))
        return jnp.where(live6 & (row_h == lane_h), q, 0.0)

    def by_group(o):
        return [jnp.sum(jnp.where(live6 & (row == lane_h * NSA_GROUP + g), o, 0.0), axis=0, keepdims=True)
                for g in range(NSA_GROUP)]

    mq = fields["mq"]
    lane3 = lax.broadcasted_iota(jnp.int32, (1, 384), 1)
    mq8 = jnp.where(row == (lane3 >> 6), mq, 0.0)
    mq8b = mq8.astype(BF)
    n_blk = past // MOBA_BLOCK
    n_blk_pad = -(-n_blk // 16) * 16
    k_m = jnp.concatenate([pg[:, 0:384] for pg in moba_pages], axis=0).astype(BF)
    v_m = jnp.concatenate([pg[:, 384:768] for pg in moba_pages], axis=0).astype(BF)
    km = _dot(emoba_ref[0:n_blk_pad, :], k_m) * (1.0 / MOBA_BLOCK)
    km = jnp.concatenate([km.astype(BF), jnp.zeros((LANES - n_blk_pad, 384), BF)], axis=0)
    gate_m = _dot_nt(mq8b, km)
    s_m = _dot_nt(mq8b, k_m)
    cnt = jnp.zeros((8, LANES), F32)
    for d in range(1, n_blk):
        lower = pltpu.roll(gate_m, d, axis=1)
        upper = pltpu.roll(gate_m, LANES - d, axis=1)
        cnt = cnt + jnp.where((lane >= d) & (lower >= gate_m), 1.0, 0.0)
        cnt = cnt + jnp.where((lane + d < n_blk) & (upper > gate_m), 1.0, 0.0)
    pick_m = jnp.where((lane < n_blk) & (cnt < MOBA_TOPK), 1.0, 0.0)
    ok_m = _dot(pick_m.astype(BF), emoba_ref[...]) > 0.5
    moba_new = fields["moba"]
    s = jnp.where(ok_m, s_m, NEG)
    s_new = jnp.sum(mq8 * moba_new[:, 0:384], axis=1, keepdims=True)
    m = jnp.maximum(jnp.max(s, axis=1, keepdims=True), s_new)
    e = jnp.exp2(s - m)
    e_new = jnp.exp2(s_new - m)
    den = jnp.sum(e, axis=1, keepdims=True) + e_new
    o_m = (e_new * moba_new[:, 384:768] + _dot(e.astype(BF), v_m)) / den
    omoba_ref[...] = jnp.sum(jnp.where(row == (lane3 >> 6), o_m, 0.0), axis=0, keepdims=True).astype(BF)

    for pi, pg in enumerate(nsa_pages):
        cmp_ref[0, pi * page:(pi + 1) * page, :] = pg[:, 0:LANES]
        cmp_ref[1, pi * page:(pi + 1) * page, :] = pg[:, LANES:2 * LANES]
    kraw = _compress_tokens(lambda l: cmp_ref[0, pl.ds(l, n_groups, stride=CMP_STRIDE), :],
                            n_groups, wabk_ref, babk_ref, w2k_ref)
    kcmp = _head_norm(kraw, gkc_ref[...], _ones_bd())
    vcmp = _compress_tokens(lambda l: cmp_ref[1, pl.ds(l, n_groups, stride=CMP_STRIDE), :],
                            n_groups, wabv_ref, babv_ref, w2v_ref)

    qc8 = stack_q(fields["qc"])
    c_end = lax.broadcasted_iota(jnp.int32, (1, n_groups), 1) * CMP_STRIDE + CMP_LEN
    p_c = _softmax_rows(_dot3_nt(qc8, kcmp) * SCALE, c_end <= past + 1)
    o_cmp = _dot(p_c.astype(BF), vcmp.astype(BF))
    imp8 = _dot_sel(p_c, ovl_ref[...])
    imp = jnp.sum(jnp.where(live6 & (row_h == lane_h), imp8, 0.0), axis=0, keepdims=True)
    jblk = lane & (SEL_BLOCK - 1)
    own = past // SEL_BLOCK
    sel = _rank_select(jnp.broadcast_to(imp, (8, LANES)), jblk, own, own + 1, SEL_TOPN)[0:1]
    sel8 = jnp.where(live6 & (row_h == lane_h), sel, 0.0)

    qr8 = stack_q(fields["qr"])
    qr8b = qr8.astype(BF)
    nsa_new = fields["nsa"]
    picked = _dot(sel8.astype(BF), esel_ref[...]) > 0.5
    k_sel = jnp.concatenate([pg[:, 2 * LANES:3 * LANES] for pg in nsa_pages], axis=0).astype(BF)
    v_sel = jnp.concatenate([pg[:, 3 * LANES:4 * LANES] for pg in nsa_pages], axis=0).astype(BF)
    s = jnp.where(picked, _dot_nt(qr8b, k_sel), NEG)
    own_picked = jnp.sum(jnp.where(jblk == own, sel8, 0.0), axis=1, keepdims=True) > 0.5
    s_new = jnp.where(own_picked, jnp.sum(qr8 * nsa_new[:, 2 * LANES:3 * LANES], axis=1, keepdims=True), NEG)
    m = jnp.maximum(jnp.maximum(jnp.max(s, axis=1, keepdims=True), s_new), M_INIT)
    e = jnp.exp2(s - m)
    e_new = jnp.exp2(s_new - m)
    den = jnp.sum(e, axis=1, keepdims=True) + e_new
    acc = e_new * nsa_new[:, 3 * LANES:4 * LANES] + _dot(e.astype(BF), v_sel)
    o_sel = jnp.where(den > 0.0, acc / den, 0.0)

    win_new = fields["win"]
    n_win = winst_ref.shape[0]
    kpos_w = past - n_win + lax.broadcasted_iota(jnp.int32, (1, n_win), 1)
    ok_w = (past - kpos_w) < WIN
    s = jnp.where(ok_w, _dot_nt(qr8b, winst_ref[:, 0:LANES].astype(BF)), NEG)
    s_new = jnp.sum(qr8 * win_new[:, 0:LANES], axis=1, keepdims=True)
    m = jnp.maximum(jnp.max(s, axis=1, keepdims=True), s_new)
    e = jnp.exp2(s - m)
    e_new = jnp.exp2(s_new - m)
    den = jnp.sum(e, axis=1, keepdims=True) + e_new
    o_win = (_dot(e.astype(BF), winst_ref[:, LANES:2 * LANES].astype(BF)) + e_new * win_new[:, LANES:2 * LANES]) / den

    gate8 = jnp.broadcast_to(fields["gate"], (8, LANES))
    gh = gate8.astype(BF)
    gl = (gate8 - gh.astype(F32)).astype(BF)
    gexp = (_dot(gh, gexp_ref[...]) + _dot(gl, gexp_ref[...]))[0:1]
    branches = [by_group(o_cmp), by_group(o_sel), by_group(o_win)]
    for g in range(NSA_GROUP):
        out = jnp.zeros((1, LANES), F32)
        for br in range(3):
            out = out + gexp[:, (g * 3 + br) * LANES:(g * 3 + br + 1) * LANES] * branches[br][g]
        onsa_ref[:, g * LANES:(g + 1) * LANES] = out.astype(BF)

    x = fields["xpool"]
    hist = poolst_ref[...]
    hrow = lax.broadcasted_iota(jnp.int32, (POOL_HIST, 1), 0)
    sums = {w: x + jnp.sum(jnp.where(hrow >= POOL_HIST - (w - 1), hist, 0.0), axis=0, keepdims=True)
            for w in POOL_WINDOWS}
    y = _pool_mix(sums, x, jnp.full((1, 1), past + 1, jnp.int32), pw_ref, sc_ref)
    ypool_ref[...] = y.astype(BF)


def _sample(page_table, rows, winst, poolst, cache_nsa, cache_moba, cw, consts, pw, sc, layer, past):
    n_seq, n_pages = page_table.shape
    page = cache_nsa.shape[2]
    per_seq = lambda w: pl.BlockSpec((None, 1, w), lambda i, pt: (i, 0, 0))
    lconst = lambda shape: pl.BlockSpec((None,) + tuple(shape), lambda i, pt: (layer,) + (0,) * len(shape),
                                        pipeline_mode=pl.Buffered(1))
    const = lambda a: pl.BlockSpec(a.shape, lambda i, pt: (0,) * a.ndim, pipeline_mode=pl.Buffered(1))
    cmp_specs = [lconst(s) for s in _cmp_weight_shapes()]

    def page_specs(width):
        def one(pj):
            return pl.BlockSpec((None, None, page, width), lambda i, pt: (layer, pt[i, pj], 0, 0))
        return [one(pj) for pj in range(n_pages)]

    row_all = jnp.concatenate([rows[name].astype(F32) for name, _ in SAMPLE_ROW_FIELDS], axis=1)
    row_w = row_all.shape[1]
    in_specs = ([per_seq(row_w),
                 pl.BlockSpec((None, None, winst.shape[2], 256), lambda i, pt: (layer, i, 0, 0)),
                 pl.BlockSpec((None, None, POOL_HIST, POOL_CH), lambda i, pt: (layer, i, 0, 0))]
                + cmp_specs + cmp_specs + [lconst((1, LANES))]
                + [const(consts["ovl_s"]), const(consts["gexp"]), const(consts["esel"]), const(consts["emoba"]),
                   lconst((POOL_CH, POOL_CH)), lconst((1, POOL_CH))]
                + page_specs(512) + page_specs(768))
    grid_spec = pltpu.PrefetchScalarGridSpec(
        num_scalar_prefetch=1, grid=(n_seq,), in_specs=in_specs,
        out_specs=[per_seq(384), per_seq(384), per_seq(POOL_CH)],
        scratch_shapes=[pltpu.VMEM((2, past, LANES), F32)])
    outs = pl.pallas_call(
        functools.partial(_sample_kernel, n_pages=n_pages, page=page, past=past),
        grid_spec=grid_spec,
        out_shape=[jax.ShapeDtypeStruct((n_seq, 1, 384), BF), jax.ShapeDtypeStruct((n_seq, 1, 384), BF),
                   jax.ShapeDtypeStruct((n_seq, 1, POOL_CH), BF)],
        compiler_params=_cparams("parallel"),
        name="sample_step",
    )(page_table, row_all.reshape(n_seq, 1, row_w), winst, poolst,
      cw["wabk"], cw["babk"], cw["w2k"], cw["wabv"], cw["babv"], cw["w2v"], cw["gkc"],
      consts["ovl_s"], consts["gexp"], consts["esel"], consts["emoba"], pw, sc,
      *([cache_nsa] * n_pages), *([cache_moba] * n_pages))
    return [o.reshape(n_seq, o.shape[2]) for o in outs]


def _combine_ffn_kernel(x_ref, yp_ref, on_ref, om_ref, w_ref, g_ref, wg_ref, wu_ref, wd_ref, o_ref, *, f_chunk):
    mix = jnp.concatenate([yp_ref[...], on_ref[...], om_ref[...]], axis=1)
    o_ref[...] = _ffn_body(x_ref[...] + _dot(mix, w_ref[...]), g_ref[...], wg_ref, wu_ref, wd_ref, f_chunk)


def _combine_ffn(x, ypool, onsa, omoba, w_out, g, wg, wu, wd, layer):
    n, d = x.shape
    d_ff = wg.shape[2]
    tm = _row_tile(n)
    row = lambda w: pl.BlockSpec((tm, w), lambda i: (i, 0))
    return pl.pallas_call(
        functools.partial(_combine_ffn_kernel, f_chunk=_f_chunk(d_ff)),
        grid=(n // tm,),
        in_specs=[row(d), row(POOL_CH), row(384), row(384), _const_spec((POOL_CH + 768, d), layer),
                  _const_spec((1, d), layer),
                  _const_spec((d, d_ff), layer), _const_spec((d, d_ff), layer), _const_spec((d_ff, d), layer)],
        out_specs=row(d),
        out_shape=jax.ShapeDtypeStruct((n, d), F32),
        compiler_params=_cparams("parallel"),
        name="combine_ffn",
    )(x, ypool, onsa, omoba, w_out, g, wg, wu, wd)


def _in_col_perm():
    cols = list(range(0, POOL_CH))
    for g in range(NSA_GROUP):
        for h in range(NSA_KV_HEADS):
            base = POOL_CH + (h * NSA_GROUP + g) * HEAD_DIM
            cols += range(base, base + HEAD_DIM)
    nsa_end = POOL_CH + NSA_HEADS * HEAD_DIM
    kv_end = nsa_end + 6 * NSA_KV_HEADS * HEAD_DIM
    cols += range(nsa_end, kv_end)
    cols += range(kv_end + GATE_COLS, kv_end + GATE_COLS + 3 * MOBA_HEADS * HEAD_DIM)
    gate_cols = list(range(kv_end, kv_end + GATE_COLS))
    return np.array(cols, np.int32), np.array(gate_cols, np.int32)


def _out_row_perm():
    rows = list(range(0, POOL_CH))
    for g in range(NSA_GROUP):
        for h in range(NSA_KV_HEADS):
            base = POOL_CH + (h * NSA_GROUP + g) * HEAD_DIM
            rows += range(base, base + HEAD_DIM)
    rows += range(POOL_CH + NSA_HEADS * HEAD_DIM, POOL_CH + NSA_HEADS * HEAD_DIM + MOBA_HEADS * HEAD_DIM)
    return np.array(rows, np.int32)


def _overlap(n_cmp_rows, n_cmp_valid, n_blocks):
    n = np.arange(n_cmp_rows)
    c_start = n * CMP_STRIDE
    c_end = c_start + CMP_LEN
    jb = np.arange(SEL_BLOCK)
    ov = (c_start[:, None] < (jb[None, :] + 1) * SEL_BLOCK) & (c_end[:, None] > jb[None, :] * SEL_BLOCK)
    ov &= (n[:, None] < n_cmp_valid) & (jb[None, :] < n_blocks)
    return ov.astype(np.float32)


def _constants(t, past):
    g_p = t // CMP_STRIDE
    ov = _overlap(g_p, g_p - 1, t // SEL_BLOCK)
    ovl_p = np.zeros((2 * g_p, LANES), np.float32)
    ovl_p[:g_p, :SEL_BLOCK] = ov
    ovl_p[g_p:, SEL_BLOCK:] = ov
    g_s = past // CMP_STRIDE
    ov_s = _overlap(g_s, g_s - 1, past // SEL_BLOCK + 1)
    ovl_s = np.concatenate([ov_s, ov_s], axis=1)
    lane = np.arange(LANES)
    gexp = np.zeros((LANES, 9 * LANES), np.float32)
    for g in range(NSA_GROUP):
        for br in range(3):
            src = (lane // HEAD_DIM) * 9 + g * 3 + br
            gexp[src, (g * 3 + br) * LANES + lane] = 1.0
    kp = np.arange(past)
    esel = ((lane[:, None] % SEL_BLOCK) == (kp[None, :] // SEL_BLOCK)).astype(np.float32)
    emoba = (lane[:, None] == (kp[None, :] // MOBA_BLOCK)).astype(np.float32)
    kt = np.arange(t)
    et_sel = ((kt[:, None] // SEL_BLOCK) == (lane[None, :] % SEL_BLOCK)).astype(np.float32)
    et_moba = ((kt[:, None] // MOBA_BLOCK) == (lane[None, :] % 16)).astype(np.float32)
    as_bf = lambda a: jnp.asarray(a, BF)
    return {"ovlt_p": as_bf(ovl_p.T), "ovl_s": as_bf(ovl_s), "gexp": as_bf(gexp), "esel": as_bf(esel),
            "emoba": as_bf(emoba), "et_sel": as_bf(et_sel), "et_moba": as_bf(et_moba)}


def _rope_tables(pos):
    half = ROPE_DIM // 2
    inv_freq = ROPE_THETA ** (-jnp.arange(half, dtype=F32) / half)
    ang = pos.astype(F32)[:, None] * inv_freq[None, :]
    cos, sin, zero = jnp.cos(ang), jnp.sin(ang), jnp.zeros_like(ang)
    rest = HEAD_DIM - ROPE_DIM
    c = jnp.concatenate([cos, cos, jnp.ones((ang.shape[0], rest), F32)], axis=1)
    a = jnp.concatenate([-sin, zero, jnp.zeros((ang.shape[0], rest), F32)], axis=1)
    b = jnp.concatenate([zero, sin, jnp.zeros((ang.shape[0], rest), F32)], axis=1)
    tile2 = lambda m: jnp.concatenate([m, m], axis=1)
    return tile2(c), tile2(a), tile2(b)


def _cmp_weights(cmp_pos, cmp_w1, cmp_w2, g_nsa_kc):
    bd = lambda w: jnp.concatenate([jnp.concatenate([w, jnp.zeros_like(w)], axis=-1),
                                    jnp.concatenate([jnp.zeros_like(w), w], axis=-1)], axis=-2)
    out = {}
    depth = cmp_pos.shape[0]
    for i, name in enumerate(("k", "v")):
        pos2 = jnp.concatenate([cmp_pos[:, i], cmp_pos[:, i]], axis=-1)
        w1 = bd(cmp_w1[:, i])
        out["w2" + name] = bd(cmp_w2[:, i]).astype(BF)
        halves = [w1[:, s:s + CMP_STRIDE].reshape(depth, CMP_STRIDE * LANES, LANES) for s in (0, CMP_STRIDE)]
        out["wab" + name] = jnp.concatenate(halves, axis=-1).astype(BF)
        bias = [jnp.einsum("xlc,xlce->xe", pos2[:, s:s + CMP_STRIDE], w1[:, s:s + CMP_STRIDE],
                           precision=lax.Precision.HIGHEST) for s in (0, CMP_STRIDE)]
        out["bab" + name] = jnp.concatenate(bias, axis=-1)[:, None, :]
    out["gkc"] = jnp.concatenate([g_nsa_kc, g_nsa_kc], axis=-1)[:, None, :]
    return out


def kernel(x_prompt, x_sample, cache_nsa_kv, cache_moba_kv, state_nsa_win, state_pool, page_table, g_ffa, w_ffa_gate, w_ffa_up, w_ffa_down, g_mix, w_in, w_out, pool_w, pool_scale, g_nsa_q, g_nsa_kc, g_nsa_ks, g_nsa_kw, cmp_pos, cmp_w1, cmp_w2, g_moba_q, g_moba_k, g_ffb, w_ffb_gate, w_ffb_up, w_ffb_down):
    batch, t, d = x_prompt.shape
    n_seq = x_sample.shape[0]
    depth, n_pool, page = cache_nsa_kv.shape[:3]
    n_pages = page_table.shape[1]
    past = n_pages * page
    n_p = batch * t
    n_win = state_nsa_win.shape[2]
    assert x_sample.shape[1] == 1 and t % MOBA_BLOCK == 0 and past % MOBA_BLOCK == 0 and n_win == WIN

    tp = _row_tile(t)
    tq_nsa = 256
    tq_moba = min(2 * MOBA_BLOCK, t)

    col_perm, gate_cols = _in_col_perm()
    w_in_p = jnp.concatenate([w_in[:, :, col_perm], w_in[:, :, gate_cols],
                              jnp.zeros((depth, d, LANES - GATE_COLS), w_in.dtype)], axis=-1).astype(BF)
    w_out_p = w_out[:, _out_row_perm(), :].astype(BF)
    bf = lambda w: w.astype(BF)
    wa = (bf(w_ffa_gate), bf(w_ffa_up), bf(w_ffa_down))
    wb = (bf(w_ffb_gate), bf(w_ffb_up), bf(w_ffb_down))
    tile2 = lambda g: jnp.concatenate([g, g], axis=-1)
    gains = jnp.stack([tile2(g_nsa_q), tile2(g_nsa_ks), tile2(g_nsa_kw), tile2(g_moba_q), tile2(g_moba_k)]
                      + [jnp.ones((depth, LANES), F32)] * 3, axis=1)
    cw = _cmp_weights(cmp_pos, cmp_w1, cmp_w2, g_nsa_kc)
    eye = jnp.eye(len(POOL_WINDOWS), dtype=pool_w.dtype)
    pw_bd = jnp.einsum("lgij,gh->lgihj", pool_w, eye).reshape(depth, POOL_CH, POOL_CH).astype(BF)
    sc = pool_scale[:, None, :]
    consts = _constants(t, past)
    rope_p = _rope_tables(jnp.arange(t, dtype=jnp.int32))
    rope_s = _rope_tables(jnp.full((n_seq,), past, jnp.int32))
    g_a, g_m, g_b = g_ffa[:, None, :], g_mix[:, None, :], g_ffb[:, None, :]

    cache_nsa = cache_nsa_kv.reshape(depth, n_pool, page, 512)
    cache_moba = cache_moba_kv.reshape(depth, n_pool, page, 768)
    winst = state_nsa_win.reshape(depth, n_seq, n_win, 256)

    xp = x_prompt.reshape(n_p, d)
    xs = x_sample.reshape(n_seq, d)
    names = ("xpool", "qc", "qr", "nsa", "nsab", "win", "winb", "mq", "moba", "mobab", "gate")
    outs = [[] for _ in range(8)]
    for l in range(depth):
        xp = _ffn(xp, g_a, *wa, l)
        xs = _ffn(xs, g_a, *wa, l)
        pp = dict(zip(names, _prep(xp, g_m, w_in_p, gains, rope_p, l)))
        ps = dict(zip(names, _prep(xs, g_m, w_in_p, gains, rope_s, l)))

        kcmp, vcmp = _compress_prompt(pp["nsa"], cw, l, batch, t)
        onsa_p = _nsa_prompt(pp["qc"], pp["qr"], pp["gate"], kcmp, vcmp, pp["nsab"], pp["winb"], consts["ovlt_p"],
                             consts["gexp"], consts["et_sel"], batch, t, tq_nsa)
        omoba_p = _moba_prompt(pp["mq"], pp["mobab"], consts["et_moba"], batch, t, tq_moba)
        ypool_p = _pool_prompt(pp["xpool"], pw_bd, sc, l, batch, t, tp)
        onsa_s, omoba_s, ypool_s = _sample(page_table, ps, winst, state_pool, cache_nsa, cache_moba, cw, consts,
                                           pw_bd, sc, l, past)

        xp = _combine_ffn(xp, ypool_p, onsa_p, omoba_p, w_out_p, g_b, *wb, l)
        xs = _combine_ffn(xs, ypool_s, onsa_s, omoba_s, w_out_p, g_b, *wb, l)

        outs[0].append(pp["nsa"].reshape(batch, t, 4, NSA_KV_HEADS, HEAD_DIM))
        outs[1].append(ps["nsa"].reshape(n_seq, 1, 4, NSA_KV_HEADS, HEAD_DIM))
        outs[2].append(pp["moba"].reshape(batch, t, 2, MOBA_HEADS, HEAD_DIM))
        outs[3].append(ps["moba"].reshape(n_seq, 1, 2, MOBA_HEADS, HEAD_DIM))
        outs[4].append(pp["win"].reshape(batch, t, 2, NSA_KV_HEADS, HEAD_DIM)[:, t - min(WIN, t):])
        outs[5].append(ps["win"].reshape(n_seq, 1, 2, NSA_KV_HEADS, HEAD_DIM))
        outs[6].append(pp["xpool"].reshape(batch, t, POOL_CH)[:, t - POOL_HIST:])
        outs[7].append(ps["xpool"].reshape(n_seq, 1, POOL_CH))

    stacked = [jnp.stack(o, axis=0) for o in outs]
    stacked[5] = jnp.concatenate([state_nsa_win[:, :, 1:], stacked[5]], axis=2)
    stacked[7] = jnp.concatenate([state_pool[:, :, 1:], stacked[7]], axis=2)
    return (xp.reshape(batch, t, d), xs.reshape(n_seq, 1, d), *stacked)
```
